```python
import math
import jax
import jax.numpy as jnp
from jax import lax
import numpy as np

D_MODEL = 2048
BATCH = 4
SEQ = 4096
DEPTH = 1
DEC_BATCH = 32
DEC_SEQ = 8
PAST_LEN = 16384
PAGE_SIZE = 128

HEAD_DIM = 128
N_HEADS = D_MODEL // 256
ATTN_WIDTH = N_HEADS * HEAD_DIM
CONV_CH = D_MODEL // 2
CONV_WIDTH = 31
MOBA_BLOCK = 256
MOBA_TOPK = 3
Q_BLOCK = 32
NUM_BUCKETS = 32
MAX_DISTANCE = 128
N_GROUPS = 4
EXPERTS_PER_GROUP = 8
N_EXPERTS = N_GROUPS * EXPERTS_PER_GROUP
TOPK_E = 2
D_EXPERT = D_MODEL // 4
MOE_BLOCK = 128
EPS = 1e-6
NEG = -1e30

kernel_name = 'moba_conformer_hmoe_decoder_step'


def rms_norm(x, g):
    xf = x.astype(jnp.float32)
    y = xf * lax.rsqrt(jnp.mean(xf * xf, axis=-1, keepdims=True) + EPS)
    return (y * g.astype(jnp.float32)).astype(x.dtype)


def layer_norm(x, g, b):
    xf = x.astype(jnp.float32)
    mu = jnp.mean(xf, axis=-1, keepdims=True)
    var = jnp.mean(jnp.square(xf - mu), axis=-1, keepdims=True)
    y = (xf - mu) * lax.rsqrt(var + EPS) * g.astype(jnp.float32) + b.astype(jnp.float32)
    return y.astype(x.dtype)


def t5_bucket(dist):
    n = jnp.maximum(dist, 0)
    max_exact = NUM_BUCKETS // 2
    nf = jnp.maximum(n, 1).astype(jnp.float32)
    large = max_exact + (jnp.log(nf / max_exact) / math.log(MAX_DISTANCE / max_exact)
                         * (NUM_BUCKETS - max_exact)).astype(jnp.int32)
    large = jnp.minimum(large, NUM_BUCKETS - 1)
    return jnp.where(n < max_exact, n, large)


def moba_select(q, q_pos, means):
    B, H, Qc, _ = q.shape
    nb = means.shape[2]
    topk = min(MOBA_TOPK, nb)
    s = jnp.einsum('bhqd,bhnd->bhqn', q.astype(jnp.float32), means)
    own = q_pos // MOBA_BLOCK
    fully_past = jnp.arange(nb)[None, :] < own[:, None]
    s = jnp.where(fully_past, s, NEG)
    _, top_idx = lax.top_k(s, topk)
    valid_top = jnp.arange(topk)[None, :] < own[:, None]
    idx = jnp.concatenate([top_idx.astype(jnp.int32),
                           jnp.broadcast_to(own[None, None, :, None], (B, H, Qc, 1)).astype(jnp.int32)], axis=-1)
    valid = jnp.concatenate([jnp.broadcast_to(valid_top, (B, H, Qc, topk)),
                             jnp.ones((B, H, Qc, 1), dtype=bool)], axis=-1)
    return idx, valid


def moba_attend(q, q_pos, kb, vb, idx, valid, rel_bias):
    B, H, Qc, dh = q.shape
    kpos = idx[..., None] * MOBA_BLOCK + jnp.arange(MOBA_BLOCK, dtype=jnp.int32)
    dist = q_pos[None, None, :, None, None] - kpos
    hi = jnp.arange(H)[None, :, None, None, None]
    bias = rel_bias.T.astype(jnp.float32)[hi, t5_bucket(dist)]
    logits = jnp.einsum('bhqd,bhqnkd->bhqnk', q, kb).astype(jnp.float32) * (HEAD_DIM ** -0.5) + bias
    mask = valid[..., None] & (dist >= 0)
    logits = jnp.where(mask, logits, NEG)
    p = jax.nn.softmax(logits.reshape(B, H, Qc, -1), axis=-1).reshape(logits.shape)
    return jnp.einsum('bhqnk,bhqnkd->bhqd', p.astype(vb.dtype), vb)


def moba_mixer(q, q_pos, means, gather_fn, rel_bias, q_block):
    B, T, H, dh = q.shape
    nc = T // q_block
    qc = q.reshape(B, nc, q_block, H, dh).transpose(1, 0, 3, 2, 4)
    pc = q_pos.reshape(nc, q_block)

    def step(args):
        qi, pi = args
        idx, valid = moba_select(qi, pi, means)
        kb, vb = gather_fn(idx)
        return moba_attend(qi, pi, kb, vb, idx, valid, rel_bias)

    out = lax.map(step, (qc, pc))
    return out.transpose(1, 0, 3, 2, 4).reshape(B, T, H * dh)


def prompt_moba(q, k, v, rel_bias):
    B, T, H, dh = q.shape
    nb = -(-T // MOBA_BLOCK)
    pad = nb * MOBA_BLOCK - T

    def blocks(a):
        a = jnp.pad(a, ((0, 0), (0, pad), (0, 0), (0, 0)))
        return a.reshape(B, nb, MOBA_BLOCK, H, dh).transpose(0, 3, 1, 2, 4)

    kb_all, vb_all = blocks(k), blocks(v)
    means = jnp.mean(kb_all.astype(jnp.float32), axis=3)
    bi = jnp.arange(B)[:, None, None, None]
    hi = jnp.arange(H)[None, :, None, None]

    def gather(idx):
        return kb_all[bi, hi, idx], vb_all[bi, hi, idx]

    q_pos = jnp.arange(T, dtype=jnp.int32)
    return moba_mixer(q, q_pos, means, gather, rel_bias, Q_BLOCK)


def sample_moba(q, k, v, cache_k, cache_v, page_table, rel_bias):
    B, S, H, dh = q.shape
    n_pages = page_table.shape[1]
    nb = -(-(PAST_LEN + S) // MOBA_BLOCK)
    page_sums = jnp.sum(cache_k.astype(jnp.float32), axis=1)
    page_block = (jnp.arange(n_pages) * PAGE_SIZE) // MOBA_BLOCK
    new_pos = PAST_LEN + jnp.arange(S, dtype=jnp.int32)
    sums = jnp.zeros((B, nb, H, dh), jnp.float32)
    sums = sums.at[:, page_block].add(page_sums[page_table])
    sums = sums.at[:, new_pos // MOBA_BLOCK].add(k.astype(jnp.float32))
    means = (sums / MOBA_BLOCK).transpose(0, 2, 1, 3)
    bi = jnp.arange(B)[:, None, None, None, None]
    hi = jnp.arange(H)[None, :, None, None, None]

    def gather(idx):
        pos = idx[..., None] * MOBA_BLOCK + jnp.arange(MOBA_BLOCK, dtype=jnp.int32)
        phys = page_table[bi, jnp.clip(pos // PAGE_SIZE, 0, n_pages - 1)]
        off = pos % PAGE_SIZE
        row = jnp.clip(pos - PAST_LEN, 0, S - 1)
        in_past = (pos < PAST_LEN)[..., None]
        kb = jnp.where(in_past, cache_k[phys, off, hi], k[bi, row, hi])
        vb = jnp.where(in_past, cache_v[phys, off, hi], v[bi, row, hi])
        return kb, vb

    return moba_mixer(q, new_pos, means, gather, rel_bias, 1)


def glu(u):
    a, b = jnp.split(u, 2, axis=-1)
    return a * jax.nn.sigmoid(b)


def conformer_conv(u_ext, conv_w, conv_b, ln_g, ln_b, w_conv_out):
    C = u_ext.shape[-1]
    h = lax.conv_general_dilated(u_ext, conv_w[:, None, :].astype(u_ext.dtype), window_strides=(1,),
                                 padding='VALID', dimension_numbers=('NWC', 'WIO', 'NWC'),
                                 feature_group_count=C) + conv_b
    h = layer_norm(h, ln_g, ln_b)
    return jax.nn.silu(h) @ w_conv_out


def route(xn, router_g_w, router_g_b, router_e_w, router_e_b):
    N = xn.shape[0]
    gl = (xn @ router_g_w + router_g_b).astype(jnp.float32)
    g_idx = jnp.argmax(gl, axis=-1)
    p_g = jnp.take_along_axis(jax.nn.softmax(gl, axis=-1), g_idx[:, None], axis=-1)
    el = (xn @ router_e_w + router_e_b).astype(jnp.float32).reshape(N, N_GROUPS, EXPERTS_PER_GROUP)
    el_g = jnp.take_along_axis(el, g_idx[:, None, None], axis=1)[:, 0]
    top_v, top_i = lax.top_k(el_g, TOPK_E)
    p_e = jax.nn.softmax(top_v, axis=-1)
    expert = (g_idx[:, None] * EXPERTS_PER_GROUP + top_i).astype(jnp.int32)
    return expert, p_g * p_e


def moe_ffn(x, expert, weight, w1, w3, w2):
    N, D = x.shape
    S = N * TOPK_E
    flat_e = expert.reshape(S)
    flat_tok = jnp.arange(S, dtype=jnp.int32) // TOPK_E
    order = jnp.argsort(flat_e)
    se, stok = flat_e[order], flat_tok[order]
    sw = weight.reshape(S)[order].astype(x.dtype)
    counts = jnp.zeros((N_EXPERTS,), jnp.int32).at[flat_e].add(1)
    pcounts = (counts + MOE_BLOCK - 1) // MOE_BLOCK * MOE_BLOCK
    start = jnp.cumsum(counts) - counts
    pend = jnp.cumsum(pcounts)
    pstart = pend - pcounts
    dest = pstart[se] + (jnp.arange(S, dtype=jnp.int32) - start[se])
    n_blocks = (S + N_EXPERTS * (MOE_BLOCK - 1) + MOE_BLOCK - 1) // MOE_BLOCK
    n_rows = n_blocks * MOE_BLOCK
    row_tok = jnp.full((n_rows,), N, jnp.int32).at[dest].set(stok)
    block_e = jnp.minimum(jnp.searchsorted(pend, jnp.arange(n_blocks, dtype=jnp.int32) * MOE_BLOCK,
                                           side='right'), N_EXPERTS - 1).astype(jnp.int32)
    xpad = jnp.concatenate([x, jnp.zeros((1, D), x.dtype)], axis=0)
    xb = xpad[row_tok].reshape(n_blocks, MOE_BLOCK, D)

    def expert_block(args):
        xblk, e = args
        h = jax.nn.silu(xblk @ w1[e]) * (xblk @ w3[e])
        return h @ w2[e]

    yb = lax.map(expert_block, (xb, block_e)).reshape(n_rows, D)
    return jax.ops.segment_sum(yb[dest] * sw[:, None], stok, num_segments=N)


def trunk_layer(x, conv_prev, attend_fn, norm1_g, w_in, w_attn_out, conv_w, conv_b, conv_ln_g, conv_ln_b,
                w_conv_out, w_gate, b_gate, w_out, norm2_g, router_g_w, router_g_b, router_e_w, router_e_b,
                moe_w1, moe_w3, moe_w2):
    B, T, _ = x.shape
    xn = rms_norm(x, norm1_g)
    proj = xn @ w_in
    q, k, v, u = jnp.split(proj, [ATTN_WIDTH, 2 * ATTN_WIDTH, 3 * ATTN_WIDTH], axis=-1)
    q = q.reshape(B, T, N_HEADS, HEAD_DIM)
    k = k.reshape(B, T, N_HEADS, HEAD_DIM)
    v = v.reshape(B, T, N_HEADS, HEAD_DIM)
    y_attn = attend_fn(q, k, v) @ w_attn_out
    u_ext = jnp.concatenate([conv_prev.astype(u.dtype), glu(u)], axis=1)
    y_conv = conformer_conv(u_ext, conv_w, conv_b, conv_ln_g, conv_ln_b, w_conv_out)
    g_attn, g_conv = jnp.split(jax.nn.sigmoid(xn @ w_gate + b_gate), 2, axis=-1)
    x = x + (g_attn * y_attn + g_conv * y_conv) @ w_out
    xn2 = rms_norm(x, norm2_g).reshape(B * T, D_MODEL)
    expert, weight = route(xn2, router_g_w, router_g_b, router_e_w, router_e_b)
    x = x + moe_ffn(xn2, expert, weight, moe_w1, moe_w3, moe_w2).reshape(B, T, D_MODEL)
    return x, k, v, u_ext[:, -(CONV_WIDTH - 1):]


def setup_inputs(seed: int = 0) -> dict:
    key = jax.random.key(seed)
    ks = jax.random.split(key, 28)
    n_pages = PAST_LEN // PAGE_SIZE
    n_pool = (DEC_BATCH * n_pages * 5) // 4
    f32 = jnp.float32

    def nrm(k, shape, scale):
        return jax.random.normal(k, shape, f32) * scale

    in_cols = 3 * ATTN_WIDTH + 2 * CONV_CH
    page_table = jax.random.permutation(ks[5], n_pool)[:DEC_BATCH * n_pages].reshape(DEC_BATCH, n_pages).astype(jnp.int32)
    return {
        'x_prompt': nrm(ks[0], (BATCH, SEQ, D_MODEL), 1.0),
        'x_sample': nrm(ks[1], (DEC_BATCH, DEC_SEQ, D_MODEL), 1.0),
        'cache_k': nrm(ks[2], (DEPTH, n_pool, PAGE_SIZE, N_HEADS, HEAD_DIM), 1.0),
        'cache_v': nrm(ks[3], (DEPTH, n_pool, PAGE_SIZE, N_HEADS, HEAD_DIM), 1.0),
        'state_conv': nrm(ks[4], (DEPTH, DEC_BATCH, CONV_WIDTH - 1, CONV_CH), 0.5),
        'page_table': page_table,
        'rel_bias': nrm(ks[6], (NUM_BUCKETS, N_HEADS), 0.5),
        'norm1_g': 1.0 + nrm(ks[7], (DEPTH, D_MODEL), 0.05),
        'w_in': nrm(ks[8], (DEPTH, D_MODEL, in_cols), D_MODEL ** -0.5),
        'w_attn_out': nrm(ks[9], (DEPTH, ATTN_WIDTH, D_MODEL), ATTN_WIDTH ** -0.5),
        'conv_w': nrm(ks[10], (DEPTH, CONV_WIDTH, CONV_CH), CONV_WIDTH ** -0.5),
        'conv_b': nrm(ks[11], (DEPTH, CONV_CH), 0.02),
        'conv_ln_g': 1.0 + nrm(ks[12], (DEPTH, CONV_CH), 0.05),
        'conv_ln_b': nrm(ks[13], (DEPTH, CONV_CH), 0.02),
        'w_conv_out': nrm(ks[14], (DEPTH, CONV_CH, D_MODEL), CONV_CH ** -0.5),
        'w_gate': nrm(ks[15], (DEPTH, D_MODEL, 2 * D_MODEL), D_MODEL ** -0.5),
        'b_gate': nrm(ks[16], (DEPTH, 2 * D_MODEL), 0.02),
        'w_out': nrm(ks[17], (DEPTH, D_MODEL, D_MODEL), D_MODEL ** -0.5),
        'norm2_g': 1.0 + nrm(ks[18], (DEPTH, D_MODEL), 0.05),
        'router_g_w': nrm(ks[19], (DEPTH, D_MODEL, N_GROUPS), D_MODEL ** -0.5),
        'router_g_b': nrm(ks[20], (DEPTH, N_GROUPS), 0.01),
        'router_e_w': nrm(ks[21], (DEPTH, D_MODEL, N_EXPERTS), D_MODEL ** -0.5),
        'router_e_b': nrm(ks[22], (DEPTH, N_EXPERTS), 0.01),
        'moe_w1': nrm(ks[23], (DEPTH, N_EXPERTS, D_MODEL, D_EXPERT), D_MODEL ** -0.5),
        'moe_w3': nrm(ks[24], (DEPTH, N_EXPERTS, D_MODEL, D_EXPERT), D_MODEL ** -0.5),
        'moe_w2': nrm(ks[25], (DEPTH, N_EXPERTS, D_EXPERT, D_MODEL), D_EXPERT ** -0.5),
        'norm_f_g': 1.0 + nrm(ks[26], (D_MODEL,), 0.05),
    }


def reference(x_prompt, x_sample, cache_k, cache_v, state_conv, page_table, rel_bias, norm1_g, w_in,
              w_attn_out, conv_w, conv_b, conv_ln_g, conv_ln_b, w_conv_out, w_gate, b_gate, w_out, norm2_g,
              router_g_w, router_g_b, router_e_w, router_e_b, moe_w1, moe_w3, moe_w2, norm_f_g):
    xp, xs = x_prompt, x_sample
    kp_l, vp_l, cp_l, ks_l, vs_l, cs_l = [], [], [], [], [], []
    for l in range(DEPTH):
        lw = (norm1_g[l], w_in[l], w_attn_out[l], conv_w[l], conv_b[l], conv_ln_g[l], conv_ln_b[l],
              w_conv_out[l], w_gate[l], b_gate[l], w_out[l], norm2_g[l], router_g_w[l], router_g_b[l],
              router_e_w[l], router_e_b[l], moe_w1[l], moe_w3[l], moe_w2[l])
        ck, cv = cache_k[l], cache_v[l]
        conv0 = jnp.zeros((xp.shape[0], CONV_WIDTH - 1, CONV_CH), xp.dtype)
        xp, kp, vp, cp = trunk_layer(xp, conv0, lambda q, k, v: prompt_moba(q, k, v, rel_bias), *lw)
        xs, ks_, vs_, cs = trunk_layer(
            xs, state_conv[l], lambda q, k, v: sample_moba(q, k, v, ck, cv, page_table, rel_bias), *lw)
        kp_l.append(kp)
        vp_l.append(vp)
        cp_l.append(cp)
        ks_l.append(ks_)
        vs_l.append(vs_)
        cs_l.append(cs)
    y_prompt = rms_norm(xp, norm_f_g)
    y_sample = rms_norm(xs, norm_f_g)
    return (y_prompt, y_sample, jnp.stack(kp_l), jnp.stack(vp_l), jnp.stack(cp_l),
            jnp.stack(ks_l), jnp.stack(vs_l), jnp.stack(cs_l))
```

```python
import functools
import math

import jax
import jax.numpy as jnp
import numpy as np
from jax import lax
from jax.experimental import pallas as pl
from jax.experimental.pallas import tpu as pltpu

D_MODEL = 2048
HEAD_DIM = 128
N_HEADS = 8
ATTN_WIDTH = N_HEADS * HEAD_DIM
CONV_CH = 1024
CONV_WIDTH = 31
MOBA_BLOCK = 256
MOBA_TOPK = 3
NUM_BUCKETS = 32
MAX_DISTANCE = 128
N_GROUPS = 4
EXPERTS_PER_GROUP = 8
N_EXPERTS = N_GROUPS * EXPERTS_PER_GROUP
D_EXPERT = 512
EPS = 1e-6
NEG = -1e30

LANES = 128
ROW_TILE = 512
CONV_HALO = 32
MOE_ROWS = 256
VMEM_LIMIT = 56 * 1024 * 1024

F32 = jnp.float32
BF16 = jnp.bfloat16
_NT = (((1,), (1,)), ((), ()))


def _params(sem, vmem=VMEM_LIMIT):
    return pltpu.CompilerParams(dimension_semantics=sem, vmem_limit_bytes=vmem)


def _sigmoid(x):
    return 1.0 / (1.0 + jnp.exp(-x))


def _proj_kernel(x_ref, g_ref, w_ref, xn_ref, q_ref, k_ref, v_ref, u_ref, ks_ref, xn_scr):
    j = pl.program_id(1)

    @pl.when(j == 0)
    def _():
        x = x_ref[...]
        y = x * lax.rsqrt(jnp.mean(x * x, axis=-1, keepdims=True) + EPS) * g_ref[...]
        yb = y.astype(BF16)
        xn_scr[...] = yb
        xn_ref[...] = yb

    acc = jnp.dot(xn_scr[...], w_ref[...], preferred_element_type=F32)

    @pl.when(j == 0)
    def _():
        q_ref[...] = acc

    @pl.when(j == 1)
    def _():
        k_ref[...] = acc
        tm = acc.shape[0]
        ks_ref[0] = acc.reshape(tm // MOBA_BLOCK, MOBA_BLOCK, ATTN_WIDTH).sum(axis=1)

    @pl.when(j == 2)
    def _():
        v_ref[...] = acc

    @pl.when(j >= 3)
    def _():
        half = acc.shape[1] // 2
        u_ref[...] = acc[:, :half] * _sigmoid(acc[:, half:])


def _proj(x, g1, wcat, tm):
    n = x.shape[0]
    nblk = tm // MOBA_BLOCK
    row = lambda i, j: (i, 0)
    return pl.pallas_call(
        _proj_kernel,
        grid=(n // tm, 5),
        in_specs=[pl.BlockSpec((tm, D_MODEL), row),
                  pl.BlockSpec((1, D_MODEL), lambda i, j: (0, 0)),
                  pl.BlockSpec((D_MODEL, ATTN_WIDTH), lambda i, j: (0, j))],
        out_specs=[pl.BlockSpec((tm, D_MODEL), row),
                   pl.BlockSpec((tm, ATTN_WIDTH), row),
                   pl.BlockSpec((tm, ATTN_WIDTH), row),
                   pl.BlockSpec((tm, ATTN_WIDTH), row),
                   pl.BlockSpec((tm, CONV_CH // 2), lambda i, j: (i, jnp.maximum(j - 3, 0))),
                   pl.BlockSpec((1, nblk, ATTN_WIDTH), lambda i, j: (i, 0, 0))],
        out_shape=[jax.ShapeDtypeStruct((n, D_MODEL), BF16),
                   jax.ShapeDtypeStruct((n, ATTN_WIDTH), F32),
                   jax.ShapeDtypeStruct((n, ATTN_WIDTH), F32),
                   jax.ShapeDtypeStruct((n, ATTN_WIDTH), F32),
                   jax.ShapeDtypeStruct((n, CONV_CH), F32),
                   jax.ShapeDtypeStruct((n // tm, nblk, ATTN_WIDTH), F32)],
        scratch_shapes=[pltpu.VMEM((tm, D_MODEL), BF16)],
        compiler_params=_params(("parallel", "arbitrary")),
        name="proj",
    )(x, g1, wcat)


def _t5_bucket_py(n):
    max_exact = NUM_BUCKETS // 2
    if n < max_exact:
        return n
    return min(max_exact + int(math.log(n / max_exact) / math.log(MAX_DISTANCE / max_exact)
                               * (NUM_BUCKETS - max_exact)), NUM_BUCKETS - 1)


assert _t5_bucket_py(MOBA_BLOCK + 1) == NUM_BUCKETS - 1


def _bias_kernel(rb_ref, o_ref):
    h = pl.program_id(0)
    qi = lax.broadcasted_iota(jnp.int32, (MOBA_BLOCK, MOBA_BLOCK), 0)
    kj = lax.broadcasted_iota(jnp.int32, (MOBA_BLOCK, MOBA_BLOCK), 1)
    max_exact = NUM_BUCKETS // 2
    for c in range(2):
        dist = qi - kj + c * MOBA_BLOCK
        nn = jnp.maximum(dist, 0)
        nf = jnp.maximum(nn, 1).astype(F32)
        large = max_exact + (jnp.log(nf / max_exact) / math.log(MAX_DISTANCE / max_exact)
                             * (NUM_BUCKETS - max_exact)).astype(jnp.int32)
        large = jnp.minimum(large, NUM_BUCKETS - 1)
        bkt = jnp.where(nn < max_exact, nn, large)
        val = jnp.zeros((MOBA_BLOCK, MOBA_BLOCK), F32)
        for t in range(NUM_BUCKETS):
            val = jnp.where(bkt == t, rb_ref[t, h], val)
        if c == 0:
            val = jnp.where(dist >= 0, val, NEG)
        else:
            val = val - rb_ref[NUM_BUCKETS - 1, h]
        o_ref[0, c] = val


def _bias_tiles(rel_bias):
    return pl.pallas_call(
        _bias_kernel,
        grid=(N_HEADS,),
        in_specs=[pl.BlockSpec(memory_space=pltpu.SMEM)],
        out_specs=pl.BlockSpec((1, 2, MOBA_BLOCK, MOBA_BLOCK), lambda h: (h, 0, 0, 0)),
        out_shape=jax.ShapeDtypeStruct((N_HEADS, 2, MOBA_BLOCK, MOBA_BLOCK), F32),
        compiler_params=_params(("arbitrary",)),
        name="bias_tiles",
    )(rel_bias)


def _attn_kernel(far_ref, q_ref, k_ref, v_ref, km_ref, bias_ref, oh_ref, o_ref,
                 qb_scr, sel_scr, m_scr, l_scr, acc_scr, *, nb):
    h = pl.program_id(1)
    qi = pl.program_id(2)
    blk = MOBA_BLOCK
    q = q_ref[...]

    s = lax.dot_general(q, km_ref[0, 0], _NT, precision=lax.Precision.HIGHEST, preferred_element_type=F32)
    col = lax.broadcasted_iota(jnp.int32, (blk, LANES), 1)
    past = col < qi
    s = jnp.where(past, s, NEG)
    rank = jnp.zeros((blk, LANES), jnp.int32)
    for i in range(nb):
        si = s[:, i:i + 1]
        rank = rank + ((si > s) | ((si == s) & (col > i))).astype(jnp.int32)
    selected = past & (rank < MOBA_TOPK)

    farv = jnp.full((blk, LANES), far_ref[h], F32)
    far_hi = farv.astype(BF16)
    far_lo = (farv - far_hi.astype(F32)).astype(BF16)
    sel = jnp.where(selected, 0.0, NEG).astype(BF16)
    sel_scr[...] = jnp.where(col == nb, far_hi, jnp.where(col == nb + 1, far_lo, sel))
    qb_scr[...] = (q * (HEAD_DIM ** -0.5)).astype(BF16)

    def load_kv(j):
        r0 = pl.multiple_of(j * blk, blk)
        return k_ref[pl.ds(r0, blk), :].astype(BF16), v_ref[pl.ds(r0, blk), :].astype(BF16)

    kb, vb = load_kv(qi)
    lg = lax.dot_general(qb_scr[...], kb, _NT, preferred_element_type=F32) + bias_ref[0, 0]
    m0 = jnp.max(lg, axis=1, keepdims=True)
    p = jnp.exp(lg - m0)
    m_scr[...] = m0
    l_scr[...] = jnp.sum(p, axis=1, keepdims=True)
    acc_scr[...] = jnp.dot(p.astype(BF16), vb, preferred_element_type=F32)

    def step(j, extra):
        kb, vb = load_kv(j)
        lg = (lax.dot_general(qb_scr[...], kb, _NT, preferred_element_type=F32)
              + jnp.dot(sel_scr[...], oh_ref[j], preferred_element_type=F32))
        if extra is not None:
            lg = lg + extra
        m_old = m_scr[...]
        m_new = jnp.maximum(m_old, jnp.max(lg, axis=1, keepdims=True))
        alpha = jnp.exp(m_old - m_new)
        p = jnp.exp(lg - m_new)
        l_scr[...] = alpha * l_scr[...] + jnp.sum(p, axis=1, keepdims=True)
        acc_scr[...] = alpha * acc_scr[...] + jnp.dot(p.astype(BF16), vb, preferred_element_type=F32)
        m_scr[...] = m_new

    @pl.when(qi >= 1)
    def _():
        step(qi - 1, bias_ref[0, 1])

    def body(j, carry):
        step(j, None)
        return carry

    lax.fori_loop(0, qi - 1, body, 0)
    o_ref[...] = (acc_scr[...] / l_scr[...]).astype(o_ref.dtype)


def _block_onehots(nb):
    oh = np.zeros((nb, LANES, MOBA_BLOCK), np.float32)
    for j in range(nb):
        oh[j, j, :] = 1.0
    oh[:, nb, :] = 1.0
    oh[:, nb + 1, :] = 1.0
    return jnp.asarray(oh, BF16)


def _prompt_attention(q, k, v, kmeans, bias, far_bias, batch, seq):
    nb = seq // MOBA_BLOCK
    assert nb + 2 <= LANES
    grid_spec = pltpu.PrefetchScalarGridSpec(
        num_scalar_prefetch=0,
        grid=(batch, N_HEADS, nb),
        in_specs=[pl.BlockSpec(memory_space=pltpu.SMEM),
                  pl.BlockSpec((MOBA_BLOCK, HEAD_DIM), lambda b, h, i: (b * nb + i, h)),
                  pl.BlockSpec((seq, HEAD_DIM), lambda b, h, i: (b, h)),
                  pl.BlockSpec((seq, HEAD_DIM), lambda b, h, i: (b, h)),
                  pl.BlockSpec((1, 1, LANES, HEAD_DIM), lambda b, h, i: (b, h, 0, 0)),
                  pl.BlockSpec((1, 2, MOBA_BLOCK, MOBA_BLOCK), lambda b, h, i: (h, 0, 0, 0)),
                  pl.BlockSpec((nb, LANES, MOBA_BLOCK), lambda b, h, i: (0, 0, 0))],
        out_specs=pl.BlockSpec((MOBA_BLOCK, HEAD_DIM), lambda b, h, i: (b * nb + i, h)),
        scratch_shapes=[pltpu.VMEM((MOBA_BLOCK, HEAD_DIM), BF16),
                        pltpu.VMEM((MOBA_BLOCK, LANES), BF16),
                        pltpu.VMEM((MOBA_BLOCK, 1), F32),
                        pltpu.VMEM((MOBA_BLOCK, 1), F32),
                        pltpu.VMEM((MOBA_BLOCK, HEAD_DIM), F32)])
    return pl.pallas_call(
        functools.partial(_attn_kernel, nb=nb),
        grid_spec=grid_spec,
        out_shape=jax.ShapeDtypeStruct((batch * seq, ATTN_WIDTH), BF16),
        compiler_params=_params(("parallel", "parallel", "arbitrary")),
        name="prompt_attn",
    )(far_bias, q, k, v, kmeans, bias, _block_onehots(nb))


PAGES_PER_STEP = 8


def _pagesum_kernel(pt_ref, *refs):
    o_ref = refs[-1]
    pages_per_block = PAGES_PER_STEP // o_ref.shape[1]
    for c in range(o_ref.shape[1]):
        tot = None
        for r in range(pages_per_block):
            part = jnp.sum(refs[c * pages_per_block + r][0], axis=0)
            tot = part if tot is None else tot + part
        o_ref[0, c] = tot


def _page_block_sums(cache_k, page_table, page_size):
    db, n_pages = page_table.shape
    ppb = MOBA_BLOCK // page_size
    assert MOBA_BLOCK % page_size == 0 and PAGES_PER_STEP % ppb == 0 and n_pages % PAGES_PER_STEP == 0
    bps = PAGES_PER_STEP // ppb

    def page_spec(r):
        return pl.BlockSpec((1, page_size, N_HEADS, HEAD_DIM),
                            lambda b, i, pt: (pt[b, i * PAGES_PER_STEP + r], 0, 0, 0))

    grid_spec = pltpu.PrefetchScalarGridSpec(
        num_scalar_prefetch=1,
        grid=(db, n_pages // PAGES_PER_STEP),
        in_specs=[page_spec(r) for r in range(PAGES_PER_STEP)],
        out_specs=pl.BlockSpec((1, bps, N_HEADS, HEAD_DIM), lambda b, i, pt: (b, i, 0, 0)))
    return pl.pallas_call(
        _pagesum_kernel,
        grid_spec=grid_spec,
        out_shape=jax.ShapeDtypeStruct((db, n_pages // ppb, N_HEADS, HEAD_DIM), F32),
        compiler_params=_params(("parallel", "arbitrary")),
        name="page_block_sums",
    )(page_table, *([cache_k] * PAGES_PER_STEP))


def _sselect_kernel(q_ref, km_ref, o_ref, *, own):
    s_new = q_ref.shape[0]
    col = lax.broadcasted_iota(jnp.int32, (s_new, LANES), 1)
    colf = col.astype(F32)
    for h in range(N_HEADS):
        qh = q_ref[:, h * HEAD_DIM:(h + 1) * HEAD_DIM]
        s = lax.dot_general(qh, km_ref[0, h], _NT, precision=lax.Precision.HIGHEST, preferred_element_type=F32)
        s = jnp.where(col < own, s, NEG)
        out = jnp.zeros((s_new, LANES), F32)
        for t in range(MOBA_TOPK):
            mx = jnp.max(s, axis=1, keepdims=True)
            arg = jnp.min(jnp.where(s == mx, colf, float(LANES)), axis=1, keepdims=True)
            out = jnp.where(col == t, arg, out)
            s = jnp.where(colf == arg, -jnp.inf, s)
        o_ref[0, h] = out.astype(jnp.int32)


def _sample_select(q, kmeans_s, row0, db, s_new, own):
    assert MOBA_TOPK <= own < LANES
    return pl.pallas_call(
        functools.partial(_sselect_kernel, own=own),
        grid=(db,),
        in_specs=[pl.BlockSpec((s_new, ATTN_WIDTH), lambda b: (row0 // s_new + b, 0)),
                  pl.BlockSpec((1, N_HEADS, LANES, HEAD_DIM), lambda b: (b, 0, 0, 0))],
        out_specs=pl.BlockSpec((1, N_HEADS, s_new, LANES), lambda b: (b, 0, 0, 0)),
        out_shape=jax.ShapeDtypeStruct((db, N_HEADS, s_new, LANES), jnp.int32),
        compiler_params=_params(("parallel",)),
        name="sample_select",
    )(q, kmeans_s)


def _sattn_kernel(pt_ref, idx_ref, far_ref, q_ref, kn_ref, vn_ref, bias_ref, ck_hbm, cv_hbm, o_ref,
                  kbuf, vbuf, sem, *, n_pages, page_size, own, s_new):
    b = pl.program_id(0)
    h = pl.program_id(1)
    nh = pl.num_programs(1)
    step = b * nh + h
    nsteps = pl.num_programs(0) * nh
    slot = step % 2
    ppb = MOBA_BLOCK // page_size
    nsel = s_new * MOBA_TOPK
    nkeys = nsel * MOBA_BLOCK

    def copies(step_, slot_):
        b_ = step_ // nh
        h_ = step_ % nh
        out = []
        for c in range(nsel):
            blk = idx_ref[step_ * nsel + c]
            for r in range(ppb):
                phys = pt_ref[b_ * n_pages + blk * ppb + r]
                dst = pl.ds((c * ppb + r) * page_size, page_size)
                out.append(pltpu.make_async_copy(ck_hbm.at[phys, :, h_, :], kbuf.at[slot_, dst, :], sem.at[slot_, 0]))
                out.append(pltpu.make_async_copy(cv_hbm.at[phys, :, h_, :], vbuf.at[slot_, dst, :], sem.at[slot_, 1]))
        return out

    @pl.when(step == 0)
    def _():
        for cp in copies(step, slot):
            cp.start()

    @pl.when(step + 1 < nsteps)
    def _():
        for cp in copies(step + 1, 1 - slot):
            cp.start()

    for cp in copies(step, slot):
        cp.wait()

    qb = (q_ref[...] * (HEAD_DIM ** -0.5)).astype(BF16)
    kb = kbuf[slot].astype(BF16)
    vb = vbuf[slot].astype(BF16)
    lg = lax.dot_general(qb, kb, _NT, preferred_element_type=F32) + far_ref[h]
    pieces = []
    for c in range(nsel):
        is_prev = idx_ref[step * nsel + c] == own - 1
        row = bias_ref[0, 1, (c // MOBA_TOPK):(c // MOBA_TOPK) + 1, :]
        pieces.append(jnp.where(is_prev, row, 0.0))
    lg = lg + jnp.concatenate(pieces, axis=1)
    col = lax.broadcasted_iota(jnp.int32, (s_new, nkeys), 1)
    row_i = lax.broadcasted_iota(jnp.int32, (s_new, nkeys), 0)
    mine = (col >= row_i * (MOBA_TOPK * MOBA_BLOCK)) & (col < (row_i + 1) * (MOBA_TOPK * MOBA_BLOCK))
    lg = jnp.where(mine, lg, NEG)
    lg_own = (lax.dot_general(qb, kn_ref[...].astype(BF16), _NT, preferred_element_type=F32)
              + bias_ref[0, 0, 0:s_new, 0:s_new])
    m = jnp.maximum(jnp.max(lg, axis=1, keepdims=True), jnp.max(lg_own, axis=1, keepdims=True))
    p = jnp.exp(lg - m)
    p_own = jnp.exp(lg_own - m)
    denom = jnp.sum(p, axis=1, keepdims=True) + jnp.sum(p_own, axis=1, keepdims=True)
    acc = (jnp.dot(p.astype(BF16), vb, preferred_element_type=F32)
           + jnp.dot(p_own.astype(BF16), vn_ref[...].astype(BF16), preferred_element_type=F32))
    o_ref[...] = acc / denom


def _sample_attention(q, k, v, cache_k, cache_v, page_table, sel_idx, bias, far_bias, row0, s_new, own):
    db, n_pages = page_table.shape
    page_size = cache_k.shape[1]
    nkeys = s_new * MOBA_TOPK * MOBA_BLOCK
    new_spec = pl.BlockSpec((s_new, HEAD_DIM), lambda b, h, pt, ix: (row0 // s_new + b, h))
    grid_spec = pltpu.PrefetchScalarGridSpec(
        num_scalar_prefetch=2,
        grid=(db, N_HEADS),
        in_specs=[pl.BlockSpec(memory_space=pltpu.SMEM),
                  new_spec, new_spec, new_spec,
                  pl.BlockSpec((1, 2, MOBA_BLOCK, MOBA_BLOCK), lambda b, h, pt, ix: (h, 0, 0, 0)),
                  pl.BlockSpec(memory_space=pl.ANY),
                  pl.BlockSpec(memory_space=pl.ANY)],
        out_specs=pl.BlockSpec((s_new, HEAD_DIM), lambda b, h, pt, ix: (b, h)),
        scratch_shapes=[pltpu.VMEM((2, nkeys, HEAD_DIM), F32),
                        pltpu.VMEM((2, nkeys, HEAD_DIM), F32),
                        pltpu.SemaphoreType.DMA((2, 2))])
    return pl.pallas_call(
        functools.partial(_sattn_kernel, n_pages=n_pages, page_size=page_size, own=own, s_new=s_new),
        grid_spec=grid_spec,
        out_shape=jax.ShapeDtypeStruct((db * s_new, ATTN_WIDTH), F32),
        compiler_params=_params(("arbitrary", "arbitrary")),
        name="sample_attn",
    )(page_table.reshape(-1), sel_idx.reshape(-1), far_bias, q, k, v, bias, cache_k, cache_v)


def _conv_kernel(halo_ref, cur_ref, cw_ref, cb_ref, lg_ref, lb_ref, o_ref, ext_scr, *, tt, rc, zero_first):
    i = pl.program_id(1)
    if zero_first:
        @pl.when(i == 0)
        def _():
            ext_scr[0:CONV_HALO, :] = jnp.zeros((CONV_HALO, CONV_CH), F32)

        @pl.when(i > 0)
        def _():
            ext_scr[0:CONV_HALO, :] = halo_ref[...]
    else:
        ext_scr[0:CONV_HALO, :] = halo_ref[...]
    ext_scr[CONV_HALO:CONV_HALO + tt, :] = cur_ref[...]
    lead = CONV_HALO - (CONV_WIDTH - 1)

    def chunk(c, carry):
        r0 = pl.multiple_of(c * rc, rc)
        win = ext_scr[pl.ds(r0, rc + CONV_HALO), :]
        acc = jnp.zeros((rc, CONV_CH), F32)
        for r in range(8):
            rolled = win if r == 0 else pltpu.roll(win, rc + CONV_HALO - r, axis=0)
            for w in range(CONV_WIDTH):
                if (lead + w) % 8 == r:
                    a0 = lead + w - r
                    acc = acc + rolled[a0:a0 + rc, :] * cw_ref[w:w + 1, :]
        hc = acc + cb_ref[...]
        mu = jnp.mean(hc, axis=-1, keepdims=True)
        var = jnp.mean(jnp.square(hc - mu), axis=-1, keepdims=True)
        y = (hc - mu) * lax.rsqrt(var + EPS) * lg_ref[...] + lb_ref[...]
        o_ref[pl.ds(r0, rc), :] = (y * _sigmoid(y)).astype(o_ref.dtype)
        return carry

    lax.fori_loop(0, tt // rc, chunk, 0)


def _conv_module(halo_src, halo_map, cur_src, cur_map, conv_w, conv_b, ln_g, ln_b, n_seq, n_tiles, tt,
                 zero_first, out_rows, out_dtype):
    rc = min(tt, 32)
    vec = pl.BlockSpec((1, CONV_CH), lambda b, i: (0, 0))
    return pl.pallas_call(
        functools.partial(_conv_kernel, tt=tt, rc=rc, zero_first=zero_first),
        grid=(n_seq, n_tiles),
        in_specs=[pl.BlockSpec((CONV_HALO, CONV_CH), halo_map),
                  pl.BlockSpec((tt, CONV_CH), cur_map),
                  pl.BlockSpec((CONV_WIDTH, CONV_CH), lambda b, i: (0, 0)),
                  vec, vec, vec],
        out_specs=pl.BlockSpec((tt, CONV_CH), lambda b, i: (b * n_tiles + i, 0)),
        out_shape=jax.ShapeDtypeStruct((out_rows, CONV_CH), out_dtype),
        scratch_shapes=[pltpu.VMEM((CONV_HALO + tt, CONV_CH), F32)],
        compiler_params=_params(("parallel", "arbitrary")),
        name="conv_module",
    )(halo_src, cur_src, conv_w, conv_b, ln_g, ln_b)


def _mix_kernel(xn_ref, at_ref, cv_ref, wga_ref, wgc_ref, bga_ref, bgc_ref, wa_ref, wc_ref, o_ref):
    xn = xn_ref[...]
    ga = _sigmoid(jnp.dot(xn, wga_ref[...], preferred_element_type=F32) + bga_ref[...])
    gc = _sigmoid(jnp.dot(xn, wgc_ref[...], preferred_element_type=F32) + bgc_ref[...])
    ya = jnp.dot(at_ref[...], wa_ref[...], preferred_element_type=F32)
    yc = jnp.dot(cv_ref[...], wc_ref[...], preferred_element_type=F32)
    o_ref[...] = (ga * ya + gc * yc).astype(o_ref.dtype)


def _gate_mix(xn, attn, cva, w_gate, b_gate, w_attn_out, w_conv_out, tm, tn=512):
    n = xn.shape[0]
    nj = D_MODEL // tn
    row = lambda i, j: (i, 0)
    return pl.pallas_call(
        _mix_kernel,
        grid=(n // tm, nj),
        in_specs=[pl.BlockSpec((tm, D_MODEL), row),
                  pl.BlockSpec((tm, ATTN_WIDTH), row),
                  pl.BlockSpec((tm, CONV_CH), row),
                  pl.BlockSpec((D_MODEL, tn), lambda i, j: (0, j)),
                  pl.BlockSpec((D_MODEL, tn), lambda i, j: (0, nj + j)),
                  pl.BlockSpec((1, tn), lambda i, j: (0, j)),
                  pl.BlockSpec((1, tn), lambda i, j: (0, nj + j)),
                  pl.BlockSpec((ATTN_WIDTH, tn), lambda i, j: (0, j)),
                  pl.BlockSpec((CONV_CH, tn), lambda i, j: (0, j))],
        out_specs=pl.BlockSpec((tm, tn), lambda i, j: (i, j)),
        out_shape=jax.ShapeDtypeStruct((n, D_MODEL), BF16),
        compiler_params=_params(("parallel", "arbitrary")),
        name="gate_mix",
    )(xn, attn, cva, w_gate, w_gate, b_gate, b_gate, w_attn_out, w_conv_out)


def _outproj_kernel(x_ref, mix_ref, wo_ref, g2_ref, wrh_ref, wrl_ref, br_ref, x1_ref, xn2_ref, rt_ref):
    x1 = x_ref[...] + jnp.dot(mix_ref[...], wo_ref[...], preferred_element_type=F32)
    x1_ref[...] = x1
    xn2 = x1 * lax.rsqrt(jnp.mean(x1 * x1, axis=-1, keepdims=True) + EPS) * g2_ref[...]
    xn2_ref[...] = xn2
    hi = xn2.astype(BF16)
    lo = (xn2 - hi.astype(F32)).astype(BF16)
    wrh = wrh_ref[...]
    logits = (jnp.dot(hi, wrh, preferred_element_type=F32) + jnp.dot(lo, wrh, preferred_element_type=F32)
              + jnp.dot(hi, wrl_ref[...], preferred_element_type=F32) + br_ref[...])
    tm = logits.shape[0]
    col = lax.broadcasted_iota(jnp.int32, (tm, LANES), 1).astype(F32)
    big = float(LANES)

    def first_argmax(vals):
        mx = jnp.max(vals, axis=1, keepdims=True)
        return mx, jnp.min(jnp.where(vals == mx, col, big), axis=1, keepdims=True)

    gmask = col < N_GROUPS
    gmax, gidx = first_argmax(jnp.where(gmask, logits, -jnp.inf))
    p_g = 1.0 / jnp.sum(jnp.where(gmask, jnp.exp(logits - gmax), 0.0), axis=1, keepdims=True)
    e_lo = N_GROUPS + gidx * EXPERTS_PER_GROUP
    el = jnp.where((col >= e_lo) & (col < e_lo + EXPERTS_PER_GROUP), logits, -jnp.inf)
    v1, i1 = first_argmax(el)
    v2, i2 = first_argmax(jnp.where(col == i1, -jnp.inf, el))
    e21 = jnp.exp(v2 - v1)
    p1 = 1.0 / (1.0 + e21)
    p2 = e21 / (1.0 + e21)
    rt = jnp.where(col == 0, i1 - N_GROUPS,
                   jnp.where(col == 1, i2 - N_GROUPS,
                             jnp.where(col == 2, p_g * p1, jnp.where(col == 3, p_g * p2, 0.0))))
    rt_ref[...] = rt


def _out_proj(x, mix, w_out, g2, wr_hi, wr_lo, br, tm):
    n = x.shape[0]
    row = lambda i: (i, 0)
    const = lambda i: (0, 0)
    return pl.pallas_call(
        _outproj_kernel,
        grid=(n // tm,),
        in_specs=[pl.BlockSpec((tm, D_MODEL), row),
                  pl.BlockSpec((tm, D_MODEL), row),
                  pl.BlockSpec((D_MODEL, D_MODEL), const),
                  pl.BlockSpec((1, D_MODEL), const),
                  pl.BlockSpec((D_MODEL, LANES), const),
                  pl.BlockSpec((D_MODEL, LANES), const),
                  pl.BlockSpec((1, LANES), const)],
        out_specs=[pl.BlockSpec((tm, D_MODEL), row),
                   pl.BlockSpec((tm, D_MODEL), row),
                   pl.BlockSpec((tm, LANES), row)],
        out_shape=[jax.ShapeDtypeStruct((n, D_MODEL), F32),
                   jax.ShapeDtypeStruct((n, D_MODEL), F32),
                   jax.ShapeDtypeStruct((n, LANES), F32)],
        compiler_params=_params(("parallel",)),
        name="out_proj",
    )(x, mix, w_out, g2, wr_hi, wr_lo, br)


def _dispatch_kernel(dest_ref, x_ref, xb_in, xb_out, sem, *, tm):
    del xb_in
    i = pl.program_id(0)

    def row_copy(t, d):
        return pltpu.make_async_copy(x_ref.at[pl.ds(t, 1), :], xb_out.at[pl.ds(d, 1), :], sem)

    def issue(t, carry):
        base = (i * tm + t) * 2
        row_copy(t, dest_ref[base]).start()
        row_copy(t, dest_ref[base + 1]).start()
        return carry

    lax.fori_loop(0, tm, issue, 0)

    def drain(t, carry):
        row_copy(0, 0).wait()
        row_copy(0, 0).wait()
        return carry

    lax.fori_loop(0, tm, drain, 0)


def _dispatch(xn2, dest, n_rows, tm):
    n = xn2.shape[0]
    grid_spec = pltpu.PrefetchScalarGridSpec(
        num_scalar_prefetch=1,
        grid=(n // tm,),
        in_specs=[pl.BlockSpec((tm, D_MODEL), lambda i, d: (i, 0)),
                  pl.BlockSpec(memory_space=pl.ANY)],
        out_specs=pl.BlockSpec(memory_space=pl.ANY),
        scratch_shapes=[pltpu.SemaphoreType.DMA(())])
    return pl.pallas_call(
        functools.partial(_dispatch_kernel, tm=tm),
        grid_spec=grid_spec,
        out_shape=jax.ShapeDtypeStruct((n_rows, D_MODEL), F32),
        input_output_aliases={2: 0},
        compiler_params=_params(("arbitrary",)),
        name="moe_dispatch",
    )(dest, xn2, jnp.zeros((n_rows, D_MODEL), F32))


def _moe_kernel(be_ref, nu_ref, x_ref, w1_ref, w3_ref, w2_ref, o_ref):
    @pl.when(pl.program_id(0) < nu_ref[0])
    def _():
        xb = x_ref[...].astype(BF16)
        a = jnp.dot(xb, w1_ref[0], preferred_element_type=F32)
        g = jnp.dot(xb, w3_ref[0], preferred_element_type=F32)
        hmid = (a * _sigmoid(a) * g).astype(BF16)
        o_ref[...] = jnp.dot(hmid, w2_ref[0], preferred_element_type=F32)

    @pl.when(pl.program_id(0) >= nu_ref[0])
    def _():
        o_ref[...] = jnp.zeros(o_ref.shape, F32)


def _moe_experts(xb, block_e, n_used, w1, w3, w2):
    n_blocks = xb.shape[0] // MOE_ROWS
    blk = lambda g, be, nu: (jnp.minimum(g, nu[0] - 1), 0)
    wsel = lambda g, be, nu: (be[jnp.minimum(g, nu[0] - 1)], 0, 0)
    grid_spec = pltpu.PrefetchScalarGridSpec(
        num_scalar_prefetch=2,
        grid=(n_blocks,),
        in_specs=[pl.BlockSpec((MOE_ROWS, D_MODEL), blk),
                  pl.BlockSpec((1, D_MODEL, D_EXPERT), wsel),
                  pl.BlockSpec((1, D_MODEL, D_EXPERT), wsel),
                  pl.BlockSpec((1, D_EXPERT, D_MODEL), wsel)],
        out_specs=pl.BlockSpec((MOE_ROWS, D_MODEL), lambda g, be, nu: (g, 0)))
    return pl.pallas_call(
        _moe_kernel,
        grid_spec=grid_spec,
        out_shape=jax.ShapeDtypeStruct(xb.shape, F32),
        compiler_params=_params(("arbitrary",)),
        name="moe_experts",
    )(block_e, n_used, xb, w1, w3, w2)


def _combine_kernel(dest_ref, x1_ref, rt_ref, gf_ref, yb_hbm, o_ref, ybuf, sem, *, tm):
    i = pl.program_id(0)

    def row_copy(k, t, d):
        return pltpu.make_async_copy(yb_hbm.at[pl.ds(d, 1), :], ybuf.at[k, pl.ds(t, 1), :], sem)

    def issue(t, carry):
        base = (i * tm + t) * 2
        row_copy(0, t, dest_ref[base]).start()
        row_copy(1, t, dest_ref[base + 1]).start()
        return carry

    lax.fori_loop(0, tm, issue, 0)

    def drain(t, carry):
        row_copy(0, 0, 0).wait()
        row_copy(0, 0, 0).wait()
        return carry

    lax.fori_loop(0, tm, drain, 0)
    rt = rt_ref[...]
    moe = ybuf[0] * rt[:, 2:3] + ybuf[1] * rt[:, 3:4]
    xo = x1_ref[...] + moe
    o_ref[...] = xo * lax.rsqrt(jnp.mean(xo * xo, axis=-1, keepdims=True) + EPS) * gf_ref[...]


def _combine(x1, route, gf, yb, dest, tm):
    n = x1.shape[0]
    grid_spec = pltpu.PrefetchScalarGridSpec(
        num_scalar_prefetch=1,
        grid=(n // tm,),
        in_specs=[pl.BlockSpec((tm, D_MODEL), lambda i, d: (i, 0)),
                  pl.BlockSpec((tm, LANES), lambda i, d: (i, 0)),
                  pl.BlockSpec((1, D_MODEL), lambda i, d: (0, 0)),
                  pl.BlockSpec(memory_space=pl.ANY)],
        out_specs=pl.BlockSpec((tm, D_MODEL), lambda i, d: (i, 0)),
        scratch_shapes=[pltpu.VMEM((2, tm, D_MODEL), F32),
                        pltpu.SemaphoreType.DMA(())])
    return pl.pallas_call(
        functools.partial(_combine_kernel, tm=tm),
        grid_spec=grid_spec,
        out_shape=jax.ShapeDtypeStruct((n, D_MODEL), F32),
        compiler_params=_params(("arbitrary",)),
        name="moe_combine",
    )(dest, x1, route, gf, yb)


def _dispatch_plan(route):
    n = route.shape[0]
    flat_e = route[:, 0:2].astype(jnp.int32).reshape(-1)
    s = flat_e.shape[0]
    onehot = (flat_e[:, None] == jnp.arange(N_EXPERTS, dtype=jnp.int32)[None, :]).astype(jnp.int32)
    csum = jnp.cumsum(onehot, axis=0)
    counts = csum[-1]
    pcounts = (counts + MOE_ROWS - 1) // MOE_ROWS * MOE_ROWS
    pend = jnp.cumsum(pcounts)
    pstart = pend - pcounts
    dest = jnp.sum(onehot * (csum - 1 + pstart[None, :]), axis=1).astype(jnp.int32)
    n_blocks = (s + N_EXPERTS * (MOE_ROWS - 1) + MOE_ROWS - 1) // MOE_ROWS
    block_e = jnp.minimum(jnp.searchsorted(pend, jnp.arange(n_blocks, dtype=jnp.int32) * MOE_ROWS, side='right'),
                          N_EXPERTS - 1).astype(jnp.int32)
    n_used = (pend[-1] // MOE_ROWS).astype(jnp.int32).reshape(1)
    return dest, block_e, n_used, n_blocks * MOE_ROWS


def kernel(x_prompt, x_sample, cache_k, cache_v, state_conv, page_table, rel_bias, norm1_g, w_in, w_attn_out,
           conv_w, conv_b, conv_ln_g, conv_ln_b, w_conv_out, w_gate, b_gate, w_out, norm2_g, router_g_w,
           router_g_b, router_e_w, router_e_b, moe_w1, moe_w3, moe_w2, norm_f_g):
    depth = norm1_g.shape[0]
    assert depth == 1
    batch, seq, _ = x_prompt.shape
    db, s_new, _ = x_sample.shape
    n_pages = page_table.shape[1]
    page_size = cache_k.shape[2]
    past_len = n_pages * page_size
    assert seq % MOBA_BLOCK == 0 and past_len % MOBA_BLOCK == 0 and s_new <= MOBA_BLOCK and s_new % 8 == 0
    own_s = past_len // MOBA_BLOCK
    n_p = batch * seq
    n_s = db * s_new
    tm = ROW_TILE
    n_tok = -(-(n_p + n_s) // tm) * tm

    half = CONV_CH // 2
    wi = w_in[0]
    u0 = 3 * ATTN_WIDTH
    wcat = jnp.concatenate(
        [wi[:, :u0], wi[:, u0:u0 + half], wi[:, u0 + CONV_CH:u0 + CONV_CH + half],
         wi[:, u0 + half:u0 + CONV_CH], wi[:, u0 + CONV_CH + half:]], axis=1).astype(BF16)
    wr = jnp.concatenate([router_g_w[0], router_e_w[0]], axis=1)
    wr = jnp.pad(wr, ((0, 0), (0, LANES - wr.shape[1])))
    wr_hi = wr.astype(BF16)
    wr_lo = (wr - wr_hi.astype(F32)).astype(BF16)
    br = jnp.pad(jnp.concatenate([router_g_b[0], router_e_b[0]]), (0, LANES - N_GROUPS - N_EXPERTS))[None, :]
    far_bias = rel_bias[NUM_BUCKETS - 1, :]

    x_all = jnp.concatenate([x_prompt.reshape(n_p, D_MODEL), x_sample.reshape(n_s, D_MODEL),
                             jnp.zeros((n_tok - n_p - n_s, D_MODEL), F32)], axis=0)
    xn, q, k, v, u, ksum = _proj(x_all, norm1_g, wcat, tm)
    bias = _bias_tiles(rel_bias)

    nb = seq // MOBA_BLOCK
    km = ksum.reshape(-1, N_HEADS, HEAD_DIM)[:batch * nb].reshape(batch, nb, N_HEADS, HEAD_DIM) / MOBA_BLOCK
    km = jnp.pad(km.transpose(0, 2, 1, 3), ((0, 0), (0, 0), (0, LANES - nb), (0, 0)))
    attn_p = _prompt_attention(q, k, v, km, bias, far_bias, batch, seq)

    ck, cv = cache_k[0], cache_v[0]
    bsum = _page_block_sums(ck, page_table, page_size)
    k_s = k[n_p:n_p + n_s].reshape(db, s_new, N_HEADS, HEAD_DIM)
    sums = jnp.concatenate([bsum, jnp.sum(k_s, axis=1)[:, None]], axis=1)
    km_s = jnp.pad((sums / MOBA_BLOCK).transpose(0, 2, 1, 3), ((0, 0), (0, 0), (0, LANES - own_s - 1), (0, 0)))
    sel_idx = _sample_select(q, km_s, n_p, db, s_new, own_s)[..., :MOBA_TOPK]
    attn_s = _sample_attention(q, k, v, ck, cv, page_table, sel_idx, bias, far_bias, n_p, s_new, own_s)
    attn = jnp.concatenate([attn_p, attn_s.astype(BF16), jnp.zeros((n_tok - n_p - n_s, ATTN_WIDTH), BF16)], axis=0)

    tt = 512
    tiles = seq // tt
    per_halo = tt // CONV_HALO
    cva_p = _conv_module(u, lambda b, i: (jnp.maximum((b * tiles + i) * per_halo - 1, 0), 0),
                         u, lambda b, i: (b * tiles + i, 0),
                         conv_w[0], conv_b, conv_ln_g, conv_ln_b, batch, tiles, tt, True, n_p, BF16)
    lead = CONV_HALO - (CONV_WIDTH - 1)
    halo_s = jnp.pad(state_conv[0], ((0, 0), (lead, 0), (0, 0))).reshape(db * CONV_HALO, CONV_CH)
    cva_s = _conv_module(halo_s, lambda b, i: (b, 0),
                         u, lambda b, i: (n_p // s_new + b, 0),
                         conv_w[0], conv_b, conv_ln_g, conv_ln_b, db, 1, s_new, False, n_s, F32)
    cva = jnp.concatenate([cva_p, cva_s.astype(BF16), jnp.zeros((n_tok - n_p - n_s, CONV_CH), BF16)], axis=0)

    mix = _gate_mix(xn, attn, cva, w_gate[0].astype(BF16), b_gate, w_attn_out[0].astype(BF16),
                    w_conv_out[0].astype(BF16), tm)
    x1, xn2, route = _out_proj(x_all, mix, w_out[0].astype(BF16), norm2_g, wr_hi, wr_lo, br, 256)

    dest, block_e, n_used, n_rows = _dispatch_plan(route)
    xb = _dispatch(xn2, dest, n_rows, 256)
    yb = _moe_experts(xb, block_e, n_used, moe_w1[0].astype(BF16), moe_w3[0].astype(BF16), moe_w2[0].astype(BF16))
    y = _combine(x1, route, norm_f_g[None, :], yb, dest, 256)

    u_p = u[:n_p].reshape(batch, seq, CONV_CH)
    u_s = u[n_p:n_p + n_s].reshape(db, s_new, CONV_CH)
    conv_prompt = u_p[:, seq - (CONV_WIDTH - 1):]
    conv_sample = jnp.concatenate([state_conv[0], u_s], axis=1)[:, -(CONV_WIDTH - 1):]
    shp_p = (1, batch, seq, N_HEADS, HEAD_DIM)
    shp_s = (1, db, s_new, N_HEADS, HEAD_DIM)
    return (y[:n_p].reshape(batch, seq, D_MODEL), y[n_p:n_p + n_s].reshape(db, s_new, D_MODEL),
            k[:n_p].reshape(shp_p), v[:n_p].reshape(shp_p), conv_prompt[None],
            k[n_p:n_p + n_s].reshape(shp_s), v[n_p:n_p + n_s].reshape(shp_s), conv_sample[None])
```

```python
import functools
import math

import jax
import jax.numpy as jnp
from jax import lax
from jax.experimental import pallas as pl
from jax.experimental.pallas import tpu as pltpu

D_MODEL = 2048
HEAD_DIM = 128
N_HEADS = 8
ATTN_WIDTH = N_HEADS * HEAD_DIM
CONV_CH = 1024
CONV_WIDTH = 31
MOBA_BLOCK = 256
MOBA_TOPK = 3
NUM_BUCKETS = 32
MAX_DISTANCE = 128
N_GROUPS = 4
EXPERTS_PER_GROUP = 8
N_EXPERTS = N_GROUPS * EXPERTS_PER_GROUP
D_EXPERT = 512
EPS = 1e-6
NEG = -1e30

LANES = 128
SUBLANES = 8
CONV_HALO = 32
MOE_ROWS = 256
KEY_CHUNK = 512
VMEM_LIMIT = 56 * 1024 * 1024

F32 = jnp.float32
BF16 = jnp.bfloat16
_NT = (((1,), (1,)), ((), ()))


def _params(sem, vmem=VMEM_LIMIT):
    return pltpu.CompilerParams(dimension_semantics=sem, vmem_limit_bytes=vmem)


def _sigmoid(x):
    return 1.0 / (1.0 + jnp.exp(-x))


def _split_bf16(x):
    hi = x.astype(BF16)
    return hi, (x - hi.astype(F32)).astype(BF16)


def _proj_kernel(x_ref, g_ref, w_ref, xn_ref, q_ref, kc_ref, vc_ref, u_ref, ks_ref, k_hbm, v_hbm,
                 xn_scr, kst, vst, sem, *, tm):
    i = pl.program_id(0)
    j = pl.program_id(1)
    last_i = pl.num_programs(0) - 1

    def head_copies(src, dst_hbm, row_tile, s):
        r0 = pl.multiple_of(row_tile * tm, tm)
        return [pltpu.make_async_copy(src.at[:, h * HEAD_DIM:(h + 1) * HEAD_DIM],
                                      dst_hbm.at[pl.ds(r0, tm), h, :], sem.at[s]) for h in range(N_HEADS)]

    @pl.when(j == 0)
    def _():
        x = x_ref[...]
        y = x * lax.rsqrt(jnp.mean(x * x, axis=-1, keepdims=True) + EPS) * g_ref[...]
        yb = y.astype(BF16)
        xn_scr[...] = yb
        xn_ref[...] = yb

    acc = jnp.dot(xn_scr[...], w_ref[...], preferred_element_type=F32)

    @pl.when(j == 0)
    def _():
        q_ref[...] = acc

    def emit(stage, hbm, c_ref, s):
        @pl.when(i > 0)
        def _():
            for cp in head_copies(stage, hbm, i - 1, s):
                cp.wait()
        stage[...] = acc
        c_ref[...] = acc.astype(c_ref.dtype)
        for cp in head_copies(stage, hbm, i, s):
            cp.start()

    @pl.when(j == 1)
    def _():
        emit(kst, k_hbm, kc_ref, 0)
        ks_ref[0] = acc.reshape(tm // MOBA_BLOCK, MOBA_BLOCK, ATTN_WIDTH).sum(axis=1)

    @pl.when(j == 2)
    def _():
        emit(vst, v_hbm, vc_ref, 1)

    @pl.when(j >= 3)
    def _():
        half = acc.shape[1] // 2
        u_ref[...] = acc[:, :half] * _sigmoid(acc[:, half:])

    @pl.when((i == last_i) & (j == pl.num_programs(1) - 1))
    def _():
        for cp in head_copies(kst, k_hbm, i, 0) + head_copies(vst, v_hbm, i, 1):
            cp.wait()


def _proj(x, g1, wcat, tm, kv_dtype):
    n = x.shape[0]
    nblk = tm // MOBA_BLOCK
    row = lambda i, j: (i, 0)
    any_spec = pl.BlockSpec(memory_space=pl.ANY)
    return pl.pallas_call(
        functools.partial(_proj_kernel, tm=tm),
        grid=(n // tm, 5),
        in_specs=[pl.BlockSpec((tm, D_MODEL), row),
                  pl.BlockSpec((1, D_MODEL), lambda i, j: (0, 0)),
                  pl.BlockSpec((D_MODEL, ATTN_WIDTH), lambda i, j: (0, j))],
        out_specs=[pl.BlockSpec((tm, D_MODEL), row),
                   pl.BlockSpec((tm, ATTN_WIDTH), row),
                   pl.BlockSpec((tm, ATTN_WIDTH), row),
                   pl.BlockSpec((tm, ATTN_WIDTH), row),
                   pl.BlockSpec((tm, CONV_CH // 2), lambda i, j: (i, jnp.maximum(j - 3, 0))),
                   pl.BlockSpec((1, nblk, ATTN_WIDTH), lambda i, j: (i, 0, 0)),
                   any_spec, any_spec],
        out_shape=[jax.ShapeDtypeStruct((n, D_MODEL), BF16),
                   jax.ShapeDtypeStruct((n, ATTN_WIDTH), F32),
                   jax.ShapeDtypeStruct((n, ATTN_WIDTH), kv_dtype),
                   jax.ShapeDtypeStruct((n, ATTN_WIDTH), kv_dtype),
                   jax.ShapeDtypeStruct((n, CONV_CH), F32),
                   jax.ShapeDtypeStruct((n // tm, nblk, ATTN_WIDTH), F32),
                   jax.ShapeDtypeStruct((n, N_HEADS, HEAD_DIM), F32),
                   jax.ShapeDtypeStruct((n, N_HEADS, HEAD_DIM), F32)],
        scratch_shapes=[pltpu.VMEM((tm, D_MODEL), BF16),
                        pltpu.VMEM((tm, ATTN_WIDTH), F32),
                        pltpu.VMEM((tm, ATTN_WIDTH), F32),
                        pltpu.SemaphoreType.DMA((2,))],
        compiler_params=_params(("arbitrary", "arbitrary")),
        name="proj",
    )(x, g1, wcat)


def _t5_bucket_py(n):
    max_exact = NUM_BUCKETS // 2
    if n < max_exact:
        return n
    return min(max_exact + int(math.log(n / max_exact) / math.log(MAX_DISTANCE / max_exact)
                               * (NUM_BUCKETS - max_exact)), NUM_BUCKETS - 1)


assert _t5_bucket_py(MOBA_BLOCK + 1) == NUM_BUCKETS - 1


def _far_parts(far, shape):
    hi, lo = _split_bf16(jnp.full(shape, far, F32))
    return hi.astype(F32), lo.astype(F32)


def _bias_kernel(rb_ref, o_ref):
    h = pl.program_id(0)
    shape = (MOBA_BLOCK, MOBA_BLOCK)
    qi = lax.broadcasted_iota(jnp.int32, shape, 0)
    kj = lax.broadcasted_iota(jnp.int32, shape, 1)
    max_exact = NUM_BUCKETS // 2
    far_hi, far_lo = _far_parts(rb_ref[NUM_BUCKETS - 1, h], shape)
    for c in range(2):
        dist = qi - kj + c * MOBA_BLOCK
        nn = jnp.maximum(dist, 0)
        nf = jnp.maximum(nn, 1).astype(F32)
        large = max_exact + (jnp.log(nf / max_exact) / math.log(MAX_DISTANCE / max_exact)
                             * (NUM_BUCKETS - max_exact)).astype(jnp.int32)
        large = jnp.minimum(large, NUM_BUCKETS - 1)
        bkt = jnp.where(nn < max_exact, nn, large)
        val = jnp.zeros(shape, F32)
        for t in range(NUM_BUCKETS):
            val = jnp.where(bkt == t, rb_ref[t, h], val)
        val = val - (far_hi + far_lo)
        if c == 0:
            val = jnp.where(dist >= 0, val, NEG)
        o_ref[0, c] = val


def _bias_tiles(rel_bias):
    return pl.pallas_call(
        _bias_kernel,
        grid=(N_HEADS,),
        in_specs=[pl.BlockSpec(memory_space=pltpu.SMEM)],
        out_specs=pl.BlockSpec((1, 2, MOBA_BLOCK, MOBA_BLOCK), lambda h: (h, 0, 0, 0)),
        out_shape=jax.ShapeDtypeStruct((N_HEADS, 2, MOBA_BLOCK, MOBA_BLOCK), F32),
        compiler_params=_params(("arbitrary",)),
        name="bias_tiles",
    )(rel_bias)


def _attn_kernel(far_ref, q_ref, k_ref, v_ref, km_ref, bias_ref, o_ref, ka, va, qa, s_scr, p_scr, *, nb):
    h = pl.program_id(1)
    blk = MOBA_BLOCK
    seq = nb * blk
    nbp = -(-nb // SUBLANES) * SUBLANES

    row_blk = lax.broadcasted_iota(jnp.int32, (seq, LANES), 0) // blk
    col = lax.broadcasted_iota(jnp.int32, (seq, LANES), 1)
    ka[:, 0:HEAD_DIM] = k_ref[...]
    ka[:, HEAD_DIM:] = jnp.where((col == row_blk) | (col == nbp) | (col == nbp + 1), 1.0, 0.0).astype(BF16)
    va[:, 0:HEAD_DIM] = v_ref[...]
    va[:, HEAD_DIM:] = jnp.ones((seq, LANES), BF16)

    far_hi, far_lo = _far_parts(far_ref[h], (SUBLANES, blk))
    r8 = lax.broadcasted_iota(jnp.int32, (SUBLANES, blk), 0)
    far_rows = jnp.where(r8 == 0, far_hi, jnp.where(r8 == 1, far_lo, 0.0))
    rowi = lax.broadcasted_iota(jnp.int32, (nbp, blk), 0)
    km = km_ref[0, 0, 0:nbp, :]

    for qi in range(nb):
        par = qi % 2
        n = (qi + 1) * blk
        qt = q_ref[qi * blk:n, :]

        if qi > MOBA_TOPK:
            st = lax.dot_general(km, qt, _NT, precision=lax.Precision.HIGHEST, preferred_element_type=F32)
            rank = jnp.zeros((nbp, blk), jnp.int32)
            for i in range(qi):
                si = st[i:i + 1, :]
                rank = rank + ((si > st) | ((si == st) & (rowi > i))).astype(jnp.int32)
            dropped = (rowi < qi) & (rank >= MOBA_TOPK)
            selv = jnp.where(dropped, NEG, 0.0)
        else:
            selv = jnp.zeros((nbp, blk), F32)
        aug = jnp.concatenate([selv, far_rows, jnp.zeros((LANES - nbp - SUBLANES, blk), F32)], axis=0)
        qa[par, :, HEAD_DIM:] = aug.T.astype(BF16)
        qa[par, :, 0:HEAD_DIM] = (qt * (HEAD_DIM ** -0.5)).astype(BF16)

        m_acc = jnp.full((blk, LANES), NEG, F32)
        for c0 in range(0, n, KEY_CHUNK):
            w = min(KEY_CHUNK, n - c0)
            sc = lax.dot_general(qa[par], ka[c0:c0 + w, :], _NT, preferred_element_type=F32)
            pieces = []
            for t0 in range(0, w, blk):
                piece = sc[:, t0:t0 + blk]
                kb = (c0 + t0) // blk
                if kb == qi:
                    piece = piece + bias_ref[0, 0]
                elif kb == qi - 1:
                    piece = piece + bias_ref[0, 1]
                pieces.append(piece)
                for l0 in range(0, blk, LANES):
                    m_acc = jnp.maximum(m_acc, piece[:, l0:l0 + LANES])
            s_scr[par, :, c0:c0 + w] = pieces[0] if len(pieces) == 1 else jnp.concatenate(pieces, axis=1)
        m = jnp.max(m_acc, axis=1, keepdims=True)
        for c0 in range(0, n, KEY_CHUNK):
            w = min(KEY_CHUNK, n - c0)
            p_scr[par, :, c0:c0 + w] = jnp.exp(s_scr[par, :, c0:c0 + w] - m).astype(BF16)
        out = jnp.dot(p_scr[par, :, 0:n], va[0:n, :], preferred_element_type=F32)
        o_ref[qi * blk:n, :] = (out[:, 0:HEAD_DIM] / out[:, HEAD_DIM:HEAD_DIM + 1]).astype(o_ref.dtype)


def _prompt_attention(q, kc, vc, kmeans, bias, far_bias, batch, seq):
    nb = seq // MOBA_BLOCK
    nbp = -(-nb // SUBLANES) * SUBLANES
    assert nbp + SUBLANES <= LANES and nb + 2 <= LANES and (MOBA_BLOCK % KEY_CHUNK == 0 or KEY_CHUNK % MOBA_BLOCK == 0)
    seq_blk = lambda b, h: (b, h)
    return pl.pallas_call(
        functools.partial(_attn_kernel, nb=nb),
        grid=(batch, N_HEADS),
        in_specs=[pl.BlockSpec(memory_space=pltpu.SMEM),
                  pl.BlockSpec((seq, HEAD_DIM), seq_blk),
                  pl.BlockSpec((seq, HEAD_DIM), seq_blk),
                  pl.BlockSpec((seq, HEAD_DIM), seq_blk),
                  pl.BlockSpec((1, 1, LANES, HEAD_DIM), lambda b, h: (b, h, 0, 0)),
                  pl.BlockSpec((1, 2, MOBA_BLOCK, MOBA_BLOCK), lambda b, h: (h, 0, 0, 0))],
        out_specs=pl.BlockSpec((seq, HEAD_DIM), seq_blk),
        out_shape=jax.ShapeDtypeStruct((batch * seq, ATTN_WIDTH), BF16),
        scratch_shapes=[pltpu.VMEM((seq, 2 * HEAD_DIM), BF16),
                        pltpu.VMEM((seq, 2 * HEAD_DIM), BF16),
                        pltpu.VMEM((2, MOBA_BLOCK, 2 * HEAD_DIM), BF16),
                        pltpu.VMEM((2, MOBA_BLOCK, seq), F32),
                        pltpu.VMEM((2, MOBA_BLOCK, seq), BF16)],
        compiler_params=_params(("parallel", "parallel")),
        name="prompt_attn",
    )(far_bias, q, kc, vc, kmeans, bias)


PAGES_PER_STEP = 8


def _pagesum_kernel(pt_ref, *refs):
    o_ref = refs[-1]
    pages_per_block = PAGES_PER_STEP // o_ref.shape[1]
    for c in range(o_ref.shape[1]):
        tot = None
        for r in range(pages_per_block):
            part = jnp.sum(refs[c * pages_per_block + r][0], axis=0)
            tot = part if tot is None else tot + part
        o_ref[0, c] = tot


def _page_block_sums(cache_k, page_table, page_size):
    db, n_pages = page_table.shape
    ppb = MOBA_BLOCK // page_size
    assert MOBA_BLOCK % page_size == 0 and PAGES_PER_STEP % ppb == 0 and n_pages % PAGES_PER_STEP == 0
    bps = PAGES_PER_STEP // ppb

    def page_spec(r):
        return pl.BlockSpec((1, page_size, N_HEADS, HEAD_DIM),
                            lambda b, i, pt: (pt[b, i * PAGES_PER_STEP + r], 0, 0, 0))

    grid_spec = pltpu.PrefetchScalarGridSpec(
        num_scalar_prefetch=1,
        grid=(db, n_pages // PAGES_PER_STEP),
        in_specs=[page_spec(r) for r in range(PAGES_PER_STEP)],
        out_specs=pl.BlockSpec((1, bps, N_HEADS, HEAD_DIM), lambda b, i, pt: (b, i, 0, 0)))
    return pl.pallas_call(
        _pagesum_kernel,
        grid_spec=grid_spec,
        out_shape=jax.ShapeDtypeStruct((db, n_pages // ppb, N_HEADS, HEAD_DIM), F32),
        compiler_params=_params(("parallel", "arbitrary")),
        name="page_block_sums",
    )(page_table, *([cache_k] * PAGES_PER_STEP))


def _sselect_kernel(q_ref, km_ref, o_ref, *, own):
    s_new = q_ref.shape[0]
    col = lax.broadcasted_iota(jnp.int32, (s_new, LANES), 1)
    colf = col.astype(F32)
    for h in range(N_HEADS):
        qh = q_ref[:, h * HEAD_DIM:(h + 1) * HEAD_DIM]
        s = lax.dot_general(qh, km_ref[0, h], _NT, precision=lax.Precision.HIGHEST, preferred_element_type=F32)
        s = jnp.where(col < own, s, NEG)
        out = jnp.zeros((s_new, LANES), F32)
        for t in range(MOBA_TOPK):
            mx = jnp.max(s, axis=1, keepdims=True)
            arg = jnp.min(jnp.where(s == mx, colf, float(LANES)), axis=1, keepdims=True)
            out = jnp.where(col == t, arg, out)
            s = jnp.where(colf == arg, -jnp.inf, s)
        o_ref[0, h] = out.astype(jnp.int32)


def _sample_select(q, kmeans_s, db, s_new, own):
    assert MOBA_TOPK <= own < LANES
    return pl.pallas_call(
        functools.partial(_sselect_kernel, own=own),
        grid=(db,),
        in_specs=[pl.BlockSpec((s_new, ATTN_WIDTH), lambda b: (b, 0)),
                  pl.BlockSpec((1, N_HEADS, LANES, HEAD_DIM), lambda b: (b, 0, 0, 0))],
        out_specs=pl.BlockSpec((1, N_HEADS, s_new, LANES), lambda b: (b, 0, 0, 0)),
        out_shape=jax.ShapeDtypeStruct((db, N_HEADS, s_new, LANES), jnp.int32),
        compiler_params=_params(("parallel",)),
        name="sample_select",
    )(q, kmeans_s)


def _sattn_kernel(pt_ref, idx_ref, far_ref, q_ref, kn_ref, vn_ref, bias_ref, ck_hbm, cv_hbm, o_ref,
                  kbuf, vbuf, sem, *, n_pages, page_size, own, s_new):
    b = pl.program_id(0)
    h = pl.program_id(1)
    nh = pl.num_programs(1)
    step = b * nh + h
    nsteps = pl.num_programs(0) * nh
    slot = step % 2
    ppb = MOBA_BLOCK // page_size
    nsel = s_new * MOBA_TOPK
    nkeys = nsel * MOBA_BLOCK

    def copies(step_, slot_):
        b_ = step_ // nh
        h_ = step_ % nh
        out = []
        for c in range(nsel):
            blk = idx_ref[step_ * nsel + c]
            for r in range(ppb):
                phys = pt_ref[b_ * n_pages + blk * ppb + r]
                dst = pl.ds((c * ppb + r) * page_size, page_size)
                out.append(pltpu.make_async_copy(ck_hbm.at[phys, :, h_, :], kbuf.at[slot_, dst, :], sem.at[slot_, 0]))
                out.append(pltpu.make_async_copy(cv_hbm.at[phys, :, h_, :], vbuf.at[slot_, dst, :], sem.at[slot_, 1]))
        return out

    @pl.when(step == 0)
    def _():
        for cp in copies(step, slot):
            cp.start()

    @pl.when(step + 1 < nsteps)
    def _():
        for cp in copies(step + 1, 1 - slot):
            cp.start()

    for cp in copies(step, slot):
        cp.wait()

    far_hi, far_lo = _far_parts(far_ref[h], (s_new, LANES))
    far = (far_hi + far_lo)[:, 0:1]
    qb = (q_ref[...] * (HEAD_DIM ** -0.5)).astype(BF16)
    kb = kbuf[slot].astype(BF16)
    vb = vbuf[slot].astype(BF16)
    lg = lax.dot_general(qb, kb, _NT, preferred_element_type=F32) + far
    pieces = []
    for c in range(nsel):
        is_prev = idx_ref[step * nsel + c] == own - 1
        row = bias_ref[0, 1, (c // MOBA_TOPK):(c // MOBA_TOPK) + 1, :]
        pieces.append(jnp.where(is_prev, row, 0.0))
    lg = lg + jnp.concatenate(pieces, axis=1)
    col = lax.broadcasted_iota(jnp.int32, (s_new, nkeys), 1)
    row_i = lax.broadcasted_iota(jnp.int32, (s_new, nkeys), 0)
    mine = (col >= row_i * (MOBA_TOPK * MOBA_BLOCK)) & (col < (row_i + 1) * (MOBA_TOPK * MOBA_BLOCK))
    lg = jnp.where(mine, lg, NEG)
    lg_own = (lax.dot_general(qb, kn_ref[...].astype(BF16), _NT, preferred_element_type=F32)
              + far + bias_ref[0, 0, 0:s_new, 0:s_new])
    m = jnp.maximum(jnp.max(lg, axis=1, keepdims=True), jnp.max(lg_own, axis=1, keepdims=True))
    p = jnp.exp(lg - m)
    p_own = jnp.exp(lg_own - m)
    denom = jnp.sum(p, axis=1, keepdims=True) + jnp.sum(p_own, axis=1, keepdims=True)
    acc = (jnp.dot(p.astype(BF16), vb, preferred_element_type=F32)
           + jnp.dot(p_own.astype(BF16), vn_ref[...].astype(BF16), preferred_element_type=F32))
    o_ref[...] = acc / denom


def _sample_attention(q, k, v, cache_k, cache_v, page_table, sel_idx, bias, far_bias, s_new, own):
    db, n_pages = page_table.shape
    page_size = cache_k.shape[1]
    nkeys = s_new * MOBA_TOPK * MOBA_BLOCK
    new_spec = pl.BlockSpec((s_new, HEAD_DIM), lambda b, h, pt, ix: (b, h))
    grid_spec = pltpu.PrefetchScalarGridSpec(
        num_scalar_prefetch=2,
        grid=(db, N_HEADS),
        in_specs=[pl.BlockSpec(memory_space=pltpu.SMEM),
                  new_spec, new_spec, new_spec,
                  pl.BlockSpec((1, 2, MOBA_BLOCK, MOBA_BLOCK), lambda b, h, pt, ix: (h, 0, 0, 0)),
                  pl.BlockSpec(memory_space=pl.ANY),
                  pl.BlockSpec(memory_space=pl.ANY)],
        out_specs=new_spec,
        scratch_shapes=[pltpu.VMEM((2, nkeys, HEAD_DIM), F32),
                        pltpu.VMEM((2, nkeys, HEAD_DIM), F32),
                        pltpu.SemaphoreType.DMA((2, 2))])
    return pl.pallas_call(
        functools.partial(_sattn_kernel, n_pages=n_pages, page_size=page_size, own=own, s_new=s_new),
        grid_spec=grid_spec,
        out_shape=jax.ShapeDtypeStruct((db * s_new, ATTN_WIDTH), F32),
        compiler_params=_params(("arbitrary", "arbitrary")),
        name="sample_attn",
    )(page_table.reshape(-1), sel_idx.reshape(-1), far_bias, q, k, v, bias, cache_k, cache_v)


def _conv_kernel(halo_ref, cur_ref, cw_ref, cb_ref, lg_ref, lb_ref, o_ref, ext_scr, *, tt, rc, zero_first):
    i = pl.program_id(1)
    if zero_first:
        @pl.when(i == 0)
        def _():
            ext_scr[0:CONV_HALO, :] = jnp.zeros((CONV_HALO, CONV_CH), F32)

        @pl.when(i > 0)
        def _():
            ext_scr[0:CONV_HALO, :] = halo_ref[...]
    else:
        ext_scr[0:CONV_HALO, :] = halo_ref[...]
    ext_scr[CONV_HALO:CONV_HALO + tt, :] = cur_ref[...]
    lead = CONV_HALO - (CONV_WIDTH - 1)

    def chunk(c, carry):
        r0 = pl.multiple_of(c * rc, rc)
        win = ext_scr[pl.ds(r0, rc + CONV_HALO), :]
        acc = jnp.zeros((rc, CONV_CH), F32)
        for r in range(SUBLANES):
            rolled = win if r == 0 else pltpu.roll(win, rc + CONV_HALO - r, axis=0)
            for w in range(CONV_WIDTH):
                if (lead + w) % SUBLANES == r:
                    a0 = lead + w - r
                    acc = acc + rolled[a0:a0 + rc, :] * cw_ref[w:w + 1, :]
        hc = acc + cb_ref[...]
        mu = jnp.mean(hc, axis=-1, keepdims=True)
        var = jnp.mean(jnp.square(hc - mu), axis=-1, keepdims=True)
        y = (hc - mu) * lax.rsqrt(var + EPS) * lg_ref[...] + lb_ref[...]
        o_ref[pl.ds(r0, rc), :] = (y * _sigmoid(y)).astype(o_ref.dtype)
        return carry

    lax.fori_loop(0, tt // rc, chunk, 0)


def _conv_module(halo_src, halo_map, cur_src, conv_w, conv_b, ln_g, ln_b, n_seq, n_tiles, tt,
                 zero_first, out_dtype):
    rc = min(tt, 32)
    vec = pl.BlockSpec((1, CONV_CH), lambda b, i: (0, 0))
    cur_map = lambda b, i: (b * n_tiles + i, 0)
    return pl.pallas_call(
        functools.partial(_conv_kernel, tt=tt, rc=rc, zero_first=zero_first),
        grid=(n_seq, n_tiles),
        in_specs=[pl.BlockSpec((CONV_HALO, CONV_CH), halo_map),
                  pl.BlockSpec((tt, CONV_CH), cur_map),
                  pl.BlockSpec((CONV_WIDTH, CONV_CH), lambda b, i: (0, 0)),
                  vec, vec, vec],
        out_specs=pl.BlockSpec((tt, CONV_CH), cur_map),
        out_shape=jax.ShapeDtypeStruct((n_seq * n_tiles * tt, CONV_CH), out_dtype),
        scratch_shapes=[pltpu.VMEM((CONV_HALO + tt, CONV_CH), F32)],
        compiler_params=_params(("parallel", "arbitrary")),
        name="conv_module",
    )(halo_src, cur_src, conv_w, conv_b, ln_g, ln_b)


def _mix_kernel(xn_ref, at_ref, cv_ref, wga_ref, wgc_ref, bga_ref, bgc_ref, wa_ref, wc_ref, o_ref):
    xn = xn_ref[...]
    ga = _sigmoid(jnp.dot(xn, wga_ref[...], preferred_element_type=F32) + bga_ref[...])
    gc = _sigmoid(jnp.dot(xn, wgc_ref[...], preferred_element_type=F32) + bgc_ref[...])
    ya = jnp.dot(at_ref[...].astype(BF16), wa_ref[...], preferred_element_type=F32)
    yc = jnp.dot(cv_ref[...].astype(BF16), wc_ref[...], preferred_element_type=F32)
    o_ref[...] = (ga * ya + gc * yc).astype(o_ref.dtype)


def _gate_mix(xn, attn, cva, w_gate, b_gate, w_attn_out, w_conv_out, tm, tn=512):
    n = xn.shape[0]
    nj = D_MODEL // tn
    row = lambda i, j: (i, 0)
    return pl.pallas_call(
        _mix_kernel,
        grid=(n // tm, nj),
        in_specs=[pl.BlockSpec((tm, D_MODEL), row),
                  pl.BlockSpec((tm, ATTN_WIDTH), row),
                  pl.BlockSpec((tm, CONV_CH), row),
                  pl.BlockSpec((D_MODEL, tn), lambda i, j: (0, j)),
                  pl.BlockSpec((D_MODEL, tn), lambda i, j: (0, nj + j)),
                  pl.BlockSpec((1, tn), lambda i, j: (0, j)),
                  pl.BlockSpec((1, tn), lambda i, j: (0, nj + j)),
                  pl.BlockSpec((ATTN_WIDTH, tn), lambda i, j: (0, j)),
                  pl.BlockSpec((CONV_CH, tn), lambda i, j: (0, j))],
        out_specs=pl.BlockSpec((tm, tn), lambda i, j: (i, j)),
        out_shape=jax.ShapeDtypeStruct((n, D_MODEL), BF16),
        compiler_params=_params(("parallel", "arbitrary")),
        name="gate_mix",
    )(xn, attn, cva, w_gate, w_gate, b_gate, b_gate, w_attn_out, w_conv_out)


def _outproj_kernel(x_ref, mix_ref, wo_ref, g2_ref, wrh_ref, wrl_ref, br_ref, cin_ref,
                    x1_ref, xn2_ref, rt_ref, cout_ref, run):
    @pl.when(pl.program_id(0) == 0)
    def _():
        run[...] = cin_ref[...]

    x1 = x_ref[...] + jnp.dot(mix_ref[...], wo_ref[...], preferred_element_type=F32)
    x1_ref[...] = x1
    xn2 = x1 * lax.rsqrt(jnp.mean(x1 * x1, axis=-1, keepdims=True) + EPS) * g2_ref[...]
    xn2_ref[...] = xn2
    hi, lo = _split_bf16(xn2)
    wrh = wrh_ref[...]
    logits = (jnp.dot(hi, wrh, preferred_element_type=F32) + jnp.dot(lo, wrh, preferred_element_type=F32)
              + jnp.dot(hi, wrl_ref[...], preferred_element_type=F32) + br_ref[...])
    tm = logits.shape[0]
    col = lax.broadcasted_iota(jnp.int32, (tm, LANES), 1).astype(F32)
    big = float(LANES)

    def first_argmax(vals):
        mx = jnp.max(vals, axis=1, keepdims=True)
        return mx, jnp.min(jnp.where(vals == mx, col, big), axis=1, keepdims=True)

    gmask = col < N_GROUPS
    gmax, gidx = first_argmax(jnp.where(gmask, logits, -jnp.inf))
    p_g = 1.0 / jnp.sum(jnp.where(gmask, jnp.exp(logits - gmax), 0.0), axis=1, keepdims=True)
    e_lo = N_GROUPS + gidx * EXPERTS_PER_GROUP
    el = jnp.where((col >= e_lo) & (col < e_lo + EXPERTS_PER_GROUP), logits, -jnp.inf)
    v1, i1 = first_argmax(el)
    v2, i2 = first_argmax(jnp.where(col == i1, -jnp.inf, el))
    e21 = jnp.exp(v2 - v1)
    p1 = 1.0 / (1.0 + e21)
    p2 = e21 / (1.0 + e21)
    e1 = i1 - N_GROUPS
    e2 = i2 - N_GROUPS

    oh1 = (col == e1).astype(F32)
    oh2 = (col == e2).astype(F32)
    both = oh1 + oh2
    ri = lax.broadcasted_iota(jnp.int32, (tm, tm), 0)
    ci = lax.broadcasted_iota(jnp.int32, (tm, tm), 1)
    lower = (ci < ri).astype(BF16)
    before = jnp.dot(lower, both.astype(BF16), preferred_element_type=F32) + run[...]
    pos1 = jnp.sum(before * oh1, axis=1, keepdims=True)
    pos2 = jnp.sum(before * oh2, axis=1, keepdims=True)
    new_run = run[...] + jnp.sum(both, axis=0, keepdims=True)
    run[...] = new_run
    cout_ref[...] = new_run

    rt = jnp.zeros((tm, LANES), F32)
    for lane, val in enumerate((e1, e2, p_g * p1, p_g * p2, pos1, pos2)):
        rt = jnp.where(col == lane, val, rt)
    rt_ref[...] = rt


def _out_proj(x, mix, w_out, g2, wr_hi, wr_lo, br, cnt_in, tm):
    n = x.shape[0]
    row = lambda i: (i, 0)
    const = lambda i: (0, 0)
    return pl.pallas_call(
        _outproj_kernel,
        grid=(n // tm,),
        in_specs=[pl.BlockSpec((tm, D_MODEL), row),
                  pl.BlockSpec((tm, D_MODEL), row),
                  pl.BlockSpec((D_MODEL, D_MODEL), const),
                  pl.BlockSpec((1, D_MODEL), const),
                  pl.BlockSpec((D_MODEL, LANES), const),
                  pl.BlockSpec((D_MODEL, LANES), const),
                  pl.BlockSpec((1, LANES), const),
                  pl.BlockSpec((1, LANES), const)],
        out_specs=[pl.BlockSpec((tm, D_MODEL), row),
                   pl.BlockSpec((tm, D_MODEL), row),
                   pl.BlockSpec((tm, LANES), row),
                   pl.BlockSpec((1, LANES), const)],
        out_shape=[jax.ShapeDtypeStruct((n, D_MODEL), F32),
                   jax.ShapeDtypeStruct((n, D_MODEL), F32),
                   jax.ShapeDtypeStruct((n, LANES), F32),
                   jax.ShapeDtypeStruct((1, LANES), F32)],
        scratch_shapes=[pltpu.VMEM((1, LANES), F32)],
        compiler_params=_params(("arbitrary",)),
        name="out_proj",
    )(x, mix, w_out, g2, wr_hi, wr_lo, br, cnt_in)


def _dispatch_kernel(dest_ref, x_ref, xb_in, xb_out, sem, *, tm):
    del xb_in
    i = pl.program_id(0)

    def row_copy(t, d):
        return pltpu.make_async_copy(x_ref.at[pl.ds(t, 1), :], xb_out.at[pl.ds(d, 1), :], sem)

    def issue(t, carry):
        base = (i * tm + t) * 2
        row_copy(t, dest_ref[base]).start()
        row_copy(t, dest_ref[base + 1]).start()
        return carry

    lax.fori_loop(0, tm, issue, 0)

    def drain(t, carry):
        row_copy(0, 0).wait()
        row_copy(0, 0).wait()
        return carry

    lax.fori_loop(0, tm, drain, 0)


def _dispatch(xn2, dest, xb, tm):
    n = xn2.shape[0]
    grid_spec = pltpu.PrefetchScalarGridSpec(
        num_scalar_prefetch=1,
        grid=(n // tm,),
        in_specs=[pl.BlockSpec((tm, D_MODEL), lambda i, d: (i, 0)),
                  pl.BlockSpec(memory_space=pl.ANY)],
        out_specs=pl.BlockSpec(memory_space=pl.ANY),
        scratch_shapes=[pltpu.SemaphoreType.DMA(())])
    return pl.pallas_call(
        functools.partial(_dispatch_kernel, tm=tm),
        grid_spec=grid_spec,
        out_shape=jax.ShapeDtypeStruct(xb.shape, F32),
        input_output_aliases={2: 0},
        compiler_params=_params(("arbitrary",)),
        name="moe_dispatch",
    )(dest, xn2, xb)


def _moe_kernel(be_ref, nu_ref, x_ref, w1_ref, w3_ref, w2_ref, o_ref, w1b, w3b, w2b):
    g = pl.program_id(0)
    used = g < nu_ref[0]
    new_expert = (g == 0) | (be_ref[g] != be_ref[jnp.maximum(g - 1, 0)])

    @pl.when(used & new_expert)
    def _():
        w1b[...] = w1_ref[0].astype(BF16)
        w3b[...] = w3_ref[0].astype(BF16)
        w2b[...] = w2_ref[0].astype(BF16)

    @pl.when(used)
    def _():
        xb = x_ref[...].astype(BF16)
        a = jnp.dot(xb, w1b[...], preferred_element_type=F32)
        gt = jnp.dot(xb, w3b[...], preferred_element_type=F32)
        hmid = (a * _sigmoid(a) * gt).astype(BF16)
        o_ref[...] = jnp.dot(hmid, w2b[...], preferred_element_type=F32)

    @pl.when(jnp.logical_not(used))
    def _():
        o_ref[...] = jnp.zeros(o_ref.shape, F32)


def _moe_experts(xb, block_e, n_used, w1, w3, w2):
    n_blocks = xb.shape[0] // MOE_ROWS
    blk = lambda g, be, nu: (jnp.minimum(g, nu[0] - 1), 0)
    wsel = lambda g, be, nu: (be[jnp.minimum(g, nu[0] - 1)], 0, 0)
    grid_spec = pltpu.PrefetchScalarGridSpec(
        num_scalar_prefetch=2,
        grid=(n_blocks,),
        in_specs=[pl.BlockSpec((MOE_ROWS, D_MODEL), blk),
                  pl.BlockSpec((1, D_MODEL, D_EXPERT), wsel),
                  pl.BlockSpec((1, D_MODEL, D_EXPERT), wsel),
                  pl.BlockSpec((1, D_EXPERT, D_MODEL), wsel)],
        out_specs=pl.BlockSpec((MOE_ROWS, D_MODEL), lambda g, be, nu: (g, 0)),
        scratch_shapes=[pltpu.VMEM((D_MODEL, D_EXPERT), BF16),
                        pltpu.VMEM((D_MODEL, D_EXPERT), BF16),
                        pltpu.VMEM((D_EXPERT, D_MODEL), BF16)])
    return pl.pallas_call(
        _moe_kernel,
        grid_spec=grid_spec,
        out_shape=jax.ShapeDtypeStruct(xb.shape, F32),
        compiler_params=_params(("arbitrary",)),
        name="moe_experts",
    )(block_e, n_used, xb, w1, w3, w2)


def _combine_kernel(dest_ref, x1_ref, rt_ref, gf_ref, yb_hbm, o_ref, ybuf, sem, *, tm):
    i = pl.program_id(0)

    def row_copy(k, t, d):
        return pltpu.make_async_copy(yb_hbm.at[pl.ds(d, 1), :], ybuf.at[k, pl.ds(t, 1), :], sem)

    def issue(t, carry):
        base = (i * tm + t) * 2
        row_copy(0, t, dest_ref[base]).start()
        row_copy(1, t, dest_ref[base + 1]).start()
        return carry

    lax.fori_loop(0, tm, issue, 0)

    def drain(t, carry):
        row_copy(0, 0, 0).wait()
        row_copy(0, 0, 0).wait()
        return carry

    lax.fori_loop(0, tm, drain, 0)
    rt = rt_ref[...]
    moe = ybuf[0] * rt[:, 2:3] + ybuf[1] * rt[:, 3:4]
    xo = x1_ref[...] + moe
    o_ref[...] = xo * lax.rsqrt(jnp.mean(xo * xo, axis=-1, keepdims=True) + EPS) * gf_ref[...]


def _combine(x1, route, gf, yb, dest, tm):
    n = x1.shape[0]
    grid_spec = pltpu.PrefetchScalarGridSpec(
        num_scalar_prefetch=1,
        grid=(n // tm,),
        in_specs=[pl.BlockSpec((tm, D_MODEL), lambda i, d: (i, 0)),
                  pl.BlockSpec((tm, LANES), lambda i, d: (i, 0)),
                  pl.BlockSpec((1, D_MODEL), lambda i, d: (0, 0)),
                  pl.BlockSpec(memory_space=pl.ANY)],
        out_specs=pl.BlockSpec((tm, D_MODEL), lambda i, d: (i, 0)),
        scratch_shapes=[pltpu.VMEM((2, tm, D_MODEL), F32),
                        pltpu.SemaphoreType.DMA(())])
    return pl.pallas_call(
        functools.partial(_combine_kernel, tm=tm),
        grid_spec=grid_spec,
        out_shape=jax.ShapeDtypeStruct((n, D_MODEL), F32),
        compiler_params=_params(("arbitrary",)),
        name="moe_combine",
    )(dest, x1, route, gf, yb)


def _dispatch_plan(routes, counts):
    counts = counts[0, :N_EXPERTS].astype(jnp.int32)
    pcounts = (counts + MOE_ROWS - 1) // MOE_ROWS * MOE_ROWS
    pend = jnp.cumsum(pcounts)
    pstart = pend - pcounts
    dests = []
    n_assign = 0
    for rt in routes:
        e = rt[:, 0:2].astype(jnp.int32)
        pos = rt[:, 4:6].astype(jnp.int32)
        onehot = e[:, :, None] == jnp.arange(N_EXPERTS, dtype=jnp.int32)[None, None, :]
        dests.append((jnp.sum(jnp.where(onehot, pstart[None, None, :], 0), axis=-1) + pos).reshape(-1))
        n_assign += e.size
    n_blocks = (n_assign + N_EXPERTS * (MOE_ROWS - 1) + MOE_ROWS - 1) // MOE_ROWS
    block_e = jnp.minimum(jnp.searchsorted(pend, jnp.arange(n_blocks, dtype=jnp.int32) * MOE_ROWS, side='right'),
                          N_EXPERTS - 1).astype(jnp.int32)
    n_used = (pend[-1] // MOE_ROWS).astype(jnp.int32).reshape(1)
    return dests, block_e, n_used, n_blocks * MOE_ROWS


def kernel(x_prompt, x_sample, cache_k, cache_v, state_conv, page_table, rel_bias, norm1_g, w_in, w_attn_out,
           conv_w, conv_b, conv_ln_g, conv_ln_b, w_conv_out, w_gate, b_gate, w_out, norm2_g, router_g_w,
           router_g_b, router_e_w, router_e_b, moe_w1, moe_w3, moe_w2, norm_f_g):
    depth = norm1_g.shape[0]
    assert depth == 1
    batch, seq, _ = x_prompt.shape
    db, s_new, _ = x_sample.shape
    n_pages = page_table.shape[1]
    page_size = cache_k.shape[2]
    past_len = n_pages * page_size
    assert seq % MOBA_BLOCK == 0 and past_len % MOBA_BLOCK == 0 and s_new <= MOBA_BLOCK and s_new % SUBLANES == 0
    own_s = past_len // MOBA_BLOCK
    n_p = batch * seq
    n_s = db * s_new
    tm_p = 512
    tm_s = n_s
    assert n_p % tm_p == 0 and n_s % MOBA_BLOCK == 0

    half = CONV_CH // 2
    wi = w_in[0]
    u0 = 3 * ATTN_WIDTH
    wcat = jnp.concatenate(
        [wi[:, :u0], wi[:, u0:u0 + half], wi[:, u0 + CONV_CH:u0 + CONV_CH + half],
         wi[:, u0 + half:u0 + CONV_CH], wi[:, u0 + CONV_CH + half:]], axis=1).astype(BF16)
    wr = jnp.concatenate([router_g_w[0], router_e_w[0]], axis=1)
    wr = jnp.pad(wr, ((0, 0), (0, LANES - wr.shape[1])))
    wr_hi, wr_lo = _split_bf16(wr)
    br = jnp.pad(jnp.concatenate([router_g_b[0], router_e_b[0]]), (0, LANES - N_GROUPS - N_EXPERTS))[None, :]
    far_bias = rel_bias[NUM_BUCKETS - 1, :]
    wg_b, wa_b, wc_b, wo_b = (w[0].astype(BF16) for w in (w_gate, w_attn_out, w_conv_out, w_out))
    cw = conv_w[0]
    bias = _bias_tiles(rel_bias)

    xn_p, q_p, kc_p, vc_p, u_p, ksum, k_p, v_p = _proj(x_prompt.reshape(n_p, D_MODEL), norm1_g, wcat, tm_p, BF16)
    xn_s, q_s, kc_s, vc_s, u_s, _, k_s, v_s = _proj(x_sample.reshape(n_s, D_MODEL), norm1_g, wcat, tm_s, F32)

    nb = seq // MOBA_BLOCK
    km = ksum.reshape(batch, nb, N_HEADS, HEAD_DIM) / MOBA_BLOCK
    km = jnp.pad(km.transpose(0, 2, 1, 3), ((0, 0), (0, 0), (0, LANES - nb), (0, 0)))
    attn_p = _prompt_attention(q_p, kc_p, vc_p, km, bias, far_bias, batch, seq)

    ck, cv = cache_k[0], cache_v[0]
    bsum = _page_block_sums(ck, page_table, page_size)
    k_new = k_s.reshape(db, s_new, N_HEADS, HEAD_DIM)
    sums = jnp.concatenate([bsum, jnp.sum(k_new, axis=1)[:, None]], axis=1)
    km_s = jnp.pad((sums / MOBA_BLOCK).transpose(0, 2, 1, 3), ((0, 0), (0, 0), (0, LANES - own_s - 1), (0, 0)))
    sel_idx = _sample_select(q_s, km_s, db, s_new, own_s)[..., :MOBA_TOPK]
    attn_s = _sample_attention(q_s, kc_s, vc_s, ck, cv, page_table, sel_idx, bias, far_bias, s_new, own_s)

    tt = 512
    tiles = seq // tt
    per_halo = tt // CONV_HALO
    cva_p = _conv_module(u_p, lambda b, i: (jnp.maximum((b * tiles + i) * per_halo - 1, 0), 0), u_p,
                         cw, conv_b, conv_ln_g, conv_ln_b, batch, tiles, tt, True, BF16)
    lead = CONV_HALO - (CONV_WIDTH - 1)
    halo_s = jnp.pad(state_conv[0], ((0, 0), (lead, 0), (0, 0))).reshape(db * CONV_HALO, CONV_CH)
    cva_s = _conv_module(halo_s, lambda b, i: (b, 0), u_s,
                         cw, conv_b, conv_ln_g, conv_ln_b, db, 1, s_new, False, F32)

    mix_p = _gate_mix(xn_p, attn_p, cva_p, wg_b, b_gate, wa_b, wc_b, tm_p)
    mix_s = _gate_mix(xn_s, attn_s, cva_s, wg_b, b_gate, wa_b, wc_b, tm_s)
    zero_cnt = jnp.zeros((1, LANES), F32)
    x1_p, xn2_p, rt_p, cnt_p = _out_proj(x_prompt.reshape(n_p, D_MODEL), mix_p, wo_b, norm2_g, wr_hi, wr_lo, br,
                                         zero_cnt, 256)
    x1_s, xn2_s, rt_s, cnt = _out_proj(x_sample.reshape(n_s, D_MODEL), mix_s, wo_b, norm2_g, wr_hi, wr_lo, br,
                                       cnt_p, 256)

    (dest_p, dest_s), block_e, n_used, n_rows = _dispatch_plan((rt_p, rt_s), cnt)
    xb = _dispatch(xn2_p, dest_p, jnp.zeros((n_rows, D_MODEL), F32), 256)
    xb = _dispatch(xn2_s, dest_s, xb, 256)
    yb = _moe_experts(xb, block_e, n_used, moe_w1[0], moe_w3[0], moe_w2[0])
    gf = norm_f_g[None, :]
    y_p = _combine(x1_p, rt_p, gf, yb, dest_p, 256)
    y_s = _combine(x1_s, rt_s, gf, yb, dest_s, 256)

    conv_prompt = u_p.reshape(batch, seq, CONV_CH)[:, seq - (CONV_WIDTH - 1):]
    conv_sample = jnp.concatenate([state_conv[0], u_s.reshape(db, s_new, CONV_CH)], axis=1)[:, -(CONV_WIDTH - 1):]
    shp_p = (1, batch, seq, N_HEADS, HEAD_DIM)
    shp_s = (1, db, s_new, N_HEADS, HEAD_DIM)
    return (y_p.reshape(batch, seq, D_MODEL), y_s.reshape(db, s_new, D_MODEL),
            k_p.reshape(shp_p), v_p.reshape(shp_p), conv_prompt[None],
            k_s.reshape(shp_s), v_s.reshape(shp_s), conv_sample[None])
```

```python
import functools
import math

import jax
import jax.numpy as jnp
from jax import lax
from jax.experimental import pallas as pl
from jax.experimental.pallas import tpu as pltpu

D_MODEL = 2048
HEAD_DIM = 128
N_HEADS = 8
ATTN_WIDTH = N_HEADS * HEAD_DIM
CONV_CH = 1024
CONV_WIDTH = 31
MOBA_BLOCK = 256
MOBA_TOPK = 3
NUM_BUCKETS = 32
MAX_DISTANCE = 128
N_GROUPS = 4
EXPERTS_PER_GROUP = 8
N_EXPERTS = N_GROUPS * EXPERTS_PER_GROUP
D_EXPERT = 512
EPS = 1e-6
NEG = -1e30

LANES = 128
SUBLANES = 8
CONV_HALO = 32
ROW_TILE = 256
PAGES_PER_CHUNK = 8
DMA_UNROLL = 8
MOE_ROWS = 256
KEY_CHUNK = 512
VMEM_LIMIT = 56 * 1024 * 1024

F32 = jnp.float32
BF16 = jnp.bfloat16
_NT = (((1,), (1,)), ((), ()))


def _params(sem, vmem=VMEM_LIMIT):
    return pltpu.CompilerParams(dimension_semantics=sem, vmem_limit_bytes=vmem)


def _sigmoid(x):
    return 1.0 / (1.0 + jnp.exp(-x))


def _split_bf16(x):
    hi = x.astype(BF16)
    return hi, (x - hi.astype(F32)).astype(BF16)


def _resident(shape):
    return pl.BlockSpec(shape, lambda *_: (0,) * len(shape), pipeline_mode=pl.Buffered(1))


class _ScanRider:
    def __init__(self, pt_ref, ck_hbm, bsum_hbm, pgbuf, sumst, psem, osem, *, cps, chunk0, chunks_per_seq):
        self.pt_ref, self.ck_hbm, self.bsum_hbm = pt_ref, ck_hbm, bsum_hbm
        self.pgbuf, self.sumst, self.psem, self.osem = pgbuf, sumst, psem, osem
        self.cps, self.chunk0, self.chunks_per_seq = cps, chunk0, chunks_per_seq
        self.step = pl.program_id(0)
        self.n_local = pl.num_programs(0) * cps
        self.bps = sumst.shape[1]

    def _chunk_copies(self, c_local, slot):
        c = self.chunk0 + c_local
        n_pages = self.chunks_per_seq * PAGES_PER_CHUNK
        p0 = (c // self.chunks_per_seq) * n_pages + (c % self.chunks_per_seq) * PAGES_PER_CHUNK
        return [pltpu.make_async_copy(self.ck_hbm.at[self.pt_ref[p0 + r]], self.pgbuf.at[slot, r],
                                      self.psem.at[slot]) for r in range(PAGES_PER_CHUNK)]

    def _sum_copy(self, c_local, k):
        dst = self.bsum_hbm.at[c_local // self.chunks_per_seq,
                               pl.ds((c_local % self.chunks_per_seq) * self.bps, self.bps)]
        return pltpu.make_async_copy(self.sumst.at[k], dst, self.osem)

    def prologue(self):
        @pl.when(self.step == 0)
        def _():
            for cp in self._chunk_copies(0, 0):
                cp.start()

    def phase(self, k):
        c_local = self.step * self.cps + k
        slot = k % 2
        for cp in self._chunk_copies(jnp.minimum(c_local + 1, self.n_local - 1), 1 - slot):
            cp.start()
        for cp in self._chunk_copies(c_local, slot):
            cp.wait()
        ppb = PAGES_PER_CHUNK // self.bps
        for t in range(self.bps):
            tot = jnp.sum(self.pgbuf[slot, t * ppb], axis=0)
            for r in range(1, ppb):
                tot = tot + jnp.sum(self.pgbuf[slot, t * ppb + r], axis=0)
            self.sumst[k, t] = tot
        self._sum_copy(c_local, k).start()

    def finish(self):
        for k in range(self.cps):
            self._sum_copy(0, k).wait()

        @pl.when(self.step == pl.num_programs(0) - 1)
        def _():
            for cp in self._chunk_copies(self.n_local - 1, self.cps % 2):
                cp.wait()


def _scan_plan(page_table, cache_k, n_steps, part, n_parts):
    db, n_pages = page_table.shape
    page_size = cache_k.shape[1]
    ppb = MOBA_BLOCK // page_size
    assert MOBA_BLOCK % page_size == 0 and PAGES_PER_CHUNK % ppb == 0 and n_pages % PAGES_PER_CHUNK == 0
    bps = PAGES_PER_CHUNK // ppb
    chunks_per_seq = n_pages // PAGES_PER_CHUNK
    n_chunks = db * chunks_per_seq
    assert n_chunks % (n_parts * n_steps) == 0
    cps = n_chunks // (n_parts * n_steps)
    per_part = n_chunks // n_parts
    assert cps % 2 == 0 and per_part % chunks_per_seq == 0
    cfg = dict(cps=cps, chunk0=part * per_part, chunks_per_seq=chunks_per_seq)
    out_shape = jax.ShapeDtypeStruct((per_part // chunks_per_seq, n_pages // ppb, N_HEADS, HEAD_DIM), F32)
    scratch = [pltpu.VMEM((2, PAGES_PER_CHUNK, page_size, N_HEADS, HEAD_DIM), F32),
               pltpu.VMEM((cps, bps, N_HEADS, HEAD_DIM), F32),
               pltpu.SemaphoreType.DMA((2,)),
               pltpu.SemaphoreType.DMA(())]
    return cfg, out_shape, scratch


def _spread(n_items, n_slots):
    return [len(range(s, n_items, n_slots)) for s in range(n_slots)]


def _proj_kernel(pt_ref, x_ref, g_ref, w_ref, *rest, tm, scan):
    if scan:
        ck_hbm, rest = rest[0], rest[1:]
    xn_ref, q_ref, kc_ref, vc_ref, u_ref, ks_ref, k_hbm, v_hbm = rest[:8]
    rest = rest[8:]
    if scan:
        bsum_hbm, rest = rest[0], rest[1:]
    kst, vst, sem = rest[:3]
    rider = _ScanRider(pt_ref, ck_hbm, bsum_hbm, *rest[3:], **scan) if scan else None
    sub = iter(range(scan["cps"])) if scan else iter(())
    per_slot = _spread(scan["cps"], 4) if scan else [0] * 4

    def ride(slot_i):
        for _ in range(per_slot[slot_i]):
            rider.phase(next(sub))

    i = pl.program_id(0)
    aw = ATTN_WIDTH
    if scan:
        rider.prologue()

    def head_copies(src, dst_hbm, row_tile, s):
        r0 = pl.multiple_of(row_tile * tm, tm)
        return [pltpu.make_async_copy(src.at[:, h * HEAD_DIM:(h + 1) * HEAD_DIM],
                                      dst_hbm.at[pl.ds(r0, tm), h, :], sem.at[s]) for h in range(N_HEADS)]

    x = x_ref[...]
    xb = (x * lax.rsqrt(jnp.mean(x * x, axis=-1, keepdims=True) + EPS) * g_ref[...]).astype(BF16)
    xn_ref[...] = xb
    q_ref[...] = jnp.dot(xb, w_ref[:, 0:aw], preferred_element_type=F32)

    def emit(acc, stage, hbm, c_ref, s):
        @pl.when(i > 0)
        def _():
            for cp in head_copies(stage, hbm, i - 1, s):
                cp.wait()
        stage[...] = acc
        c_ref[...] = acc.astype(c_ref.dtype)
        for cp in head_copies(stage, hbm, i, s):
            cp.start()

    ride(0)
    k = jnp.dot(xb, w_ref[:, aw:2 * aw], preferred_element_type=F32)
    emit(k, kst, k_hbm, kc_ref, 0)
    ks_ref[0] = k.reshape(tm // MOBA_BLOCK, MOBA_BLOCK, aw).sum(axis=1)
    ride(1)
    emit(jnp.dot(xb, w_ref[:, 2 * aw:3 * aw], preferred_element_type=F32), vst, v_hbm, vc_ref, 1)
    ride(2)
    ua = jnp.dot(xb, w_ref[:, 3 * aw:3 * aw + CONV_CH], preferred_element_type=F32)
    ride(3)
    ug = jnp.dot(xb, w_ref[:, 3 * aw + CONV_CH:], preferred_element_type=F32)
    u_ref[...] = ua * _sigmoid(ug)
    if scan:
        rider.finish()

    @pl.when(i == pl.num_programs(0) - 1)
    def _():
        for cp in head_copies(kst, k_hbm, i, 0) + head_copies(vst, v_hbm, i, 1):
            cp.wait()


def _proj(x, g1, w_in, tm, kv_dtype, page_table, cache_k=None, scan_part=None):
    n = x.shape[0]
    nblk = tm // MOBA_BLOCK
    row = lambda i, pt: (i, 0)
    any_spec = pl.BlockSpec(memory_space=pl.ANY)
    scan, scan_out, scan_scratch = (None, None, [])
    if scan_part is not None:
        scan, scan_out, scan_scratch = _scan_plan(page_table, cache_k, n // tm, *scan_part)
    grid_spec = pltpu.PrefetchScalarGridSpec(
        num_scalar_prefetch=1,
        grid=(n // tm,),
        in_specs=[pl.BlockSpec((tm, D_MODEL), row),
                  _resident((1, D_MODEL)),
                  _resident(w_in.shape)] + ([any_spec] if scan else []),
        out_specs=[pl.BlockSpec((tm, D_MODEL), row),
                   pl.BlockSpec((tm, ATTN_WIDTH), row),
                   pl.BlockSpec((tm, ATTN_WIDTH), row),
                   pl.BlockSpec((tm, ATTN_WIDTH), row),
                   pl.BlockSpec((tm, CONV_CH), row),
                   pl.BlockSpec((1, nblk, ATTN_WIDTH), lambda i, pt: (i, 0, 0)),
                   any_spec, any_spec] + ([any_spec] if scan else []),
        scratch_shapes=[pltpu.VMEM((tm, ATTN_WIDTH), F32),
                        pltpu.VMEM((tm, ATTN_WIDTH), F32),
                        pltpu.SemaphoreType.DMA((2,))] + scan_scratch)
    return pl.pallas_call(
        functools.partial(_proj_kernel, tm=tm, scan=scan),
        grid_spec=grid_spec,
        out_shape=[jax.ShapeDtypeStruct((n, D_MODEL), BF16),
                   jax.ShapeDtypeStruct((n, ATTN_WIDTH), F32),
                   jax.ShapeDtypeStruct((n, ATTN_WIDTH), kv_dtype),
                   jax.ShapeDtypeStruct((n, ATTN_WIDTH), kv_dtype),
                   jax.ShapeDtypeStruct((n, CONV_CH), F32),
                   jax.ShapeDtypeStruct((n // tm, nblk, ATTN_WIDTH), F32),
                   jax.ShapeDtypeStruct((n, N_HEADS, HEAD_DIM), F32),
                   jax.ShapeDtypeStruct((n, N_HEADS, HEAD_DIM), F32)] + ([scan_out] if scan else []),
        compiler_params=_params(("arbitrary",)),
        name="proj",
    )(page_table.reshape(-1), x, g1, w_in, *([cache_k] if scan else []))


def _t5_bucket_py(n):
    max_exact = NUM_BUCKETS // 2
    if n < max_exact:
        return n
    return min(max_exact + int(math.log(n / max_exact) / math.log(MAX_DISTANCE / max_exact)
                               * (NUM_BUCKETS - max_exact)), NUM_BUCKETS - 1)


assert _t5_bucket_py(MOBA_BLOCK + 1) == NUM_BUCKETS - 1


def _far_parts(far, shape):
    hi, lo = _split_bf16(jnp.full(shape, far, F32))
    return hi.astype(F32), lo.astype(F32)


def _bias_kernel(rb_ref, o_ref):
    h = pl.program_id(0)
    shape = (MOBA_BLOCK, MOBA_BLOCK)
    qi = lax.broadcasted_iota(jnp.int32, shape, 0)
    kj = lax.broadcasted_iota(jnp.int32, shape, 1)
    max_exact = NUM_BUCKETS // 2
    far_hi, far_lo = _far_parts(rb_ref[NUM_BUCKETS - 1, h], shape)
    for c in range(2):
        dist = qi - kj + c * MOBA_BLOCK
        nn = jnp.maximum(dist, 0)
        nf = jnp.maximum(nn, 1).astype(F32)
        large = max_exact + (jnp.log(nf / max_exact) / math.log(MAX_DISTANCE / max_exact)
                             * (NUM_BUCKETS - max_exact)).astype(jnp.int32)
        large = jnp.minimum(large, NUM_BUCKETS - 1)
        bkt = jnp.where(nn < max_exact, nn, large)
        val = jnp.zeros(shape, F32)
        for t in range(NUM_BUCKETS):
            val = jnp.where(bkt == t, rb_ref[t, h], val)
        val = val - (far_hi + far_lo)
        if c == 0:
            val = jnp.where(dist >= 0, val, NEG)
        o_ref[0, c] = val


def _bias_tiles(rel_bias):
    return pl.pallas_call(
        _bias_kernel,
        grid=(N_HEADS,),
        in_specs=[pl.BlockSpec(memory_space=pltpu.SMEM)],
        out_specs=pl.BlockSpec((1, 2, MOBA_BLOCK, MOBA_BLOCK), lambda h: (h, 0, 0, 0)),
        out_shape=jax.ShapeDtypeStruct((N_HEADS, 2, MOBA_BLOCK, MOBA_BLOCK), F32),
        compiler_params=_params(("arbitrary",)),
        name="bias_tiles",
    )(rel_bias)


def _attn_kernel(far_ref, q_ref, k_ref, v_ref, km_ref, bias_ref, o_ref, ka, va, qa, s_scr, p_scr, *, nb):
    h = pl.program_id(1)
    blk = MOBA_BLOCK
    seq = nb * blk
    nbp = -(-nb // SUBLANES) * SUBLANES

    row_blk = lax.broadcasted_iota(jnp.int32, (seq, LANES), 0) // blk
    col = lax.broadcasted_iota(jnp.int32, (seq, LANES), 1)
    ka[:, 0:HEAD_DIM] = k_ref[...]
    ka[:, HEAD_DIM:] = jnp.where((col == row_blk) | (col == nbp) | (col == nbp + 1), 1.0, 0.0).astype(BF16)
    va[:, 0:HEAD_DIM] = v_ref[...]
    va[:, HEAD_DIM:] = jnp.ones((seq, LANES), BF16)

    far_hi, far_lo = _far_parts(far_ref[h], (SUBLANES, blk))
    r8 = lax.broadcasted_iota(jnp.int32, (SUBLANES, blk), 0)
    far_rows = jnp.where(r8 == 0, far_hi, jnp.where(r8 == 1, far_lo, 0.0))
    rowi = lax.broadcasted_iota(jnp.int32, (nbp, blk), 0)
    km = km_ref[0, 0, 0:nbp, :]

    for qi in range(nb):
        par = qi % 2
        n = (qi + 1) * blk
        qt = q_ref[qi * blk:n, :]

        if qi > MOBA_TOPK:
            st = lax.dot_general(km, qt, _NT, precision=lax.Precision.HIGHEST, preferred_element_type=F32)
            rank = jnp.zeros((nbp, blk), jnp.int32)
            for i in range(qi):
                si = st[i:i + 1, :]
                rank = rank + ((si > st) | ((si == st) & (rowi > i))).astype(jnp.int32)
            dropped = (rowi < qi) & (rank >= MOBA_TOPK)
            selv = jnp.where(dropped, NEG, 0.0)
        else:
            selv = jnp.zeros((nbp, blk), F32)
        aug = jnp.concatenate([selv, far_rows, jnp.zeros((LANES - nbp - SUBLANES, blk), F32)], axis=0)
        qa[par, :, HEAD_DIM:] = aug.T.astype(BF16)
        qa[par, :, 0:HEAD_DIM] = (qt * (HEAD_DIM ** -0.5)).astype(BF16)

        m_acc = jnp.full((blk, LANES), NEG, F32)
        for c0 in range(0, n, KEY_CHUNK):
            w = min(KEY_CHUNK, n - c0)
            sc = lax.dot_general(qa[par], ka[c0:c0 + w, :], _NT, preferred_element_type=F32)
            pieces = []
            for t0 in range(0, w, blk):
                piece = sc[:, t0:t0 + blk]
                kb = (c0 + t0) // blk
                if kb == qi:
                    piece = piece + bias_ref[0, 0]
                elif kb == qi - 1:
                    piece = piece + bias_ref[0, 1]
                pieces.append(piece)
                for l0 in range(0, blk, LANES):
                    m_acc = jnp.maximum(m_acc, piece[:, l0:l0 + LANES])
            s_scr[par, :, c0:c0 + w] = pieces[0] if len(pieces) == 1 else jnp.concatenate(pieces, axis=1)
        m = jnp.max(m_acc, axis=1, keepdims=True)
        for c0 in range(0, n, KEY_CHUNK):
            w = min(KEY_CHUNK, n - c0)
            p_scr[par, :, c0:c0 + w] = jnp.exp(s_scr[par, :, c0:c0 + w] - m).astype(BF16)
        out = jnp.dot(p_scr[par, :, 0:n], va[0:n, :], preferred_element_type=F32)
        o_ref[qi * blk:n, :] = (out[:, 0:HEAD_DIM] / out[:, HEAD_DIM:HEAD_DIM + 1]).astype(o_ref.dtype)


def _prompt_attention(q, kc, vc, kmeans, bias, far_bias, batch, seq):
    nb = seq // MOBA_BLOCK
    nbp = -(-nb // SUBLANES) * SUBLANES
    assert nbp + SUBLANES <= LANES and (MOBA_BLOCK % KEY_CHUNK == 0 or KEY_CHUNK % MOBA_BLOCK == 0)
    seq_blk = lambda b, h: (b, h)
    return pl.pallas_call(
        functools.partial(_attn_kernel, nb=nb),
        grid=(batch, N_HEADS),
        in_specs=[pl.BlockSpec(memory_space=pltpu.SMEM),
                  pl.BlockSpec((seq, HEAD_DIM), seq_blk),
                  pl.BlockSpec((seq, HEAD_DIM), seq_blk),
                  pl.BlockSpec((seq, HEAD_DIM), seq_blk),
                  pl.BlockSpec((1, 1, LANES, HEAD_DIM), lambda b, h: (b, h, 0, 0)),
                  pl.BlockSpec((1, 2, MOBA_BLOCK, MOBA_BLOCK), lambda b, h: (h, 0, 0, 0))],
        out_specs=pl.BlockSpec((seq, HEAD_DIM), seq_blk),
        out_shape=jax.ShapeDtypeStruct((batch * seq, ATTN_WIDTH), BF16),
        scratch_shapes=[pltpu.VMEM((seq, 2 * HEAD_DIM), BF16),
                        pltpu.VMEM((seq, 2 * HEAD_DIM), BF16),
                        pltpu.VMEM((2, MOBA_BLOCK, 2 * HEAD_DIM), BF16),
                        pltpu.VMEM((2, MOBA_BLOCK, seq), F32),
                        pltpu.VMEM((2, MOBA_BLOCK, seq), BF16)],
        compiler_params=_params(("parallel", "parallel")),
        name="prompt_attn",
    )(far_bias, q, kc, vc, kmeans, bias)


def _sselect_kernel(q_ref, km_ref, o_ref, *, own):
    s_new = q_ref.shape[0]
    col = lax.broadcasted_iota(jnp.int32, (s_new, LANES), 1)
    colf = col.astype(F32)
    for h in range(N_HEADS):
        qh = q_ref[:, h * HEAD_DIM:(h + 1) * HEAD_DIM]
        s = lax.dot_general(qh, km_ref[0, h], _NT, precision=lax.Precision.HIGHEST, preferred_element_type=F32)
        s = jnp.where(col < own, s, NEG)
        out = jnp.zeros((s_new, LANES), F32)
        for t in range(MOBA_TOPK):
            mx = jnp.max(s, axis=1, keepdims=True)
            arg = jnp.min(jnp.where(s == mx, colf, float(LANES)), axis=1, keepdims=True)
            out = jnp.where(col == t, arg, out)
            s = jnp.where(colf == arg, -jnp.inf, s)
        o_ref[0, h] = out.astype(jnp.int32)


def _sample_select(q, kmeans_s, db, s_new, own):
    assert MOBA_TOPK <= own < LANES
    return pl.pallas_call(
        functools.partial(_sselect_kernel, own=own),
        grid=(db,),
        in_specs=[pl.BlockSpec((s_new, ATTN_WIDTH), lambda b: (b, 0)),
                  pl.BlockSpec((1, N_HEADS, LANES, HEAD_DIM), lambda b: (b, 0, 0, 0))],
        out_specs=pl.BlockSpec((1, N_HEADS, s_new, LANES), lambda b: (b, 0, 0, 0)),
        out_shape=jax.ShapeDtypeStruct((db, N_HEADS, s_new, LANES), jnp.int32),
        compiler_params=_params(("parallel",)),
        name="sample_select",
    )(q, kmeans_s)


def _sattn_kernel(pt_ref, idx_ref, far_ref, q_ref, kn_ref, vn_ref, bias_ref, ck_hbm, cv_hbm, o_ref,
                  kbuf, vbuf, sem, *, n_pages, page_size, own, s_new):
    b = pl.program_id(0)
    h = pl.program_id(1)
    nh = pl.num_programs(1)
    step = b * nh + h
    nsteps = pl.num_programs(0) * nh
    slot = step % 2
    ppb = MOBA_BLOCK // page_size
    nsel = s_new * MOBA_TOPK
    nkeys = nsel * MOBA_BLOCK

    def copies(step_, slot_):
        b_ = step_ // nh
        h_ = step_ % nh
        out = []
        for c in range(nsel):
            blk = idx_ref[step_ * nsel + c]
            for r in range(ppb):
                phys = pt_ref[b_ * n_pages + blk * ppb + r]
                dst = pl.ds((c * ppb + r) * page_size, page_size)
                out.append(pltpu.make_async_copy(ck_hbm.at[phys, :, h_, :], kbuf.at[slot_, dst, :], sem.at[slot_, 0]))
                out.append(pltpu.make_async_copy(cv_hbm.at[phys, :, h_, :], vbuf.at[slot_, dst, :], sem.at[slot_, 1]))
        return out

    @pl.when(step == 0)
    def _():
        for cp in copies(step, slot):
            cp.start()

    @pl.when(step + 1 < nsteps)
    def _():
        for cp in copies(step + 1, 1 - slot):
            cp.start()

    for cp in copies(step, slot):
        cp.wait()

    far_hi, far_lo = _far_parts(far_ref[h], (s_new, LANES))
    far = (far_hi + far_lo)[:, 0:1]
    qb = (q_ref[...] * (HEAD_DIM ** -0.5)).astype(BF16)
    kb = kbuf[slot].astype(BF16)
    vb = vbuf[slot].astype(BF16)
    lg = lax.dot_general(qb, kb, _NT, preferred_element_type=F32) + far
    pieces = []
    for c in range(nsel):
        is_prev = idx_ref[step * nsel + c] == own - 1
        row = bias_ref[0, 1, (c // MOBA_TOPK):(c // MOBA_TOPK) + 1, :]
        pieces.append(jnp.where(is_prev, row, 0.0))
    lg = lg + jnp.concatenate(pieces, axis=1)
    col = lax.broadcasted_iota(jnp.int32, (s_new, nkeys), 1)
    row_i = lax.broadcasted_iota(jnp.int32, (s_new, nkeys), 0)
    mine = (col >= row_i * (MOBA_TOPK * MOBA_BLOCK)) & (col < (row_i + 1) * (MOBA_TOPK * MOBA_BLOCK))
    lg = jnp.where(mine, lg, NEG)
    lg_own = (lax.dot_general(qb, kn_ref[...].astype(BF16), _NT, preferred_element_type=F32)
              + far + bias_ref[0, 0, 0:s_new, 0:s_new])
    m = jnp.maximum(jnp.max(lg, axis=1, keepdims=True), jnp.max(lg_own, axis=1, keepdims=True))
    p = jnp.exp(lg - m)
    p_own = jnp.exp(lg_own - m)
    denom = jnp.sum(p, axis=1, keepdims=True) + jnp.sum(p_own, axis=1, keepdims=True)
    acc = (jnp.dot(p.astype(BF16), vb, preferred_element_type=F32)
           + jnp.dot(p_own.astype(BF16), vn_ref[...].astype(BF16), preferred_element_type=F32))
    o_ref[...] = acc / denom


def _sample_attention(q, k, v, cache_k, cache_v, page_table, sel_idx, bias, far_bias, s_new, own):
    db, n_pages = page_table.shape
    page_size = cache_k.shape[1]
    nkeys = s_new * MOBA_TOPK * MOBA_BLOCK
    new_spec = pl.BlockSpec((s_new, HEAD_DIM), lambda b, h, pt, ix: (b, h))
    grid_spec = pltpu.PrefetchScalarGridSpec(
        num_scalar_prefetch=2,
        grid=(db, N_HEADS),
        in_specs=[pl.BlockSpec(memory_space=pltpu.SMEM),
                  new_spec, new_spec, new_spec,
                  pl.BlockSpec((1, 2, MOBA_BLOCK, MOBA_BLOCK), lambda b, h, pt, ix: (h, 0, 0, 0)),
                  pl.BlockSpec(memory_space=pl.ANY),
                  pl.BlockSpec(memory_space=pl.ANY)],
        out_specs=new_spec,
        scratch_shapes=[pltpu.VMEM((2, nkeys, HEAD_DIM), F32),
                        pltpu.VMEM((2, nkeys, HEAD_DIM), F32),
                        pltpu.SemaphoreType.DMA((2, 2))])
    return pl.pallas_call(
        functools.partial(_sattn_kernel, n_pages=n_pages, page_size=page_size, own=own, s_new=s_new),
        grid_spec=grid_spec,
        out_shape=jax.ShapeDtypeStruct((db * s_new, ATTN_WIDTH), F32),
        compiler_params=_params(("arbitrary", "arbitrary")),
        name="sample_attn",
    )(page_table.reshape(-1), sel_idx.reshape(-1), far_bias, q, k, v, bias, cache_k, cache_v)


def _conv_kernel(halo_ref, cur_ref, cw_ref, cb_ref, lg_ref, lb_ref, o_ref, ext_scr, *, tt, rc, zero_first):
    i = pl.program_id(1)
    if zero_first:
        @pl.when(i == 0)
        def _():
            ext_scr[0:CONV_HALO, :] = jnp.zeros((CONV_HALO, CONV_CH), F32)

        @pl.when(i > 0)
        def _():
            ext_scr[0:CONV_HALO, :] = halo_ref[...]
    else:
        ext_scr[0:CONV_HALO, :] = halo_ref[...]
    ext_scr[CONV_HALO:CONV_HALO + tt, :] = cur_ref[...]
    lead = CONV_HALO - (CONV_WIDTH - 1)

    def chunk(c, carry):
        r0 = pl.multiple_of(c * rc, rc)
        win = ext_scr[pl.ds(r0, rc + CONV_HALO), :]
        acc = jnp.zeros((rc, CONV_CH), F32)
        for r in range(SUBLANES):
            rolled = win if r == 0 else pltpu.roll(win, rc + CONV_HALO - r, axis=0)
            for w in range(CONV_WIDTH):
                if (lead + w) % SUBLANES == r:
                    a0 = lead + w - r
                    acc = acc + rolled[a0:a0 + rc, :] * cw_ref[w:w + 1, :]
        hc = acc + cb_ref[...]
        mu = jnp.mean(hc, axis=-1, keepdims=True)
        var = jnp.mean(jnp.square(hc - mu), axis=-1, keepdims=True)
        y = (hc - mu) * lax.rsqrt(var + EPS) * lg_ref[...] + lb_ref[...]
        o_ref[pl.ds(r0, rc), :] = (y * _sigmoid(y)).astype(o_ref.dtype)
        return carry

    lax.fori_loop(0, tt // rc, chunk, 0)


def _conv_module(halo_src, halo_map, cur_src, conv_w, conv_b, ln_g, ln_b, n_seq, n_tiles, tt,
                 zero_first, out_dtype):
    rc = min(tt, 32)
    vec = pl.BlockSpec((1, CONV_CH), lambda b, i: (0, 0))
    cur_map = lambda b, i: (b * n_tiles + i, 0)
    return pl.pallas_call(
        functools.partial(_conv_kernel, tt=tt, rc=rc, zero_first=zero_first),
        grid=(n_seq, n_tiles),
        in_specs=[pl.BlockSpec((CONV_HALO, CONV_CH), halo_map),
                  pl.BlockSpec((tt, CONV_CH), cur_map),
                  pl.BlockSpec((CONV_WIDTH, CONV_CH), lambda b, i: (0, 0)),
                  vec, vec, vec],
        out_specs=pl.BlockSpec((tt, CONV_CH), cur_map),
        out_shape=jax.ShapeDtypeStruct((n_seq * n_tiles * tt, CONV_CH), out_dtype),
        scratch_shapes=[pltpu.VMEM((CONV_HALO + tt, CONV_CH), F32)],
        compiler_params=_params(("parallel", "arbitrary")),
        name="conv_module",
    )(halo_src, cur_src, conv_w, conv_b, ln_g, ln_b)


def _mix_kernel(pt_ref, xn_ref, at_ref, cv_ref, wg_ref, bg_ref, wa_ref, wc_ref, *rest, tn, scan):
    if scan:
        ck_hbm, o_ref, bsum_hbm = rest[:3]
        rider = _ScanRider(pt_ref, ck_hbm, bsum_hbm, *rest[3:], **scan)
        rider.prologue()
        per_slot = _spread(scan["cps"], D_MODEL // tn)
        sub = iter(range(scan["cps"]))
    else:
        o_ref = rest[0]
    xn = xn_ref[...]
    at = at_ref[...].astype(BF16)
    cv = cv_ref[...].astype(BF16)
    for c0 in range(0, D_MODEL, tn):
        if scan:
            for _ in range(per_slot[c0 // tn]):
                rider.phase(next(sub))
        a_cols = slice(c0, c0 + tn)
        c_cols = slice(D_MODEL + c0, D_MODEL + c0 + tn)
        ga = _sigmoid(jnp.dot(xn, wg_ref[:, a_cols], preferred_element_type=F32) + bg_ref[:, a_cols])
        gc = _sigmoid(jnp.dot(xn, wg_ref[:, c_cols], preferred_element_type=F32) + bg_ref[:, c_cols])
        ya = jnp.dot(at, wa_ref[:, a_cols], preferred_element_type=F32)
        yc = jnp.dot(cv, wc_ref[:, a_cols], preferred_element_type=F32)
        o_ref[:, a_cols] = (ga * ya + gc * yc).astype(o_ref.dtype)
    if scan:
        rider.finish()


def _gate_mix(xn, attn, cva, w_gate, b_gate, w_attn_out, w_conv_out, tm, page_table, cache_k=None,
              scan_part=None, tn=512):
    n = xn.shape[0]
    row = lambda i, pt: (i, 0)
    any_spec = pl.BlockSpec(memory_space=pl.ANY)
    scan, scan_out, scan_scratch = (None, None, [])
    if scan_part is not None:
        scan, scan_out, scan_scratch = _scan_plan(page_table, cache_k, n // tm, *scan_part)
    grid_spec = pltpu.PrefetchScalarGridSpec(
        num_scalar_prefetch=1,
        grid=(n // tm,),
        in_specs=[pl.BlockSpec((tm, D_MODEL), row),
                  pl.BlockSpec((tm, ATTN_WIDTH), row),
                  pl.BlockSpec((tm, CONV_CH), row),
                  _resident(w_gate.shape), _resident(b_gate.shape),
                  _resident(w_attn_out.shape), _resident(w_conv_out.shape)] + ([any_spec] if scan else []),
        out_specs=[pl.BlockSpec((tm, D_MODEL), row)] + ([any_spec] if scan else []),
        scratch_shapes=scan_scratch)
    return pl.pallas_call(
        functools.partial(_mix_kernel, tn=tn, scan=scan),
        grid_spec=grid_spec,
        out_shape=[jax.ShapeDtypeStruct((n, D_MODEL), BF16)] + ([scan_out] if scan else []),
        compiler_params=_params(("arbitrary",)),
        name="gate_mix",
    )(page_table.reshape(-1), xn, attn, cva, w_gate, b_gate, w_attn_out, w_conv_out, *([cache_k] if scan else []))


def _outproj_kernel(x_ref, mix_ref, wo_ref, g2_ref, wrh_ref, wrl_ref, br_ref, cin_ref,
                    x1_ref, xn2_ref, rt_ref, cout_ref, run):
    @pl.when(pl.program_id(0) == 0)
    def _():
        run[...] = cin_ref[...]

    x1 = x_ref[...] + jnp.dot(mix_ref[...], wo_ref[...], preferred_element_type=F32)
    x1_ref[...] = x1
    xn2 = x1 * lax.rsqrt(jnp.mean(x1 * x1, axis=-1, keepdims=True) + EPS) * g2_ref[...]
    xn2_ref[...] = xn2
    hi, lo = _split_bf16(xn2)
    wrh = wrh_ref[...]
    logits = (jnp.dot(hi, wrh, preferred_element_type=F32) + jnp.dot(lo, wrh, preferred_element_type=F32)
              + jnp.dot(hi, wrl_ref[...], preferred_element_type=F32) + br_ref[...])
    tm = logits.shape[0]
    col = lax.broadcasted_iota(jnp.int32, (tm, LANES), 1).astype(F32)
    big = float(LANES)

    def first_argmax(vals):
        mx = jnp.max(vals, axis=1, keepdims=True)
        return mx, jnp.min(jnp.where(vals == mx, col, big), axis=1, keepdims=True)

    gmask = col < N_GROUPS
    gmax, gidx = first_argmax(jnp.where(gmask, logits, -jnp.inf))
    p_g = 1.0 / jnp.sum(jnp.where(gmask, jnp.exp(logits - gmax), 0.0), axis=1, keepdims=True)
    e_lo = N_GROUPS + gidx * EXPERTS_PER_GROUP
    el = jnp.where((col >= e_lo) & (col < e_lo + EXPERTS_PER_GROUP), logits, -jnp.inf)
    v1, i1 = first_argmax(el)
    v2, i2 = first_argmax(jnp.where(col == i1, -jnp.inf, el))
    e21 = jnp.exp(v2 - v1)
    p1 = 1.0 / (1.0 + e21)
    p2 = e21 / (1.0 + e21)
    e1 = i1 - N_GROUPS
    e2 = i2 - N_GROUPS

    oh1 = (col == e1).astype(F32)
    oh2 = (col == e2).astype(F32)
    both = oh1 + oh2
    ri = lax.broadcasted_iota(jnp.int32, (tm, tm), 0)
    ci = lax.broadcasted_iota(jnp.int32, (tm, tm), 1)
    lower = (ci < ri).astype(BF16)
    before = jnp.dot(lower, both.astype(BF16), preferred_element_type=F32) + run[...]
    pos1 = jnp.sum(before * oh1, axis=1, keepdims=True)
    pos2 = jnp.sum(before * oh2, axis=1, keepdims=True)
    new_run = run[...] + jnp.sum(both, axis=0, keepdims=True)
    run[...] = new_run
    cout_ref[...] = new_run

    rt = jnp.zeros((tm, LANES), F32)
    for lane, val in enumerate((e1, e2, p_g * p1, p_g * p2, pos1, pos2)):
        rt = jnp.where(col == lane, val, rt)
    rt_ref[...] = rt


def _out_proj(x, mix, w_out, g2, wr_hi, wr_lo, br, cnt_in, tm):
    n = x.shape[0]
    row = lambda i: (i, 0)
    const = lambda i: (0, 0)
    return pl.pallas_call(
        _outproj_kernel,
        grid=(n // tm,),
        in_specs=[pl.BlockSpec((tm, D_MODEL), row),
                  pl.BlockSpec((tm, D_MODEL), row),
                  _resident((D_MODEL, D_MODEL)),
                  _resident((1, D_MODEL)),
                  _resident((D_MODEL, LANES)),
                  _resident((D_MODEL, LANES)),
                  _resident((1, LANES)),
                  _resident((1, LANES))],
        out_specs=[pl.BlockSpec((tm, D_MODEL), row),
                   pl.BlockSpec((tm, D_MODEL), row),
                   pl.BlockSpec((tm, LANES), row),
                   pl.BlockSpec((1, LANES), const)],
        out_shape=[jax.ShapeDtypeStruct((n, D_MODEL), F32),
                   jax.ShapeDtypeStruct((n, D_MODEL), F32),
                   jax.ShapeDtypeStruct((n, LANES), F32),
                   jax.ShapeDtypeStruct((1, LANES), F32)],
        scratch_shapes=[pltpu.VMEM((1, LANES), F32)],
        compiler_params=_params(("arbitrary",)),
        name="out_proj",
    )(x, mix, w_out, g2, wr_hi, wr_lo, br, cnt_in)


def _dispatch_kernel(dest_ref, x_ref, xb_in, xb_out, sem, *, tm):
    del xb_in
    i = pl.program_id(0)

    def row_copy(t, d):
        return pltpu.make_async_copy(x_ref.at[pl.ds(t, 1), :], xb_out.at[pl.ds(d, 1), :], sem)

    def issue(t, carry):
        base = (i * tm + t) * 2
        row_copy(t, dest_ref[base]).start()
        row_copy(t, dest_ref[base + 1]).start()
        return carry

    lax.fori_loop(0, tm, issue, 0, unroll=DMA_UNROLL)
    for _ in range(2):
        pltpu.make_async_copy(x_ref, xb_out.at[pl.ds(0, tm), :], sem).wait()


def _dispatch(xn2, dest, xb, tm):
    n = xn2.shape[0]
    grid_spec = pltpu.PrefetchScalarGridSpec(
        num_scalar_prefetch=1,
        grid=(n // tm,),
        in_specs=[pl.BlockSpec((tm, D_MODEL), lambda i, d: (i, 0)),
                  pl.BlockSpec(memory_space=pl.ANY)],
        out_specs=pl.BlockSpec(memory_space=pl.ANY),
        scratch_shapes=[pltpu.SemaphoreType.DMA(())])
    return pl.pallas_call(
        functools.partial(_dispatch_kernel, tm=tm),
        grid_spec=grid_spec,
        out_shape=jax.ShapeDtypeStruct(xb.shape, F32),
        input_output_aliases={2: 0},
        compiler_params=_params(("arbitrary",)),
        name="moe_dispatch",
    )(dest, xn2, xb)


def _moe_kernel(be_ref, nu_ref, x_ref, w1_ref, w3_ref, w2_ref, o_ref, w1b, w3b, w2b):
    g = pl.program_id(0)
    used = g < nu_ref[0]
    new_expert = (g == 0) | (be_ref[g] != be_ref[jnp.maximum(g - 1, 0)])

    @pl.when(used & new_expert)
    def _():
        w1b[...] = w1_ref[0].astype(BF16)
        w3b[...] = w3_ref[0].astype(BF16)
        w2b[...] = w2_ref[0].astype(BF16)

    @pl.when(used)
    def _():
        xb = x_ref[...].astype(BF16)
        a = jnp.dot(xb, w1b[...], preferred_element_type=F32)
        gt = jnp.dot(xb, w3b[...], preferred_element_type=F32)
        hmid = (a * _sigmoid(a) * gt).astype(BF16)
        o_ref[...] = jnp.dot(hmid, w2b[...], preferred_element_type=F32)

    @pl.when(jnp.logical_not(used))
    def _():
        o_ref[...] = jnp.zeros(o_ref.shape, F32)


def _moe_experts(xb, block_e, n_used, w1, w3, w2):
    n_blocks = xb.shape[0] // MOE_ROWS
    blk = lambda g, be, nu: (jnp.minimum(g, nu[0] - 1), 0)
    wsel = lambda g, be, nu: (be[jnp.minimum(g, nu[0] - 1)], 0, 0)
    grid_spec = pltpu.PrefetchScalarGridSpec(
        num_scalar_prefetch=2,
        grid=(n_blocks,),
        in_specs=[pl.BlockSpec((MOE_ROWS, D_MODEL), blk),
                  pl.BlockSpec((1, D_MODEL, D_EXPERT), wsel),
                  pl.BlockSpec((1, D_MODEL, D_EXPERT), wsel),
                  pl.BlockSpec((1, D_EXPERT, D_MODEL), wsel)],
        out_specs=pl.BlockSpec((MOE_ROWS, D_MODEL), lambda g, be, nu: (g, 0)),
        scratch_shapes=[pltpu.VMEM((D_MODEL, D_EXPERT), BF16),
                        pltpu.VMEM((D_MODEL, D_EXPERT), BF16),
                        pltpu.VMEM((D_EXPERT, D_MODEL), BF16)])
    return pl.pallas_call(
        _moe_kernel,
        grid_spec=grid_spec,
        out_shape=jax.ShapeDtypeStruct(xb.shape, F32),
        compiler_params=_params(("arbitrary",)),
        name="moe_experts",
    )(block_e, n_used, xb, w1, w3, w2)


def _combine_kernel(dest_ref, x1_ref, rt_ref, gf_ref, yb_hbm, o_ref, ybuf, sem, *, tm):
    i = pl.program_id(0)

    def row_copy(k, t, d):
        return pltpu.make_async_copy(yb_hbm.at[pl.ds(d, 1), :], ybuf.at[k, pl.ds(t, 1), :], sem)

    def issue(t, carry):
        base = (i * tm + t) * 2
        row_copy(0, t, dest_ref[base]).start()
        row_copy(1, t, dest_ref[base + 1]).start()
        return carry

    lax.fori_loop(0, tm, issue, 0, unroll=DMA_UNROLL)
    for k in range(2):
        pltpu.make_async_copy(yb_hbm.at[pl.ds(0, tm), :], ybuf.at[k], sem).wait()
    rt = rt_ref[...]
    moe = ybuf[0] * rt[:, 2:3] + ybuf[1] * rt[:, 3:4]
    xo = x1_ref[...] + moe
    o_ref[...] = xo * lax.rsqrt(jnp.mean(xo * xo, axis=-1, keepdims=True) + EPS) * gf_ref[...]


def _combine(x1, route, gf, yb, dest, tm):
    n = x1.shape[0]
    grid_spec = pltpu.PrefetchScalarGridSpec(
        num_scalar_prefetch=1,
        grid=(n // tm,),
        in_specs=[pl.BlockSpec((tm, D_MODEL), lambda i, d: (i, 0)),
                  pl.BlockSpec((tm, LANES), lambda i, d: (i, 0)),
                  pl.BlockSpec((1, D_MODEL), lambda i, d: (0, 0)),
                  pl.BlockSpec(memory_space=pl.ANY)],
        out_specs=pl.BlockSpec((tm, D_MODEL), lambda i, d: (i, 0)),
        scratch_shapes=[pltpu.VMEM((2, tm, D_MODEL), F32),
                        pltpu.SemaphoreType.DMA(())])
    return pl.pallas_call(
        functools.partial(_combine_kernel, tm=tm),
        grid_spec=grid_spec,
        out_shape=jax.ShapeDtypeStruct((n, D_MODEL), F32),
        compiler_params=_params(("arbitrary",)),
        name="moe_combine",
    )(dest, x1, route, gf, yb)


def _dispatch_plan(routes, counts):
    counts = counts[0, :N_EXPERTS].astype(jnp.int32)
    pcounts = (counts + MOE_ROWS - 1) // MOE_ROWS * MOE_ROWS
    pend = jnp.cumsum(pcounts)
    pstart = pend - pcounts
    dests = []
    n_assign = 0
    for rt in routes:
        e = rt[:, 0:2].astype(jnp.int32)
        pos = rt[:, 4:6].astype(jnp.int32)
        onehot = e[:, :, None] == jnp.arange(N_EXPERTS, dtype=jnp.int32)[None, None, :]
        dests.append((jnp.sum(jnp.where(onehot, pstart[None, None, :], 0), axis=-1) + pos).reshape(-1))
        n_assign += e.size
    n_blocks = (n_assign + N_EXPERTS * (MOE_ROWS - 1) + MOE_ROWS - 1) // MOE_ROWS
    first_row = jnp.arange(n_blocks, dtype=jnp.int32) * MOE_ROWS
    block_e = jnp.minimum(jnp.sum((pend[None, :] <= first_row[:, None]).astype(jnp.int32), axis=1), N_EXPERTS - 1)
    n_used = (pend[-1] // MOE_ROWS).astype(jnp.int32).reshape(1)
    return dests, block_e, n_used, n_blocks * MOE_ROWS


def kernel(x_prompt, x_sample, cache_k, cache_v, state_conv, page_table, rel_bias, norm1_g, w_in, w_attn_out,
           conv_w, conv_b, conv_ln_g, conv_ln_b, w_conv_out, w_gate, b_gate, w_out, norm2_g, router_g_w,
           router_g_b, router_e_w, router_e_b, moe_w1, moe_w3, moe_w2, norm_f_g):
    depth = norm1_g.shape[0]
    assert depth == 1
    batch, seq, _ = x_prompt.shape
    db, s_new, _ = x_sample.shape
    n_pages = page_table.shape[1]
    page_size = cache_k.shape[2]
    past_len = n_pages * page_size
    assert seq % MOBA_BLOCK == 0 and past_len % MOBA_BLOCK == 0 and s_new <= MOBA_BLOCK and s_new % SUBLANES == 0
    own_s = past_len // MOBA_BLOCK
    n_p = batch * seq
    n_s = db * s_new
    tm = ROW_TILE
    assert n_p % tm == 0 and n_s % tm == 0

    wi_b = w_in[0].astype(BF16)
    wr = jnp.concatenate([router_g_w[0], router_e_w[0]], axis=1)
    wr = jnp.pad(wr, ((0, 0), (0, LANES - wr.shape[1])))
    wr_hi, wr_lo = _split_bf16(wr)
    br = jnp.pad(jnp.concatenate([router_g_b[0], router_e_b[0]]), (0, LANES - N_GROUPS - N_EXPERTS))[None, :]
    far_bias = rel_bias[NUM_BUCKETS - 1, :]
    wg_b, wa_b, wc_b, wo_b = (w[0].astype(BF16) for w in (w_gate, w_attn_out, w_conv_out, w_out))
    cw = conv_w[0]
    bias = _bias_tiles(rel_bias)

    ck, cv = cache_k[0], cache_v[0]
    xn_p, q_p, kc_p, vc_p, u_p, ksum, k_p, v_p, bsum_a = _proj(
        x_prompt.reshape(n_p, D_MODEL), norm1_g, wi_b, tm, BF16, page_table, ck, (0, 2))
    xn_s, q_s, kc_s, vc_s, u_s, _, k_s, v_s = _proj(x_sample.reshape(n_s, D_MODEL), norm1_g, wi_b, tm, F32, page_table)

    nb = seq // MOBA_BLOCK
    km = ksum.reshape(batch, nb, N_HEADS, HEAD_DIM) / MOBA_BLOCK
    km = jnp.pad(km.transpose(0, 2, 1, 3), ((0, 0), (0, 0), (0, LANES - nb), (0, 0)))
    attn_p = _prompt_attention(q_p, kc_p, vc_p, km, bias, far_bias, batch, seq)

    tt = 512
    tiles = seq // tt
    per_halo = tt // CONV_HALO
    cva_p = _conv_module(u_p, lambda b, i: (jnp.maximum((b * tiles + i) * per_halo - 1, 0), 0), u_p,
                         cw, conv_b, conv_ln_g, conv_ln_b, batch, tiles, tt, True, BF16)
    lead = CONV_HALO - (CONV_WIDTH - 1)
    halo_s = jnp.pad(state_conv[0], ((0, 0), (lead, 0), (0, 0))).reshape(db * CONV_HALO, CONV_CH)
    cva_s = _conv_module(halo_s, lambda b, i: (b, 0), u_s,
                         cw, conv_b, conv_ln_g, conv_ln_b, db, 1, s_new, False, F32)

    mix_p, bsum_b = _gate_mix(xn_p, attn_p, cva_p, wg_b, b_gate, wa_b, wc_b, tm, page_table, ck, (1, 2))

    bsum = jnp.concatenate([bsum_a, bsum_b], axis=0)
    k_new = k_s.reshape(db, s_new, N_HEADS, HEAD_DIM)
    sums = jnp.concatenate([bsum, jnp.sum(k_new, axis=1)[:, None]], axis=1)
    km_s = jnp.pad((sums / MOBA_BLOCK).transpose(0, 2, 1, 3), ((0, 0), (0, 0), (0, LANES - own_s - 1), (0, 0)))
    sel_idx = _sample_select(q_s, km_s, db, s_new, own_s)[..., :MOBA_TOPK]
    attn_s = _sample_attention(q_s, kc_s, vc_s, ck, cv, page_table, sel_idx, bias, far_bias, s_new, own_s)

    mix_s, = _gate_mix(xn_s, attn_s, cva_s, wg_b, b_gate, wa_b, wc_b, tm, page_table)
    zero_cnt = jnp.zeros((1, LANES), F32)
    x1_p, xn2_p, rt_p, cnt_p = _out_proj(x_prompt.reshape(n_p, D_MODEL), mix_p, wo_b, norm2_g, wr_hi, wr_lo, br,
                                         zero_cnt, tm)
    x1_s, xn2_s, rt_s, cnt = _out_proj(x_sample.reshape(n_s, D_MODEL), mix_s, wo_b, norm2_g, wr_hi, wr_lo, br,
                                       cnt_p, tm)

    (dest_p, dest_s), block_e, n_used, n_rows = _dispatch_plan((rt_p, rt_s), cnt)
    xb = _dispatch(xn2_p, dest_p, jnp.zeros((n_rows, D_MODEL), F32), tm)
    xb = _dispatch(xn2_s, dest_s, xb, tm)
    yb = _moe_experts(xb, block_e, n_used, moe_w1[0], moe_w3[0], moe_w2[0])
    gf = norm_f_g[None, :]
    y_p = _combine(x1_p, rt_p, gf, yb, dest_p, tm)
    y_s = _combine(x1_s, rt_s, gf, yb, dest_s, tm)

    conv_prompt = u_p.reshape(batch, seq, CONV_CH)[:, seq - (CONV_WIDTH - 1):]
    conv_sample = jnp.concatenate([state_conv[0], u_s.reshape(db, s_new, CONV_CH)], axis=1)[:, -(CONV_WIDTH - 1):]
    shp_p = (1, batch, seq, N_HEADS, HEAD_DIM)
    shp_s = (1, db, s_new, N_HEADS, HEAD_DIM)
    return (y_p.reshape(batch, seq, D_MODEL), y_s.reshape(db, s_new, D_MODEL),
            k_p.reshape(shp_p), v_p.reshape(shp_p), conv_prompt[None],
            k_s.reshape(shp_s), v_s.reshape(shp_s), conv_sample[None])
```

```python
import functools
import math

import jax
import jax.numpy as jnp
from jax import lax
from jax.experimental import pallas as pl
from jax.experimental.pallas import tpu as pltpu

D_MODEL = 2048
HEAD_DIM = 128
N_HEADS = 8
ATTN_WIDTH = N_HEADS * HEAD_DIM
CONV_CH = 1024
CONV_WIDTH = 31
MOBA_BLOCK = 256
MOBA_TOPK = 3
NUM_BUCKETS = 32
MAX_DISTANCE = 128
N_GROUPS = 4
EXPERTS_PER_GROUP = 8
N_EXPERTS = N_GROUPS * EXPERTS_PER_GROUP
D_EXPERT = 512
EPS = 1e-6
NEG = -1e30

LANES = 128
SUBLANES = 8
CONV_HALO = 32
ROW_TILE = 256
PAGES_PER_CHUNK = 8
DMA_UNROLL = 8
MOE_ROWS = 256
KEY_CHUNK = 512
VMEM_LIMIT = 56 * 1024 * 1024

F32 = jnp.float32
BF16 = jnp.bfloat16
_NT = (((1,), (1,)), ((), ()))


def _params(sem, vmem=VMEM_LIMIT):
    return pltpu.CompilerParams(dimension_semantics=sem, vmem_limit_bytes=vmem)


def _sigmoid(x):
    return 1.0 / (1.0 + jnp.exp(-x))


def _split_bf16(x):
    hi = x.astype(BF16)
    return hi, (x - hi.astype(F32)).astype(BF16)


def _resident(shape):
    return pl.BlockSpec(shape, lambda *_: (0,) * len(shape), pipeline_mode=pl.Buffered(1))


class _ScanRider:
    def __init__(self, pt_ref, ck_hbm, bsum_hbm, pgbuf, sumst, psem, osem, *, cps, chunk0, chunks_per_seq,
                 step=None, n_steps=None):
        self.pt_ref, self.ck_hbm, self.bsum_hbm = pt_ref, ck_hbm, bsum_hbm
        self.pgbuf, self.sumst, self.psem, self.osem = pgbuf, sumst, psem, osem
        self.cps, self.chunk0, self.chunks_per_seq = cps, chunk0, chunks_per_seq
        self.step = pl.program_id(0) if step is None else step
        self.n_steps = pl.num_programs(0) if n_steps is None else n_steps
        self.n_local = self.n_steps * cps
        self.bps = sumst.shape[1]

    def _chunk_copies(self, c_local, slot):
        c = self.chunk0 + c_local
        n_pages = self.chunks_per_seq * PAGES_PER_CHUNK
        p0 = (c // self.chunks_per_seq) * n_pages + (c % self.chunks_per_seq) * PAGES_PER_CHUNK
        return [pltpu.make_async_copy(self.ck_hbm.at[self.pt_ref[p0 + r]], self.pgbuf.at[slot, r],
                                      self.psem.at[slot]) for r in range(PAGES_PER_CHUNK)]

    def _sum_copy(self, c_local, k):
        dst = self.bsum_hbm.at[c_local // self.chunks_per_seq,
                               pl.ds((c_local % self.chunks_per_seq) * self.bps, self.bps)]
        return pltpu.make_async_copy(self.sumst.at[k], dst, self.osem)

    def prologue(self):
        @pl.when(self.step == 0)
        def _():
            for cp in self._chunk_copies(0, 0):
                cp.start()

    def phase(self, k):
        c_local = self.step * self.cps + k
        slot = k % 2
        for cp in self._chunk_copies(jnp.minimum(c_local + 1, self.n_local - 1), 1 - slot):
            cp.start()
        for cp in self._chunk_copies(c_local, slot):
            cp.wait()
        ppb = PAGES_PER_CHUNK // self.bps
        for t in range(self.bps):
            tot = jnp.sum(self.pgbuf[slot, t * ppb], axis=0)
            for r in range(1, ppb):
                tot = tot + jnp.sum(self.pgbuf[slot, t * ppb + r], axis=0)
            self.sumst[k, t] = tot
        self._sum_copy(c_local, k).start()

    def finish(self):
        for k in range(self.cps):
            self._sum_copy(0, k).wait()

        @pl.when(self.step == self.n_steps - 1)
        def _():
            for cp in self._chunk_copies(self.n_local - 1, self.cps % 2):
                cp.wait()


def _scan_plan(page_table, cache_k, n_steps, part):
    db, n_pages = page_table.shape
    page_size = cache_k.shape[1]
    ppb = MOBA_BLOCK // page_size
    assert MOBA_BLOCK % page_size == 0 and PAGES_PER_CHUNK % ppb == 0 and n_pages % PAGES_PER_CHUNK == 0
    bps = PAGES_PER_CHUNK // ppb
    chunks_per_seq = n_pages // PAGES_PER_CHUNK
    n_chunks = db * chunks_per_seq
    assert n_chunks % 4 == 0
    chunk0, per_part = part[0] * (n_chunks // 4), part[1] * (n_chunks // 4)
    assert per_part % n_steps == 0 and chunk0 % chunks_per_seq == 0 and per_part % chunks_per_seq == 0
    cps = per_part // n_steps
    assert cps % 2 == 0
    cfg = dict(cps=cps, chunk0=chunk0, chunks_per_seq=chunks_per_seq)
    out_shape = jax.ShapeDtypeStruct((per_part // chunks_per_seq, n_pages // ppb, N_HEADS, HEAD_DIM), F32)
    scratch = [pltpu.VMEM((2, PAGES_PER_CHUNK, page_size, N_HEADS, HEAD_DIM), F32),
               pltpu.VMEM((cps, bps, N_HEADS, HEAD_DIM), F32),
               pltpu.SemaphoreType.DMA((2,)),
               pltpu.SemaphoreType.DMA(())]
    return cfg, out_shape, scratch


def _spread(n_items, n_slots):
    return [len(range(s, n_items, n_slots)) for s in range(n_slots)]


def _proj_kernel(pt_ref, x_ref, g_ref, w_ref, *rest, tm, scan):
    if scan:
        ck_hbm, rest = rest[0], rest[1:]
    xn_ref, q_ref, kc_ref, vc_ref, u_ref, ks_ref, k_hbm, v_hbm = rest[:8]
    rest = rest[8:]
    if scan:
        bsum_hbm, rest = rest[0], rest[1:]
    kst, vst, sem = rest[:3]
    rider = _ScanRider(pt_ref, ck_hbm, bsum_hbm, *rest[3:], **scan) if scan else None
    sub = iter(range(scan["cps"])) if scan else iter(())
    per_slot = _spread(scan["cps"], 4) if scan else [0] * 4

    def ride(slot_i):
        for _ in range(per_slot[slot_i]):
            rider.phase(next(sub))

    i = pl.program_id(0)
    aw = ATTN_WIDTH
    if scan:
        rider.prologue()

    def head_copies(src, dst_hbm, row_tile, s):
        r0 = pl.multiple_of(row_tile * tm, tm)
        return [pltpu.make_async_copy(src.at[:, h * HEAD_DIM:(h + 1) * HEAD_DIM],
                                      dst_hbm.at[pl.ds(r0, tm), h, :], sem.at[s]) for h in range(N_HEADS)]

    x = x_ref[...]
    xb = (x * lax.rsqrt(jnp.mean(x * x, axis=-1, keepdims=True) + EPS) * g_ref[...]).astype(BF16)
    xn_ref[...] = xb
    q_ref[...] = jnp.dot(xb, w_ref[:, 0:aw], preferred_element_type=F32)

    def emit(acc, stage, hbm, c_ref, s):
        @pl.when(i > 0)
        def _():
            for cp in head_copies(stage, hbm, i - 1, s):
                cp.wait()
        stage[...] = acc
        c_ref[...] = acc.astype(c_ref.dtype)
        for cp in head_copies(stage, hbm, i, s):
            cp.start()

    ride(0)
    k = jnp.dot(xb, w_ref[:, aw:2 * aw], preferred_element_type=F32)
    emit(k, kst, k_hbm, kc_ref, 0)
    ks_ref[0] = k.reshape(tm // MOBA_BLOCK, MOBA_BLOCK, aw).sum(axis=1)
    ride(1)
    emit(jnp.dot(xb, w_ref[:, 2 * aw:3 * aw], preferred_element_type=F32), vst, v_hbm, vc_ref, 1)
    ride(2)
    ua = jnp.dot(xb, w_ref[:, 3 * aw:3 * aw + CONV_CH], preferred_element_type=F32)
    ride(3)
    ug = jnp.dot(xb, w_ref[:, 3 * aw + CONV_CH:], preferred_element_type=F32)
    u_ref[...] = ua * _sigmoid(ug)
    if scan:
        rider.finish()

    @pl.when(i == pl.num_programs(0) - 1)
    def _():
        for cp in head_copies(kst, k_hbm, i, 0) + head_copies(vst, v_hbm, i, 1):
            cp.wait()


def _proj(x, g1, w_in, tm, kv_dtype, page_table, cache_k=None, scan_part=None):
    n = x.shape[0]
    nblk = tm // MOBA_BLOCK
    row = lambda i, pt: (i, 0)
    any_spec = pl.BlockSpec(memory_space=pl.ANY)
    scan, scan_out, scan_scratch = (None, None, [])
    if scan_part is not None:
        scan, scan_out, scan_scratch = _scan_plan(page_table, cache_k, n // tm, scan_part)
    grid_spec = pltpu.PrefetchScalarGridSpec(
        num_scalar_prefetch=1,
        grid=(n // tm,),
        in_specs=[pl.BlockSpec((tm, D_MODEL), row),
                  _resident((1, D_MODEL)),
                  _resident(w_in.shape)] + ([any_spec] if scan else []),
        out_specs=[pl.BlockSpec((tm, D_MODEL), row),
                   pl.BlockSpec((tm, ATTN_WIDTH), row),
                   pl.BlockSpec((tm, ATTN_WIDTH), row),
                   pl.BlockSpec((tm, ATTN_WIDTH), row),
                   pl.BlockSpec((tm, CONV_CH), row),
                   pl.BlockSpec((1, nblk, ATTN_WIDTH), lambda i, pt: (i, 0, 0)),
                   any_spec, any_spec] + ([any_spec] if scan else []),
        scratch_shapes=[pltpu.VMEM((tm, ATTN_WIDTH), F32),
                        pltpu.VMEM((tm, ATTN_WIDTH), F32),
                        pltpu.SemaphoreType.DMA((2,))] + scan_scratch)
    return pl.pallas_call(
        functools.partial(_proj_kernel, tm=tm, scan=scan),
        grid_spec=grid_spec,
        out_shape=[jax.ShapeDtypeStruct((n, D_MODEL), BF16),
                   jax.ShapeDtypeStruct((n, ATTN_WIDTH), F32),
                   jax.ShapeDtypeStruct((n, ATTN_WIDTH), kv_dtype),
                   jax.ShapeDtypeStruct((n, ATTN_WIDTH), kv_dtype),
                   jax.ShapeDtypeStruct((n, CONV_CH), F32),
                   jax.ShapeDtypeStruct((n // tm, nblk, ATTN_WIDTH), F32),
                   jax.ShapeDtypeStruct((n, N_HEADS, HEAD_DIM), F32),
                   jax.ShapeDtypeStruct((n, N_HEADS, HEAD_DIM), F32)] + ([scan_out] if scan else []),
        compiler_params=_params(("arbitrary",)),
        name="proj",
    )(page_table.reshape(-1), x, g1, w_in, *([cache_k] if scan else []))


def _t5_bucket_py(n):
    max_exact = NUM_BUCKETS // 2
    if n < max_exact:
        return n
    return min(max_exact + int(math.log(n / max_exact) / math.log(MAX_DISTANCE / max_exact)
                               * (NUM_BUCKETS - max_exact)), NUM_BUCKETS - 1)


assert _t5_bucket_py(MOBA_BLOCK + 1) == NUM_BUCKETS - 1


def _far_parts(far, shape):
    hi, lo = _split_bf16(jnp.full(shape, far, F32))
    return hi.astype(F32), lo.astype(F32)


def _bias_kernel(rb_ref, o_ref):
    h = pl.program_id(0)
    shape = (MOBA_BLOCK, MOBA_BLOCK)
    qi = lax.broadcasted_iota(jnp.int32, shape, 0)
    kj = lax.broadcasted_iota(jnp.int32, shape, 1)
    max_exact = NUM_BUCKETS // 2
    far_hi, far_lo = _far_parts(rb_ref[NUM_BUCKETS - 1, h], shape)
    for c in range(2):
        dist = qi - kj + c * MOBA_BLOCK
        nn = jnp.maximum(dist, 0)
        nf = jnp.maximum(nn, 1).astype(F32)
        large = max_exact + (jnp.log(nf / max_exact) / math.log(MAX_DISTANCE / max_exact)
                             * (NUM_BUCKETS - max_exact)).astype(jnp.int32)
        large = jnp.minimum(large, NUM_BUCKETS - 1)
        bkt = jnp.where(nn < max_exact, nn, large)
        val = jnp.zeros(shape, F32)
        for t in range(NUM_BUCKETS):
            val = jnp.where(bkt == t, rb_ref[t, h], val)
        val = val - (far_hi + far_lo)
        if c == 0:
            val = jnp.where(dist >= 0, val, NEG)
        o_ref[0, c] = val


def _bias_tiles(rel_bias):
    return pl.pallas_call(
        _bias_kernel,
        grid=(N_HEADS,),
        in_specs=[pl.BlockSpec(memory_space=pltpu.SMEM)],
        out_specs=pl.BlockSpec((1, 2, MOBA_BLOCK, MOBA_BLOCK), lambda h: (h, 0, 0, 0)),
        out_shape=jax.ShapeDtypeStruct((N_HEADS, 2, MOBA_BLOCK, MOBA_BLOCK), F32),
        compiler_params=_params(("arbitrary",)),
        name="bias_tiles",
    )(rel_bias)


def _attn_kernel(far_ref, q_ref, k_ref, v_ref, km_ref, bias_ref, o_ref, ka, va, qa, s_scr, p_scr, *, nb):
    h = pl.program_id(1)
    blk = MOBA_BLOCK
    seq = nb * blk
    nbp = -(-nb // SUBLANES) * SUBLANES

    row_blk = lax.broadcasted_iota(jnp.int32, (seq, LANES), 0) // blk
    col = lax.broadcasted_iota(jnp.int32, (seq, LANES), 1)
    ka[:, 0:HEAD_DIM] = k_ref[...]
    ka[:, HEAD_DIM:] = jnp.where((col == row_blk) | (col == nbp) | (col == nbp + 1), 1.0, 0.0).astype(BF16)
    va[:, 0:HEAD_DIM] = v_ref[...]
    va[:, HEAD_DIM:] = jnp.ones((seq, LANES), BF16)

    far_hi, far_lo = _far_parts(far_ref[h], (SUBLANES, blk))
    r8 = lax.broadcasted_iota(jnp.int32, (SUBLANES, blk), 0)
    far_rows = jnp.where(r8 == 0, far_hi, jnp.where(r8 == 1, far_lo, 0.0))
    rowi = lax.broadcasted_iota(jnp.int32, (nbp, blk), 0)
    km = km_ref[0, 0, 0:nbp, :]

    for qi in range(nb):
        par = qi % 2
        n = (qi + 1) * blk
        qt = q_ref[qi * blk:n, :]

        if qi > MOBA_TOPK:
            st = lax.dot_general(km, qt, _NT, precision=lax.Precision.HIGHEST, preferred_element_type=F32)
            rank = jnp.zeros((nbp, blk), jnp.int32)
            for i in range(qi):
                si = st[i:i + 1, :]
                rank = rank + ((si > st) | ((si == st) & (rowi > i))).astype(jnp.int32)
            dropped = (rowi < qi) & (rank >= MOBA_TOPK)
            selv = jnp.where(dropped, NEG, 0.0)
        else:
            selv = jnp.zeros((nbp, blk), F32)
        aug = jnp.concatenate([selv, far_rows, jnp.zeros((LANES - nbp - SUBLANES, blk), F32)], axis=0)
        qa[par, :, HEAD_DIM:] = aug.T.astype(BF16)
        qa[par, :, 0:HEAD_DIM] = (qt * (HEAD_DIM ** -0.5)).astype(BF16)

        m_acc = jnp.full((blk, LANES), NEG, F32)
        for c0 in range(0, n, KEY_CHUNK):
            w = min(KEY_CHUNK, n - c0)
            sc = lax.dot_general(qa[par], ka[c0:c0 + w, :], _NT, preferred_element_type=F32)
            pieces = []
            for t0 in range(0, w, blk):
                piece = sc[:, t0:t0 + blk]
                kb = (c0 + t0) // blk
                if kb == qi:
                    piece = piece + bias_ref[0, 0]
                elif kb == qi - 1:
                    piece = piece + bias_ref[0, 1]
                pieces.append(piece)
                for l0 in range(0, blk, LANES):
                    m_acc = jnp.maximum(m_acc, piece[:, l0:l0 + LANES])
            s_scr[par, :, c0:c0 + w] = pieces[0] if len(pieces) == 1 else jnp.concatenate(pieces, axis=1)
        m = jnp.max(m_acc, axis=1, keepdims=True)
        for c0 in range(0, n, KEY_CHUNK):
            w = min(KEY_CHUNK, n - c0)
            p_scr[par, :, c0:c0 + w] = jnp.exp(s_scr[par, :, c0:c0 + w] - m).astype(BF16)
        out = jnp.dot(p_scr[par, :, 0:n], va[0:n, :], preferred_element_type=F32)
        o_ref[qi * blk:n, :] = (out[:, 0:HEAD_DIM] / out[:, HEAD_DIM:HEAD_DIM + 1]).astype(o_ref.dtype)


def _prompt_attention(q, kc, vc, kmeans, bias, far_bias, batch, seq):
    nb = seq // MOBA_BLOCK
    nbp = -(-nb // SUBLANES) * SUBLANES
    assert nbp + SUBLANES <= LANES and (MOBA_BLOCK % KEY_CHUNK == 0 or KEY_CHUNK % MOBA_BLOCK == 0)
    seq_blk = lambda b, h: (b, h)
    return pl.pallas_call(
        functools.partial(_attn_kernel, nb=nb),
        grid=(batch, N_HEADS),
        in_specs=[pl.BlockSpec(memory_space=pltpu.SMEM),
                  pl.BlockSpec((seq, HEAD_DIM), seq_blk),
                  pl.BlockSpec((seq, HEAD_DIM), seq_blk),
                  pl.BlockSpec((seq, HEAD_DIM), seq_blk),
                  pl.BlockSpec((1, 1, LANES, HEAD_DIM), lambda b, h: (b, h, 0, 0)),
                  pl.BlockSpec((1, 2, MOBA_BLOCK, MOBA_BLOCK), lambda b, h: (h, 0, 0, 0))],
        out_specs=pl.BlockSpec((seq, HEAD_DIM), seq_blk),
        out_shape=jax.ShapeDtypeStruct((batch * seq, ATTN_WIDTH), BF16),
        scratch_shapes=[pltpu.VMEM((seq, 2 * HEAD_DIM), BF16),
                        pltpu.VMEM((seq, 2 * HEAD_DIM), BF16),
                        pltpu.VMEM((2, MOBA_BLOCK, 2 * HEAD_DIM), BF16),
                        pltpu.VMEM((2, MOBA_BLOCK, seq), F32),
                        pltpu.VMEM((2, MOBA_BLOCK, seq), BF16)],
        compiler_params=_params(("parallel", "parallel")),
        name="prompt_attn",
    )(far_bias, q, kc, vc, kmeans, bias)


def _sselect_kernel(q_ref, km_ref, o_ref, *, own):
    s_new = q_ref.shape[0]
    col = lax.broadcasted_iota(jnp.int32, (s_new, LANES), 1)
    colf = col.astype(F32)
    for h in range(N_HEADS):
        qh = q_ref[:, h * HEAD_DIM:(h + 1) * HEAD_DIM]
        s = lax.dot_general(qh, km_ref[0, h], _NT, precision=lax.Precision.HIGHEST, preferred_element_type=F32)
        s = jnp.where(col < own, s, NEG)
        out = jnp.zeros((s_new, LANES), F32)
        for t in range(MOBA_TOPK):
            mx = jnp.max(s, axis=1, keepdims=True)
            arg = jnp.min(jnp.where(s == mx, colf, float(LANES)), axis=1, keepdims=True)
            out = jnp.where(col == t, arg, out)
            s = jnp.where(colf == arg, -jnp.inf, s)
        o_ref[0, h] = out.astype(jnp.int32)


def _sample_select(q, kmeans_s, db, s_new, own):
    assert MOBA_TOPK <= own < LANES
    return pl.pallas_call(
        functools.partial(_sselect_kernel, own=own),
        grid=(db,),
        in_specs=[pl.BlockSpec((s_new, ATTN_WIDTH), lambda b: (b, 0)),
                  pl.BlockSpec((1, N_HEADS, LANES, HEAD_DIM), lambda b: (b, 0, 0, 0))],
        out_specs=pl.BlockSpec((1, N_HEADS, s_new, LANES), lambda b: (b, 0, 0, 0)),
        out_shape=jax.ShapeDtypeStruct((db, N_HEADS, s_new, LANES), jnp.int32),
        compiler_params=_params(("parallel",)),
        name="sample_select",
    )(q, kmeans_s)


def _sattn_kernel(pt_ref, idx_ref, far_ref, q_ref, kn_ref, vn_ref, bias_ref, ck_hbm, cv_hbm, o_ref,
                  kbuf, vbuf, sem, *, n_pages, page_size, own, s_new):
    b = pl.program_id(0)
    h = pl.program_id(1)
    nh = pl.num_programs(1)
    step = b * nh + h
    nsteps = pl.num_programs(0) * nh
    slot = step % 2
    ppb = MOBA_BLOCK // page_size
    nsel = s_new * MOBA_TOPK
    nkeys = nsel * MOBA_BLOCK

    def copies(step_, slot_):
        b_ = step_ // nh
        h_ = step_ % nh
        out = []
        for c in range(nsel):
            blk = idx_ref[step_ * nsel + c]
            for r in range(ppb):
                phys = pt_ref[b_ * n_pages + blk * ppb + r]
                dst = pl.ds((c * ppb + r) * page_size, page_size)
                out.append(pltpu.make_async_copy(ck_hbm.at[phys, :, h_, :], kbuf.at[slot_, dst, :], sem.at[slot_, 0]))
                out.append(pltpu.make_async_copy(cv_hbm.at[phys, :, h_, :], vbuf.at[slot_, dst, :], sem.at[slot_, 1]))
        return out

    @pl.when(step == 0)
    def _():
        for cp in copies(step, slot):
            cp.start()

    @pl.when(step + 1 < nsteps)
    def _():
        for cp in copies(step + 1, 1 - slot):
            cp.start()

    for cp in copies(step, slot):
        cp.wait()

    far_hi, far_lo = _far_parts(far_ref[h], (s_new, LANES))
    far = (far_hi + far_lo)[:, 0:1]
    qb = (q_ref[...] * (HEAD_DIM ** -0.5)).astype(BF16)
    kb = kbuf[slot].astype(BF16)
    vb = vbuf[slot].astype(BF16)
    lg = lax.dot_general(qb, kb, _NT, preferred_element_type=F32) + far
    pieces = []
    for c in range(nsel):
        is_prev = idx_ref[step * nsel + c] == own - 1
        row = bias_ref[0, 1, (c // MOBA_TOPK):(c // MOBA_TOPK) + 1, :]
        pieces.append(jnp.where(is_prev, row, 0.0))
    lg = lg + jnp.concatenate(pieces, axis=1)
    col = lax.broadcasted_iota(jnp.int32, (s_new, nkeys), 1)
    row_i = lax.broadcasted_iota(jnp.int32, (s_new, nkeys), 0)
    mine = (col >= row_i * (MOBA_TOPK * MOBA_BLOCK)) & (col < (row_i + 1) * (MOBA_TOPK * MOBA_BLOCK))
    lg = jnp.where(mine, lg, NEG)
    lg_own = (lax.dot_general(qb, kn_ref[...].astype(BF16), _NT, preferred_element_type=F32)
              + far + bias_ref[0, 0, 0:s_new, 0:s_new])
    m = jnp.maximum(jnp.max(lg, axis=1, keepdims=True), jnp.max(lg_own, axis=1, keepdims=True))
    p = jnp.exp(lg - m)
    p_own = jnp.exp(lg_own - m)
    denom = jnp.sum(p, axis=1, keepdims=True) + jnp.sum(p_own, axis=1, keepdims=True)
    acc = (jnp.dot(p.astype(BF16), vb, preferred_element_type=F32)
           + jnp.dot(p_own.astype(BF16), vn_ref[...].astype(BF16), preferred_element_type=F32))
    o_ref[...] = acc / denom


def _sample_attention(q, k, v, cache_k, cache_v, page_table, sel_idx, bias, far_bias, s_new, own):
    db, n_pages = page_table.shape
    page_size = cache_k.shape[1]
    nkeys = s_new * MOBA_TOPK * MOBA_BLOCK
    new_spec = pl.BlockSpec((s_new, HEAD_DIM), lambda b, h, pt, ix: (b, h))
    grid_spec = pltpu.PrefetchScalarGridSpec(
        num_scalar_prefetch=2,
        grid=(db, N_HEADS),
        in_specs=[pl.BlockSpec(memory_space=pltpu.SMEM),
                  new_spec, new_spec, new_spec,
                  pl.BlockSpec((1, 2, MOBA_BLOCK, MOBA_BLOCK), lambda b, h, pt, ix: (h, 0, 0, 0)),
                  pl.BlockSpec(memory_space=pl.ANY),
                  pl.BlockSpec(memory_space=pl.ANY)],
        out_specs=new_spec,
        scratch_shapes=[pltpu.VMEM((2, nkeys, HEAD_DIM), F32),
                        pltpu.VMEM((2, nkeys, HEAD_DIM), F32),
                        pltpu.SemaphoreType.DMA((2, 2))])
    return pl.pallas_call(
        functools.partial(_sattn_kernel, n_pages=n_pages, page_size=page_size, own=own, s_new=s_new),
        grid_spec=grid_spec,
        out_shape=jax.ShapeDtypeStruct((db * s_new, ATTN_WIDTH), F32),
        compiler_params=_params(("arbitrary", "arbitrary")),
        name="sample_attn",
    )(page_table.reshape(-1), sel_idx.reshape(-1), far_bias, q, k, v, bias, cache_k, cache_v)


def _conv_kernel(pt_ref, halo_ref, cur_ref, cw_ref, cb_ref, lg_ref, lb_ref, *rest, tt, rc, zero_first, scan):
    i = pl.program_id(1)
    if scan:
        ck_hbm, o_ref, bsum_hbm, ext_scr, sh = rest[:5]
        rider = _ScanRider(pt_ref, ck_hbm, bsum_hbm, *rest[5:], **scan,
                           step=pl.program_id(0) * pl.num_programs(1) + i,
                           n_steps=pl.num_programs(0) * pl.num_programs(1))
        rider.prologue()
    else:
        o_ref, ext_scr, sh = rest
    if zero_first:
        @pl.when(i == 0)
        def _():
            ext_scr[0:CONV_HALO, :] = jnp.zeros((CONV_HALO, CONV_CH), F32)

        @pl.when(i > 0)
        def _():
            ext_scr[0:CONV_HALO, :] = halo_ref[...]
    else:
        ext_scr[0:CONV_HALO, :] = halo_ref[...]
    ext_scr[CONV_HALO:CONV_HALO + tt, :] = cur_ref[...]
    lead = CONV_HALO - (CONV_WIDTH - 1)

    for r in range(SUBLANES):
        span = tt + CONV_HALO - r
        for s0 in range(0, span, LANES):
            rows = min(LANES, span - s0)
            sh[r, s0:s0 + rows, :] = ext_scr[r + s0:r + s0 + rows, :]

    def chunk(c, carry):
        r0 = pl.multiple_of(c * rc, rc)
        acc = jnp.zeros((rc, CONV_CH), F32)
        for w in range(CONV_WIDTH):
            r = (lead + w) % SUBLANES
            a0 = lead + w - r
            acc = acc + sh[r, pl.ds(pl.multiple_of(r0 + a0, SUBLANES), rc), :] * cw_ref[w:w + 1, :]
        hc = acc + cb_ref[...]
        mu = jnp.mean(hc, axis=-1, keepdims=True)
        var = jnp.mean(jnp.square(hc - mu), axis=-1, keepdims=True)
        y = (hc - mu) * lax.rsqrt(var + EPS) * lg_ref[...] + lb_ref[...]
        o_ref[pl.ds(r0, rc), :] = (y * _sigmoid(y)).astype(o_ref.dtype)
        return carry

    n_chunks = tt // rc
    if scan:
        cps = scan["cps"]
        assert n_chunks % cps == 0
        for k in range(cps):
            rider.phase(k)
            lax.fori_loop(k * (n_chunks // cps), (k + 1) * (n_chunks // cps), chunk, 0)
        rider.finish()
    else:
        lax.fori_loop(0, n_chunks, chunk, 0)


def _conv_module(halo_src, halo_map, cur_src, conv_w, conv_b, ln_g, ln_b, n_seq, n_tiles, tt,
                 zero_first, out_dtype, page_table, cache_k=None, scan_part=None):
    rc = min(tt, 32)
    vec = pl.BlockSpec((1, CONV_CH), lambda b, i, pt: (0, 0))
    cur_map = lambda b, i, pt: (b * n_tiles + i, 0)
    any_spec = pl.BlockSpec(memory_space=pl.ANY)
    scan, scan_out, scan_scratch = (None, None, [])
    if scan_part is not None:
        scan, scan_out, scan_scratch = _scan_plan(page_table, cache_k, n_seq * n_tiles, scan_part)
    grid_spec = pltpu.PrefetchScalarGridSpec(
        num_scalar_prefetch=1,
        grid=(n_seq, n_tiles),
        in_specs=[pl.BlockSpec((CONV_HALO, CONV_CH), halo_map),
                  pl.BlockSpec((tt, CONV_CH), cur_map),
                  pl.BlockSpec((CONV_WIDTH, CONV_CH), lambda b, i, pt: (0, 0)),
                  vec, vec, vec] + ([any_spec] if scan else []),
        out_specs=[pl.BlockSpec((tt, CONV_CH), cur_map)] + ([any_spec] if scan else []),
        scratch_shapes=[pltpu.VMEM((CONV_HALO + tt, CONV_CH), F32),
                        pltpu.VMEM((SUBLANES, tt + CONV_HALO, CONV_CH), F32)] + scan_scratch)
    return pl.pallas_call(
        functools.partial(_conv_kernel, tt=tt, rc=rc, zero_first=zero_first, scan=scan),
        grid_spec=grid_spec,
        out_shape=[jax.ShapeDtypeStruct((n_seq * n_tiles * tt, CONV_CH), out_dtype)] + ([scan_out] if scan else []),
        compiler_params=_params(("arbitrary", "arbitrary")),
        name="conv_module",
    )(page_table.reshape(-1), halo_src, cur_src, conv_w, conv_b, ln_g, ln_b, *([cache_k] if scan else []))


def _mix_kernel(pt_ref, xn_ref, at_ref, cv_ref, wg_ref, bg_ref, wa_ref, wc_ref, *rest, tn, scan):
    if scan:
        ck_hbm, o_ref, bsum_hbm = rest[:3]
        rider = _ScanRider(pt_ref, ck_hbm, bsum_hbm, *rest[3:], **scan)
        rider.prologue()
        per_slot = _spread(scan["cps"], D_MODEL // tn)
        sub = iter(range(scan["cps"]))
    else:
        o_ref = rest[0]
    xn = xn_ref[...]
    at = at_ref[...].astype(BF16)
    cv = cv_ref[...].astype(BF16)
    for c0 in range(0, D_MODEL, tn):
        if scan:
            for _ in range(per_slot[c0 // tn]):
                rider.phase(next(sub))
        a_cols = slice(c0, c0 + tn)
        c_cols = slice(D_MODEL + c0, D_MODEL + c0 + tn)
        ga = _sigmoid(jnp.dot(xn, wg_ref[:, a_cols], preferred_element_type=F32) + bg_ref[:, a_cols])
        gc = _sigmoid(jnp.dot(xn, wg_ref[:, c_cols], preferred_element_type=F32) + bg_ref[:, c_cols])
        ya = jnp.dot(at, wa_ref[:, a_cols], preferred_element_type=F32)
        yc = jnp.dot(cv, wc_ref[:, a_cols], preferred_element_type=F32)
        o_ref[:, a_cols] = (ga * ya + gc * yc).astype(o_ref.dtype)
    if scan:
        rider.finish()


def _gate_mix(xn, attn, cva, w_gate, b_gate, w_attn_out, w_conv_out, tm, page_table, cache_k=None,
              scan_part=None, tn=512):
    n = xn.shape[0]
    row = lambda i, pt: (i, 0)
    any_spec = pl.BlockSpec(memory_space=pl.ANY)
    scan, scan_out, scan_scratch = (None, None, [])
    if scan_part is not None:
        scan, scan_out, scan_scratch = _scan_plan(page_table, cache_k, n // tm, scan_part)
    grid_spec = pltpu.PrefetchScalarGridSpec(
        num_scalar_prefetch=1,
        grid=(n // tm,),
        in_specs=[pl.BlockSpec((tm, D_MODEL), row),
                  pl.BlockSpec((tm, ATTN_WIDTH), row),
                  pl.BlockSpec((tm, CONV_CH), row),
                  _resident(w_gate.shape), _resident(b_gate.shape),
                  _resident(w_attn_out.shape), _resident(w_conv_out.shape)] + ([any_spec] if scan else []),
        out_specs=[pl.BlockSpec((tm, D_MODEL), row)] + ([any_spec] if scan else []),
        scratch_shapes=scan_scratch)
    return pl.pallas_call(
        functools.partial(_mix_kernel, tn=tn, scan=scan),
        grid_spec=grid_spec,
        out_shape=[jax.ShapeDtypeStruct((n, D_MODEL), BF16)] + ([scan_out] if scan else []),
        compiler_params=_params(("arbitrary",)),
        name="gate_mix",
    )(page_table.reshape(-1), xn, attn, cva, w_gate, b_gate, w_attn_out, w_conv_out, *([cache_k] if scan else []))


def _outproj_kernel(pt_ref, x_ref, mix_ref, wo_ref, g2_ref, wrh_ref, wrl_ref, br_ref, cin_ref, *rest, scan):
    if scan:
        ck_hbm, x1_ref, xn2_ref, rt_ref, cout_ref, bsum_hbm, run = rest[:7]
        rider = _ScanRider(pt_ref, ck_hbm, bsum_hbm, *rest[7:], **scan)
        rider.prologue()
        first = -(-scan["cps"] // 2)
        for k in range(first):
            rider.phase(k)
    else:
        x1_ref, xn2_ref, rt_ref, cout_ref, run = rest

    @pl.when(pl.program_id(0) == 0)
    def _():
        run[...] = cin_ref[...]

    x1 = x_ref[...] + jnp.dot(mix_ref[...], wo_ref[...], preferred_element_type=F32)
    x1_ref[...] = x1
    xn2 = x1 * lax.rsqrt(jnp.mean(x1 * x1, axis=-1, keepdims=True) + EPS) * g2_ref[...]
    xn2_ref[...] = xn2
    if scan:
        for k in range(first, scan["cps"]):
            rider.phase(k)
    hi, lo = _split_bf16(xn2)
    wrh = wrh_ref[...]
    logits = (jnp.dot(hi, wrh, preferred_element_type=F32) + jnp.dot(lo, wrh, preferred_element_type=F32)
              + jnp.dot(hi, wrl_ref[...], preferred_element_type=F32) + br_ref[...])
    tm = logits.shape[0]
    col = lax.broadcasted_iota(jnp.int32, (tm, LANES), 1).astype(F32)
    big = float(LANES)

    def first_argmax(vals):
        mx = jnp.max(vals, axis=1, keepdims=True)
        return mx, jnp.min(jnp.where(vals == mx, col, big), axis=1, keepdims=True)

    gmask = col < N_GROUPS
    gmax, gidx = first_argmax(jnp.where(gmask, logits, -jnp.inf))
    p_g = 1.0 / jnp.sum(jnp.where(gmask, jnp.exp(logits - gmax), 0.0), axis=1, keepdims=True)
    e_lo = N_GROUPS + gidx * EXPERTS_PER_GROUP
    el = jnp.where((col >= e_lo) & (col < e_lo + EXPERTS_PER_GROUP), logits, -jnp.inf)
    v1, i1 = first_argmax(el)
    v2, i2 = first_argmax(jnp.where(col == i1, -jnp.inf, el))
    e21 = jnp.exp(v2 - v1)
    p1 = 1.0 / (1.0 + e21)
    p2 = e21 / (1.0 + e21)
    e1 = i1 - N_GROUPS
    e2 = i2 - N_GROUPS

    oh1 = (col == e1).astype(F32)
    oh2 = (col == e2).astype(F32)
    both = oh1 + oh2
    ri = lax.broadcasted_iota(jnp.int32, (tm, tm), 0)
    ci = lax.broadcasted_iota(jnp.int32, (tm, tm), 1)
    lower = (ci < ri).astype(BF16)
    before = jnp.dot(lower, both.astype(BF16), preferred_element_type=F32) + run[...]
    pos1 = jnp.sum(before * oh1, axis=1, keepdims=True)
    pos2 = jnp.sum(before * oh2, axis=1, keepdims=True)
    new_run = run[...] + jnp.sum(both, axis=0, keepdims=True)
    run[...] = new_run
    cout_ref[...] = new_run

    rt = jnp.zeros((tm, LANES), F32)
    for lane, val in enumerate((e1, e2, p_g * p1, p_g * p2, pos1, pos2)):
        rt = jnp.where(col == lane, val, rt)
    rt_ref[...] = rt
    if scan:
        rider.finish()


def _out_proj(x, mix, w_out, g2, wr_hi, wr_lo, br, cnt_in, tm, page_table, cache_k=None, scan_part=None):
    n = x.shape[0]
    row = lambda i, pt: (i, 0)
    const = lambda i, pt: (0, 0)
    any_spec = pl.BlockSpec(memory_space=pl.ANY)
    scan, scan_out, scan_scratch = (None, None, [])
    if scan_part is not None:
        scan, scan_out, scan_scratch = _scan_plan(page_table, cache_k, n // tm, scan_part)
    grid_spec = pltpu.PrefetchScalarGridSpec(
        num_scalar_prefetch=1,
        grid=(n // tm,),
        in_specs=[pl.BlockSpec((tm, D_MODEL), row),
                  pl.BlockSpec((tm, D_MODEL), row),
                  _resident((D_MODEL, D_MODEL)),
                  _resident((1, D_MODEL)),
                  _resident((D_MODEL, LANES)),
                  _resident((D_MODEL, LANES)),
                  _resident((1, LANES)),
                  _resident((1, LANES))] + ([any_spec] if scan else []),
        out_specs=[pl.BlockSpec((tm, D_MODEL), row),
                   pl.BlockSpec((tm, D_MODEL), row),
                   pl.BlockSpec((tm, LANES), row),
                   pl.BlockSpec((1, LANES), const)] + ([any_spec] if scan else []),
        scratch_shapes=[pltpu.VMEM((1, LANES), F32)] + scan_scratch)
    return pl.pallas_call(
        functools.partial(_outproj_kernel, scan=scan),
        grid_spec=grid_spec,
        out_shape=[jax.ShapeDtypeStruct((n, D_MODEL), F32),
                   jax.ShapeDtypeStruct((n, D_MODEL), F32),
                   jax.ShapeDtypeStruct((n, LANES), F32),
                   jax.ShapeDtypeStruct((1, LANES), F32)] + ([scan_out] if scan else []),
        compiler_params=_params(("arbitrary",)),
        name="out_proj",
    )(page_table.reshape(-1), x, mix, w_out, g2, wr_hi, wr_lo, br, cnt_in, *([cache_k] if scan else []))


def _dispatch_kernel(dest_ref, pad0_ref, padn_ref, tail_ref, xp_ref, xs_ref, xb_out, zbuf, sem, zsem, *,
                     tm, np_tiles, n_rows):
    i = pl.program_id(0)

    def issue_rows(x_ref):
        def row_copy(t, d):
            return pltpu.make_async_copy(x_ref.at[pl.ds(t, 1), :], xb_out.at[pl.ds(d, 1), :], sem)

        def issue(t, carry):
            base = (i * tm + t) * 2
            row_copy(t, dest_ref[base]).start()
            row_copy(t, dest_ref[base + 1]).start()
            return carry

        lax.fori_loop(0, tm, issue, 0, unroll=DMA_UNROLL)

    @pl.when(i < np_tiles)
    def _():
        issue_rows(xp_ref)

    @pl.when(i >= np_tiles)
    def _():
        issue_rows(xs_ref)

    @pl.when(i == 0)
    def _():
        zbuf[...] = jnp.zeros(zbuf.shape, F32)

        def zero_row(d):
            return pltpu.make_async_copy(zbuf.at[pl.ds(0, 1), :], xb_out.at[pl.ds(d, 1), :], zsem)

        def zero_blk(d):
            return pltpu.make_async_copy(zbuf, xb_out.at[pl.ds(d, MOE_ROWS), :], zsem)

        def each(n, fn):
            def body(r, carry):
                fn(r)
                return carry
            lax.fori_loop(0, n, body, 0)

        n_tail = (n_rows - tail_ref[0]) // MOE_ROWS
        for e in range(N_EXPERTS):
            each(padn_ref[e], lambda r, e=e: zero_row(pad0_ref[e] + r).start())
        each(n_tail, lambda j: zero_blk(pl.multiple_of(tail_ref[0] + j * MOE_ROWS, MOE_ROWS)).start())
        for e in range(N_EXPERTS):
            each(padn_ref[e], lambda r: zero_row(0).wait())
        each(n_tail, lambda j: zero_blk(0).wait())

    for _ in range(2):
        pltpu.make_async_copy(xp_ref, xb_out.at[pl.ds(0, tm), :], sem).wait()


def _dispatch(xn2_p, xn2_s, dest, pad0, padn, tail, n_rows, tm):
    np_tiles = xn2_p.shape[0] // tm
    ns_tiles = xn2_s.shape[0] // tm
    grid_spec = pltpu.PrefetchScalarGridSpec(
        num_scalar_prefetch=4,
        grid=(np_tiles + ns_tiles,),
        in_specs=[pl.BlockSpec((tm, D_MODEL), lambda i, *_: (jnp.minimum(i, np_tiles - 1), 0)),
                  pl.BlockSpec((tm, D_MODEL), lambda i, *_: (jnp.maximum(i - np_tiles, 0), 0))],
        out_specs=pl.BlockSpec(memory_space=pl.ANY),
        scratch_shapes=[pltpu.VMEM((MOE_ROWS, D_MODEL), F32),
                        pltpu.SemaphoreType.DMA(()),
                        pltpu.SemaphoreType.DMA(())])
    return pl.pallas_call(
        functools.partial(_dispatch_kernel, tm=tm, np_tiles=np_tiles, n_rows=n_rows),
        grid_spec=grid_spec,
        out_shape=jax.ShapeDtypeStruct((n_rows, D_MODEL), F32),
        compiler_params=_params(("arbitrary",)),
        name="moe_dispatch",
    )(dest, pad0, padn, tail, xn2_p, xn2_s)


def _moe_kernel(be_ref, nu_ref, x_ref, w1_ref, w3_ref, w2_ref, o_ref, w1b, w3b, w2b):
    g = pl.program_id(0)
    used = g < nu_ref[0]
    new_expert = (g == 0) | (be_ref[g] != be_ref[jnp.maximum(g - 1, 0)])

    @pl.when(used & new_expert)
    def _():
        w1b[...] = w1_ref[0].astype(BF16)
        w3b[...] = w3_ref[0].astype(BF16)
        w2b[...] = w2_ref[0].astype(BF16)

    @pl.when(used)
    def _():
        xb = x_ref[...].astype(BF16)
        a = jnp.dot(xb, w1b[...], preferred_element_type=F32)
        gt = jnp.dot(xb, w3b[...], preferred_element_type=F32)
        hmid = (a * _sigmoid(a) * gt).astype(BF16)
        o_ref[...] = jnp.dot(hmid, w2b[...], preferred_element_type=F32)

    @pl.when(jnp.logical_not(used))
    def _():
        o_ref[...] = jnp.zeros(o_ref.shape, F32)


def _moe_experts(xb, block_e, n_used, w1, w3, w2):
    n_blocks = xb.shape[0] // MOE_ROWS
    blk = lambda g, be, nu: (jnp.minimum(g, nu[0] - 1), 0)
    wsel = lambda g, be, nu: (be[jnp.minimum(g, nu[0] - 1)], 0, 0)
    grid_spec = pltpu.PrefetchScalarGridSpec(
        num_scalar_prefetch=2,
        grid=(n_blocks,),
        in_specs=[pl.BlockSpec((MOE_ROWS, D_MODEL), blk),
                  pl.BlockSpec((1, D_MODEL, D_EXPERT), wsel),
                  pl.BlockSpec((1, D_MODEL, D_EXPERT), wsel),
                  pl.BlockSpec((1, D_EXPERT, D_MODEL), wsel)],
        out_specs=pl.BlockSpec((MOE_ROWS, D_MODEL), lambda g, be, nu: (g, 0)),
        scratch_shapes=[pltpu.VMEM((D_MODEL, D_EXPERT), BF16),
                        pltpu.VMEM((D_MODEL, D_EXPERT), BF16),
                        pltpu.VMEM((D_EXPERT, D_MODEL), BF16)])
    return pl.pallas_call(
        _moe_kernel,
        grid_spec=grid_spec,
        out_shape=jax.ShapeDtypeStruct(xb.shape, F32),
        compiler_params=_params(("arbitrary",)),
        name="moe_experts",
    )(block_e, n_used, xb, w1, w3, w2)


def _combine_kernel(dest_ref, x1_ref, rt_ref, gf_ref, yb_hbm, o_ref, ybuf, sem, *, tm):
    i = pl.program_id(0)

    def row_copy(k, t, d):
        return pltpu.make_async_copy(yb_hbm.at[pl.ds(d, 1), :], ybuf.at[k, pl.ds(t, 1), :], sem)

    def issue(t, carry):
        base = (i * tm + t) * 2
        row_copy(0, t, dest_ref[base]).start()
        row_copy(1, t, dest_ref[base + 1]).start()
        return carry

    lax.fori_loop(0, tm, issue, 0, unroll=DMA_UNROLL)
    for k in range(2):
        pltpu.make_async_copy(yb_hbm.at[pl.ds(0, tm), :], ybuf.at[k], sem).wait()
    rt = rt_ref[...]
    moe = ybuf[0] * rt[:, 2:3] + ybuf[1] * rt[:, 3:4]
    xo = x1_ref[...] + moe
    o_ref[...] = xo * lax.rsqrt(jnp.mean(xo * xo, axis=-1, keepdims=True) + EPS) * gf_ref[...]


def _combine(x1, route, gf, yb, dest, tm):
    n = x1.shape[0]
    grid_spec = pltpu.PrefetchScalarGridSpec(
        num_scalar_prefetch=1,
        grid=(n // tm,),
        in_specs=[pl.BlockSpec((tm, D_MODEL), lambda i, d: (i, 0)),
                  pl.BlockSpec((tm, LANES), lambda i, d: (i, 0)),
                  pl.BlockSpec((1, D_MODEL), lambda i, d: (0, 0)),
                  pl.BlockSpec(memory_space=pl.ANY)],
        out_specs=pl.BlockSpec((tm, D_MODEL), lambda i, d: (i, 0)),
        scratch_shapes=[pltpu.VMEM((2, tm, D_MODEL), F32),
                        pltpu.SemaphoreType.DMA(())])
    return pl.pallas_call(
        functools.partial(_combine_kernel, tm=tm),
        grid_spec=grid_spec,
        out_shape=jax.ShapeDtypeStruct((n, D_MODEL), F32),
        compiler_params=_params(("arbitrary",)),
        name="moe_combine",
    )(dest, x1, route, gf, yb)


def _dispatch_plan(routes, counts):
    counts = counts[0, :N_EXPERTS].astype(jnp.int32)
    pcounts = (counts + MOE_ROWS - 1) // MOE_ROWS * MOE_ROWS
    pend = jnp.cumsum(pcounts)
    pstart = pend - pcounts
    dests = []
    n_assign = 0
    for rt in routes:
        e = rt[:, 0:2].astype(jnp.int32)
        pos = rt[:, 4:6].astype(jnp.int32)
        onehot = e[:, :, None] == jnp.arange(N_EXPERTS, dtype=jnp.int32)[None, None, :]
        dests.append((jnp.sum(jnp.where(onehot, pstart[None, None, :], 0), axis=-1) + pos).reshape(-1))
        n_assign += e.size
    n_blocks = (n_assign + N_EXPERTS * (MOE_ROWS - 1) + MOE_ROWS - 1) // MOE_ROWS
    first_row = jnp.arange(n_blocks, dtype=jnp.int32) * MOE_ROWS
    block_e = jnp.minimum(jnp.sum((pend[None, :] <= first_row[:, None]).astype(jnp.int32), axis=1), N_EXPERTS - 1)
    n_used = (pend[-1] // MOE_ROWS).astype(jnp.int32).reshape(1)
    pads = (pstart + counts, pcounts - counts, pend[-1:])
    return dests, block_e, n_used, n_blocks * MOE_ROWS, pads


def kernel(x_prompt, x_sample, cache_k, cache_v, state_conv, page_table, rel_bias, norm1_g, w_in, w_attn_out,
           conv_w, conv_b, conv_ln_g, conv_ln_b, w_conv_out, w_gate, b_gate, w_out, norm2_g, router_g_w,
           router_g_b, router_e_w, router_e_b, moe_w1, moe_w3, moe_w2, norm_f_g):
    depth = norm1_g.shape[0]
    assert depth == 1
    batch, seq, _ = x_prompt.shape
    db, s_new, _ = x_sample.shape
    n_pages = page_table.shape[1]
    page_size = cache_k.shape[2]
    past_len = n_pages * page_size
    assert seq % MOBA_BLOCK == 0 and past_len % MOBA_BLOCK == 0 and s_new <= MOBA_BLOCK and s_new % SUBLANES == 0
    own_s = past_len // MOBA_BLOCK
    n_p = batch * seq
    n_s = db * s_new
    tm = ROW_TILE
    assert n_p % tm == 0 and n_s % tm == 0

    wi_b = w_in[0].astype(BF16)
    wr = jnp.concatenate([router_g_w[0], router_e_w[0]], axis=1)
    wr = jnp.pad(wr, ((0, 0), (0, LANES - wr.shape[1])))
    wr_hi, wr_lo = _split_bf16(wr)
    br = jnp.pad(jnp.concatenate([router_g_b[0], router_e_b[0]]), (0, LANES - N_GROUPS - N_EXPERTS))[None, :]
    far_bias = rel_bias[NUM_BUCKETS - 1, :]
    wg_b, wa_b, wc_b, wo_b = (w[0].astype(BF16) for w in (w_gate, w_attn_out, w_conv_out, w_out))
    cw = conv_w[0]
    bias = _bias_tiles(rel_bias)

    ck, cv = cache_k[0], cache_v[0]
    xn_p, q_p, kc_p, vc_p, u_p, ksum, k_p, v_p, bsum_a = _proj(
        x_prompt.reshape(n_p, D_MODEL), norm1_g, wi_b, tm, BF16, page_table, ck, (0, 1))
    xn_s, q_s, kc_s, vc_s, u_s, _, k_s, v_s = _proj(x_sample.reshape(n_s, D_MODEL), norm1_g, wi_b, tm, F32, page_table)

    nb = seq // MOBA_BLOCK
    km = ksum.reshape(batch, nb, N_HEADS, HEAD_DIM) / MOBA_BLOCK
    km = jnp.pad(km.transpose(0, 2, 1, 3), ((0, 0), (0, 0), (0, LANES - nb), (0, 0)))
    attn_p = _prompt_attention(q_p, kc_p, vc_p, km, bias, far_bias, batch, seq)

    tt = 512
    tiles = seq // tt
    per_halo = tt // CONV_HALO
    cva_p, bsum_b = _conv_module(u_p, lambda b, i, pt: (jnp.maximum((b * tiles + i) * per_halo - 1, 0), 0), u_p,
                                 cw, conv_b, conv_ln_g, conv_ln_b, batch, tiles, tt, True, BF16, page_table, ck, (1, 1))
    lead = CONV_HALO - (CONV_WIDTH - 1)
    halo_s = jnp.pad(state_conv[0], ((0, 0), (lead, 0), (0, 0))).reshape(db * CONV_HALO, CONV_CH)
    cva_s, = _conv_module(halo_s, lambda b, i, pt: (b, 0), u_s,
                          cw, conv_b, conv_ln_g, conv_ln_b, db, 1, s_new, False, F32, page_table)

    mix_p, bsum_c = _gate_mix(xn_p, attn_p, cva_p, wg_b, b_gate, wa_b, wc_b, tm, page_table, ck, (2, 1))
    zero_cnt = jnp.zeros((1, LANES), F32)
    x1_p, xn2_p, rt_p, cnt_p, bsum_d = _out_proj(x_prompt.reshape(n_p, D_MODEL), mix_p, wo_b, norm2_g, wr_hi, wr_lo,
                                                 br, zero_cnt, tm, page_table, ck, (3, 1))

    bsum = jnp.concatenate([bsum_a, bsum_b, bsum_c, bsum_d], axis=0)
    k_new = k_s.reshape(db, s_new, N_HEADS, HEAD_DIM)
    sums = jnp.concatenate([bsum, jnp.sum(k_new, axis=1)[:, None]], axis=1)
    km_s = jnp.pad((sums / MOBA_BLOCK).transpose(0, 2, 1, 3), ((0, 0), (0, 0), (0, LANES - own_s - 1), (0, 0)))
    sel_idx = _sample_select(q_s, km_s, db, s_new, own_s)[..., :MOBA_TOPK]
    attn_s = _sample_attention(q_s, kc_s, vc_s, ck, cv, page_table, sel_idx, bias, far_bias, s_new, own_s)

    mix_s, = _gate_mix(xn_s, attn_s, cva_s, wg_b, b_gate, wa_b, wc_b, tm, page_table)
    x1_s, xn2_s, rt_s, cnt = _out_proj(x_sample.reshape(n_s, D_MODEL), mix_s, wo_b, norm2_g, wr_hi, wr_lo, br,
                                       cnt_p, tm, page_table)

    (dest_p, dest_s), block_e, n_used, n_rows, pads = _dispatch_plan((rt_p, rt_s), cnt)
    xb = _dispatch(xn2_p, xn2_s, jnp.concatenate([dest_p, dest_s]), *pads, n_rows, tm)
    yb = _moe_experts(xb, block_e, n_used, moe_w1[0], moe_w3[0], moe_w2[0])
    gf = norm_f_g[None, :]
    y_p = _combine(x1_p, rt_p, gf, yb, dest_p, tm)
    y_s = _combine(x1_s, rt_s, gf, yb, dest_s, tm)

    conv_prompt = u_p.reshape(batch, seq, CONV_CH)[:, seq - (CONV_WIDTH - 1):]
    conv_sample = jnp.concatenate([state_conv[0], u_s.reshape(db, s_new, CONV_CH)], axis=1)[:, -(CONV_WIDTH - 1):]
    shp_p = (1, batch, seq, N_HEADS, HEAD_DIM)
    shp_s = (1, db, s_new, N_HEADS, HEAD_DIM)
    return (y_p.reshape(batch, seq, D_MODEL), y_s.reshape(db, s_new, D_MODEL),
            k_p.reshape(shp_p), v_p.reshape(shp_p), conv_prompt[None],
            k_s.reshape(shp_s), v_s.reshape(shp_s), conv_sample[None])
```

```python
import functools
import math

import jax
import jax.numpy as jnp
from jax import lax
from jax.experimental import pallas as pl
from jax.experimental.pallas import tpu as pltpu

D_MODEL = 2048
HEAD_DIM = 128
N_HEADS = 8
ATTN_WIDTH = N_HEADS * HEAD_DIM
CONV_CH = 1024
CONV_WIDTH = 31
MOBA_BLOCK = 256
MOBA_TOPK = 3
NUM_BUCKETS = 32
MAX_DISTANCE = 128
N_GROUPS = 4
EXPERTS_PER_GROUP = 8
N_EXPERTS = N_GROUPS * EXPERTS_PER_GROUP
D_EXPERT = 512
EPS = 1e-6
NEG = -1e30

LANES = 128
SUBLANES = 8
CONV_HALO = 32
ROW_TILE = 256
PAGES_PER_CHUNK = 8
DMA_UNROLL = 8
MOE_ROWS = 256
KEY_CHUNK = 512
VMEM_LIMIT = 56 * 1024 * 1024

F32 = jnp.float32
BF16 = jnp.bfloat16
_NT = (((1,), (1,)), ((), ()))


def _params(sem, vmem=VMEM_LIMIT):
    return pltpu.CompilerParams(dimension_semantics=sem, vmem_limit_bytes=vmem)


def _sigmoid(x):
    return 1.0 / (1.0 + jnp.exp(-x))


def _split_bf16(x):
    hi = x.astype(BF16)
    return hi, (x - hi.astype(F32)).astype(BF16)


def _pack_bf16_pairs(x):
    c = x.shape[1] // 2
    lo = lax.bitcast_convert_type(x[:, :c].astype(BF16).astype(F32), jnp.uint32)
    hi = lax.bitcast_convert_type(x[:, c:].astype(BF16).astype(F32), jnp.uint32)
    return (lo >> 16) | (hi & jnp.uint32(0xFFFF0000))


def _unpack_bf16_pairs(p):
    lo = lax.bitcast_convert_type(p << 16, F32).astype(BF16)
    hi = lax.bitcast_convert_type(p & jnp.uint32(0xFFFF0000), F32).astype(BF16)
    return jnp.concatenate([lo, hi], axis=1)


def _resident(shape):
    return pl.BlockSpec(shape, lambda *_: (0,) * len(shape), pipeline_mode=pl.Buffered(1))


class _ScanRider:
    def __init__(self, pt_ref, ck_hbm, bsum_hbm, pgbuf, sumst, psem, osem, *, cps, chunk0, chunks_per_seq,
                 step=None, n_steps=None):
        self.pt_ref, self.ck_hbm, self.bsum_hbm = pt_ref, ck_hbm, bsum_hbm
        self.pgbuf, self.sumst, self.psem, self.osem = pgbuf, sumst, psem, osem
        self.cps, self.chunk0, self.chunks_per_seq = cps, chunk0, chunks_per_seq
        self.step = pl.program_id(0) if step is None else step
        self.n_steps = pl.num_programs(0) if n_steps is None else n_steps
        self.n_local = self.n_steps * cps
        self.bps = sumst.shape[1]

    def _chunk_copies(self, c_local, slot):
        c = self.chunk0 + c_local
        n_pages = self.chunks_per_seq * PAGES_PER_CHUNK
        p0 = (c // self.chunks_per_seq) * n_pages + (c % self.chunks_per_seq) * PAGES_PER_CHUNK
        return [pltpu.make_async_copy(self.ck_hbm.at[self.pt_ref[p0 + r]], self.pgbuf.at[slot, r],
                                      self.psem.at[slot]) for r in range(PAGES_PER_CHUNK)]

    def _sum_copy(self, c_local, k):
        dst = self.bsum_hbm.at[c_local // self.chunks_per_seq,
                               pl.ds((c_local % self.chunks_per_seq) * self.bps, self.bps)]
        return pltpu.make_async_copy(self.sumst.at[k], dst, self.osem)

    def prologue(self):
        @pl.when(self.step == 0)
        def _():
            for cp in self._chunk_copies(0, 0):
                cp.start()

    def phase(self, k):
        c_local = self.step * self.cps + k
        slot = k % 2
        for cp in self._chunk_copies(jnp.minimum(c_local + 1, self.n_local - 1), 1 - slot):
            cp.start()
        for cp in self._chunk_copies(c_local, slot):
            cp.wait()
        ppb = PAGES_PER_CHUNK // self.bps
        page_size = self.pgbuf.shape[2]

        def page_sum(p):
            page = self.pgbuf[slot, p].reshape(page_size // SUBLANES, SUBLANES, N_HEADS, HEAD_DIM)
            return jnp.sum(jnp.sum(page, axis=0), axis=0)

        for t in range(self.bps):
            tot = page_sum(t * ppb)
            for r in range(1, ppb):
                tot = tot + page_sum(t * ppb + r)
            self.sumst[k, t] = tot
        self._sum_copy(c_local, k).start()

    def finish(self):
        for k in range(self.cps):
            self._sum_copy(0, k).wait()

        @pl.when(self.step == self.n_steps - 1)
        def _():
            for cp in self._chunk_copies(self.n_local - 1, self.cps % 2):
                cp.wait()


def _scan_plan(page_table, cache_k, n_steps, part):
    db, n_pages = page_table.shape
    page_size = cache_k.shape[1]
    ppb = MOBA_BLOCK // page_size
    assert MOBA_BLOCK % page_size == 0 and PAGES_PER_CHUNK % ppb == 0 and n_pages % PAGES_PER_CHUNK == 0
    bps = PAGES_PER_CHUNK // ppb
    chunks_per_seq = n_pages // PAGES_PER_CHUNK
    n_chunks = db * chunks_per_seq
    assert n_chunks % 4 == 0
    chunk0, per_part = part[0] * (n_chunks // 4), part[1] * (n_chunks // 4)
    assert per_part % n_steps == 0 and chunk0 % chunks_per_seq == 0 and per_part % chunks_per_seq == 0
    cps = per_part // n_steps
    assert cps % 2 == 0
    cfg = dict(cps=cps, chunk0=chunk0, chunks_per_seq=chunks_per_seq)
    out_shape = jax.ShapeDtypeStruct((per_part // chunks_per_seq, n_pages // ppb, N_HEADS, HEAD_DIM), F32)
    scratch = [pltpu.VMEM((2, PAGES_PER_CHUNK, page_size, N_HEADS, HEAD_DIM), F32),
               pltpu.VMEM((cps, bps, N_HEADS, HEAD_DIM), F32),
               pltpu.SemaphoreType.DMA((2,)),
               pltpu.SemaphoreType.DMA(())]
    return cfg, out_shape, scratch


def _spread(n_items, n_slots):
    return [len(range(s, n_items, n_slots)) for s in range(n_slots)]


def _proj_kernel(pt_ref, x_ref, g_ref, w_ref, *rest, tm, scan):
    if scan:
        ck_hbm, rest = rest[0], rest[1:]
    xn_ref, q_ref, kc_ref, vc_ref, u_ref, ks_ref, k_hbm, v_hbm = rest[:8]
    rest = rest[8:]
    if scan:
        bsum_hbm, rest = rest[0], rest[1:]
    kst, vst, sem = rest[:3]
    rider = _ScanRider(pt_ref, ck_hbm, bsum_hbm, *rest[3:], **scan) if scan else None
    sub = iter(range(scan["cps"])) if scan else iter(())
    per_slot = _spread(scan["cps"], 4) if scan else [0] * 4

    def ride(slot_i):
        for _ in range(per_slot[slot_i]):
            rider.phase(next(sub))

    i = pl.program_id(0)
    aw = ATTN_WIDTH
    if scan:
        rider.prologue()

    def head_copies(src, dst_hbm, row_tile, s):
        r0 = pl.multiple_of(row_tile * tm, tm)
        return [pltpu.make_async_copy(src.at[:, h * HEAD_DIM:(h + 1) * HEAD_DIM],
                                      dst_hbm.at[pl.ds(r0, tm), h, :], sem.at[s]) for h in range(N_HEADS)]

    x = x_ref[...]
    xb = (x * lax.rsqrt(jnp.mean(x * x, axis=-1, keepdims=True) + EPS) * g_ref[...]).astype(BF16)
    xn_ref[...] = xb
    q_ref[...] = jnp.dot(xb, w_ref[:, 0:aw], preferred_element_type=F32)

    def emit(acc, stage, hbm, c_ref, s):
        @pl.when(i > 0)
        def _():
            for cp in head_copies(stage, hbm, i - 1, s):
                cp.wait()
        stage[...] = acc
        c_ref[...] = acc.astype(c_ref.dtype)
        for cp in head_copies(stage, hbm, i, s):
            cp.start()

    ride(0)
    k = jnp.dot(xb, w_ref[:, aw:2 * aw], preferred_element_type=F32)
    emit(k, kst, k_hbm, kc_ref, 0)
    ks_ref[0] = k.reshape(tm // MOBA_BLOCK, MOBA_BLOCK, aw).sum(axis=1)
    ride(1)
    emit(jnp.dot(xb, w_ref[:, 2 * aw:3 * aw], preferred_element_type=F32), vst, v_hbm, vc_ref, 1)
    ride(2)
    ua = jnp.dot(xb, w_ref[:, 3 * aw:3 * aw + CONV_CH], preferred_element_type=F32)
    ride(3)
    ug = jnp.dot(xb, w_ref[:, 3 * aw + CONV_CH:], preferred_element_type=F32)
    u_ref[...] = ua * _sigmoid(ug)
    if scan:
        rider.finish()

    @pl.when(i == pl.num_programs(0) - 1)
    def _():
        for cp in head_copies(kst, k_hbm, i, 0) + head_copies(vst, v_hbm, i, 1):
            cp.wait()


def _proj(x, g1, w_in, tm, kv_dtype, page_table, cache_k=None, scan_part=None):
    n = x.shape[0]
    nblk = tm // MOBA_BLOCK
    row = lambda i, pt: (i, 0)
    any_spec = pl.BlockSpec(memory_space=pl.ANY)
    scan, scan_out, scan_scratch = (None, None, [])
    if scan_part is not None:
        scan, scan_out, scan_scratch = _scan_plan(page_table, cache_k, n // tm, scan_part)
    grid_spec = pltpu.PrefetchScalarGridSpec(
        num_scalar_prefetch=1,
        grid=(n // tm,),
        in_specs=[pl.BlockSpec((tm, D_MODEL), row),
                  _resident((1, D_MODEL)),
                  _resident(w_in.shape)] + ([any_spec] if scan else []),
        out_specs=[pl.BlockSpec((tm, D_MODEL), row),
                   pl.BlockSpec((tm, ATTN_WIDTH), row),
                   pl.BlockSpec((tm, ATTN_WIDTH), row),
                   pl.BlockSpec((tm, ATTN_WIDTH), row),
                   pl.BlockSpec((tm, CONV_CH), row),
                   pl.BlockSpec((1, nblk, ATTN_WIDTH), lambda i, pt: (i, 0, 0)),
                   any_spec, any_spec] + ([any_spec] if scan else []),
        scratch_shapes=[pltpu.VMEM((tm, ATTN_WIDTH), F32),
                        pltpu.VMEM((tm, ATTN_WIDTH), F32),
                        pltpu.SemaphoreType.DMA((2,))] + scan_scratch)
    return pl.pallas_call(
        functools.partial(_proj_kernel, tm=tm, scan=scan),
        grid_spec=grid_spec,
        out_shape=[jax.ShapeDtypeStruct((n, D_MODEL), BF16),
                   jax.ShapeDtypeStruct((n, ATTN_WIDTH), F32),
                   jax.ShapeDtypeStruct((n, ATTN_WIDTH), kv_dtype),
                   jax.ShapeDtypeStruct((n, ATTN_WIDTH), kv_dtype),
                   jax.ShapeDtypeStruct((n, CONV_CH), F32),
                   jax.ShapeDtypeStruct((n // tm, nblk, ATTN_WIDTH), F32),
                   jax.ShapeDtypeStruct((n, N_HEADS, HEAD_DIM), F32),
                   jax.ShapeDtypeStruct((n, N_HEADS, HEAD_DIM), F32)] + ([scan_out] if scan else []),
        compiler_params=_params(("arbitrary",)),
        name="proj",
    )(page_table.reshape(-1), x, g1, w_in, *([cache_k] if scan else []))


def _t5_bucket_py(n):
    max_exact = NUM_BUCKETS // 2
    if n < max_exact:
        return n
    return min(max_exact + int(math.log(n / max_exact) / math.log(MAX_DISTANCE / max_exact)
                               * (NUM_BUCKETS - max_exact)), NUM_BUCKETS - 1)


assert _t5_bucket_py(MOBA_BLOCK + 1) == NUM_BUCKETS - 1


def _far_parts(far, shape):
    hi, lo = _split_bf16(jnp.full(shape, far, F32))
    return hi.astype(F32), lo.astype(F32)


def _bias_kernel(rb_ref, o_ref):
    h = pl.program_id(0)
    shape = (MOBA_BLOCK, MOBA_BLOCK)
    qi = lax.broadcasted_iota(jnp.int32, shape, 0)
    kj = lax.broadcasted_iota(jnp.int32, shape, 1)
    max_exact = NUM_BUCKETS // 2
    far_hi, far_lo = _far_parts(rb_ref[NUM_BUCKETS - 1, h], shape)
    for c in range(2):
        dist = qi - kj + c * MOBA_BLOCK
        nn = jnp.maximum(dist, 0)
        nf = jnp.maximum(nn, 1).astype(F32)
        large = max_exact + (jnp.log(nf / max_exact) / math.log(MAX_DISTANCE / max_exact)
                             * (NUM_BUCKETS - max_exact)).astype(jnp.int32)
        large = jnp.minimum(large, NUM_BUCKETS - 1)
        bkt = jnp.where(nn < max_exact, nn, large)
        val = jnp.zeros(shape, F32)
        for t in range(NUM_BUCKETS):
            val = jnp.where(bkt == t, rb_ref[t, h], val)
        val = val - (far_hi + far_lo)
        if c == 0:
            val = jnp.where(dist >= 0, val, NEG)
        o_ref[0, c] = val


def _bias_tiles(rel_bias):
    return pl.pallas_call(
        _bias_kernel,
        grid=(N_HEADS,),
        in_specs=[pl.BlockSpec(memory_space=pltpu.SMEM)],
        out_specs=pl.BlockSpec((1, 2, MOBA_BLOCK, MOBA_BLOCK), lambda h: (h, 0, 0, 0)),
        out_shape=jax.ShapeDtypeStruct((N_HEADS, 2, MOBA_BLOCK, MOBA_BLOCK), F32),
        compiler_params=_params(("arbitrary",)),
        name="bias_tiles",
    )(rel_bias)


def _attn_kernel(far_ref, q_ref, k_ref, v_ref, km_ref, bias_ref, o_ref, ka, va, qa, s_scr, p_scr, *, nb):
    h = pl.program_id(1)
    blk = MOBA_BLOCK
    seq = nb * blk
    nbp = -(-nb // SUBLANES) * SUBLANES

    row_blk = lax.broadcasted_iota(jnp.int32, (seq, LANES), 0) // blk
    col = lax.broadcasted_iota(jnp.int32, (seq, LANES), 1)
    ka[:, 0:HEAD_DIM] = k_ref[...]
    ka[:, HEAD_DIM:] = jnp.where((col == row_blk) | (col == nbp) | (col == nbp + 1), 1.0, 0.0).astype(BF16)
    va[:, 0:HEAD_DIM] = v_ref[...]
    va[:, HEAD_DIM:] = jnp.ones((seq, LANES), BF16)

    far_hi, far_lo = _far_parts(far_ref[h], (SUBLANES, blk))
    r8 = lax.broadcasted_iota(jnp.int32, (SUBLANES, blk), 0)
    far_rows = jnp.where(r8 == 0, far_hi, jnp.where(r8 == 1, far_lo, 0.0))
    rowi = lax.broadcasted_iota(jnp.int32, (nbp, blk), 0)
    km = km_ref[0, 0, 0:nbp, :]

    for qi in range(nb):
        par = qi % 2
        n = (qi + 1) * blk
        qt = q_ref[qi * blk:n, :]

        if qi > MOBA_TOPK:
            st = lax.dot_general(km, qt, _NT, precision=lax.Precision.HIGHEST, preferred_element_type=F32)
            rank = jnp.zeros((nbp, blk), jnp.int32)
            for i in range(qi):
                si = st[i:i + 1, :]
                rank = rank + ((si > st) | ((si == st) & (rowi > i))).astype(jnp.int32)
            dropped = (rowi < qi) & (rank >= MOBA_TOPK)
            selv = jnp.where(dropped, NEG, 0.0)
        else:
            selv = jnp.zeros((nbp, blk), F32)
        aug = jnp.concatenate([selv, far_rows, jnp.zeros((LANES - nbp - SUBLANES, blk), F32)], axis=0)
        qa[par, :, HEAD_DIM:] = aug.T.astype(BF16)
        qa[par, :, 0:HEAD_DIM] = (qt * (HEAD_DIM ** -0.5)).astype(BF16)

        m_acc = jnp.full((blk, LANES), NEG, F32)
        for c0 in range(0, n, KEY_CHUNK):
            w = min(KEY_CHUNK, n - c0)
            sc = lax.dot_general(qa[par], ka[c0:c0 + w, :], _NT, preferred_element_type=F32)
            pieces = []
            for t0 in range(0, w, blk):
                piece = sc[:, t0:t0 + blk]
                kb = (c0 + t0) // blk
                if kb == qi:
                    piece = piece + bias_ref[0, 0]
                elif kb == qi - 1:
                    piece = piece + bias_ref[0, 1]
                pieces.append(piece)
                for l0 in range(0, blk, LANES):
                    m_acc = jnp.maximum(m_acc, piece[:, l0:l0 + LANES])
            s_scr[par, :, c0:c0 + w] = pieces[0] if len(pieces) == 1 else jnp.concatenate(pieces, axis=1)
        m = jnp.max(m_acc, axis=1, keepdims=True)
        for c0 in range(0, n, KEY_CHUNK):
            w = min(KEY_CHUNK, n - c0)
            p_scr[par, :, c0:c0 + w] = jnp.exp(s_scr[par, :, c0:c0 + w] - m).astype(BF16)
        out = jnp.dot(p_scr[par, :, 0:n], va[0:n, :], preferred_element_type=F32)
        o_ref[qi * blk:n, :] = (out[:, 0:HEAD_DIM] / out[:, HEAD_DIM:HEAD_DIM + 1]).astype(o_ref.dtype)


def _prompt_attention(q, kc, vc, kmeans, bias, far_bias, batch, seq):
    nb = seq // MOBA_BLOCK
    nbp = -(-nb // SUBLANES) * SUBLANES
    assert nbp + SUBLANES <= LANES and (MOBA_BLOCK % KEY_CHUNK == 0 or KEY_CHUNK % MOBA_BLOCK == 0)
    seq_blk = lambda b, h: (b, h)
    return pl.pallas_call(
        functools.partial(_attn_kernel, nb=nb),
        grid=(batch, N_HEADS),
        in_specs=[pl.BlockSpec(memory_space=pltpu.SMEM),
                  pl.BlockSpec((seq, HEAD_DIM), seq_blk),
                  pl.BlockSpec((seq, HEAD_DIM), seq_blk),
                  pl.BlockSpec((seq, HEAD_DIM), seq_blk),
                  pl.BlockSpec((1, 1, LANES, HEAD_DIM), lambda b, h: (b, h, 0, 0)),
                  pl.BlockSpec((1, 2, MOBA_BLOCK, MOBA_BLOCK), lambda b, h: (h, 0, 0, 0))],
        out_specs=pl.BlockSpec((seq, HEAD_DIM), seq_blk),
        out_shape=jax.ShapeDtypeStruct((batch * seq, ATTN_WIDTH), BF16),
        scratch_shapes=[pltpu.VMEM((seq, 2 * HEAD_DIM), BF16),
                        pltpu.VMEM((seq, 2 * HEAD_DIM), BF16),
                        pltpu.VMEM((2, MOBA_BLOCK, 2 * HEAD_DIM), BF16),
                        pltpu.VMEM((2, MOBA_BLOCK, seq), F32),
                        pltpu.VMEM((2, MOBA_BLOCK, seq), BF16)],
        compiler_params=_params(("parallel", "parallel")),
        name="prompt_attn",
    )(far_bias, q, kc, vc, kmeans, bias)


def _sselect_kernel(q_ref, km_ref, o_ref, *, own):
    s_new = q_ref.shape[0]
    col = lax.broadcasted_iota(jnp.int32, (s_new, LANES), 1)
    colf = col.astype(F32)
    for h in range(N_HEADS):
        qh = q_ref[:, h * HEAD_DIM:(h + 1) * HEAD_DIM]
        s = lax.dot_general(qh, km_ref[0, h], _NT, precision=lax.Precision.HIGHEST, preferred_element_type=F32)
        s = jnp.where(col < own, s, NEG)
        out = jnp.zeros((s_new, LANES), F32)
        for t in range(MOBA_TOPK):
            mx = jnp.max(s, axis=1, keepdims=True)
            arg = jnp.min(jnp.where(s == mx, colf, float(LANES)), axis=1, keepdims=True)
            out = jnp.where(col == t, arg, out)
            s = jnp.where(colf == arg, -jnp.inf, s)
        o_ref[0, h] = out.astype(jnp.int32)


def _sample_select(q, kmeans_s, db, s_new, own):
    assert MOBA_TOPK <= own < LANES
    return pl.pallas_call(
        functools.partial(_sselect_kernel, own=own),
        grid=(db,),
        in_specs=[pl.BlockSpec((s_new, ATTN_WIDTH), lambda b: (b, 0)),
                  pl.BlockSpec((1, N_HEADS, LANES, HEAD_DIM), lambda b: (b, 0, 0, 0))],
        out_specs=pl.BlockSpec((1, N_HEADS, s_new, LANES), lambda b: (b, 0, 0, 0)),
        out_shape=jax.ShapeDtypeStruct((db, N_HEADS, s_new, LANES), jnp.int32),
        compiler_params=_params(("parallel",)),
        name="sample_select",
    )(q, kmeans_s)


def _sattn_kernel(pt_ref, idx_ref, far_ref, q_ref, kn_ref, vn_ref, bias_ref, ck_hbm, cv_hbm, o_ref,
                  kbuf, vbuf, sem, *, n_pages, page_size, own, s_new):
    b = pl.program_id(0)
    h = pl.program_id(1)
    nh = pl.num_programs(1)
    step = b * nh + h
    nsteps = pl.num_programs(0) * nh
    slot = step % 2
    ppb = MOBA_BLOCK // page_size
    nsel = s_new * MOBA_TOPK
    nkeys = nsel * MOBA_BLOCK

    def copies(step_, slot_):
        b_ = step_ // nh
        h_ = step_ % nh
        out = []
        for c in range(nsel):
            blk = idx_ref[step_ * nsel + c]
            for r in range(ppb):
                phys = pt_ref[b_ * n_pages + blk * ppb + r]
                dst = pl.ds((c * ppb + r) * page_size, page_size)
                out.append(pltpu.make_async_copy(ck_hbm.at[phys, :, h_, :], kbuf.at[slot_, dst, :], sem.at[slot_, 0]))
                out.append(pltpu.make_async_copy(cv_hbm.at[phys, :, h_, :], vbuf.at[slot_, dst, :], sem.at[slot_, 1]))
        return out

    @pl.when(step == 0)
    def _():
        for cp in copies(step, slot):
            cp.start()

    @pl.when(step + 1 < nsteps)
    def _():
        for cp in copies(step + 1, 1 - slot):
            cp.start()

    for cp in copies(step, slot):
        cp.wait()

    far_hi, far_lo = _far_parts(far_ref[h], (s_new, LANES))
    far = (far_hi + far_lo)[:, 0:1]
    qb = (q_ref[...] * (HEAD_DIM ** -0.5)).astype(BF16)
    kb = kbuf[slot].astype(BF16)
    vb = vbuf[slot].astype(BF16)
    lg = lax.dot_general(qb, kb, _NT, preferred_element_type=F32) + far
    pieces = []
    for c in range(nsel):
        is_prev = idx_ref[step * nsel + c] == own - 1
        row = bias_ref[0, 1, (c // MOBA_TOPK):(c // MOBA_TOPK) + 1, :]
        pieces.append(jnp.where(is_prev, row, 0.0))
    lg = lg + jnp.concatenate(pieces, axis=1)
    col = lax.broadcasted_iota(jnp.int32, (s_new, nkeys), 1)
    row_i = lax.broadcasted_iota(jnp.int32, (s_new, nkeys), 0)
    mine = (col >= row_i * (MOBA_TOPK * MOBA_BLOCK)) & (col < (row_i + 1) * (MOBA_TOPK * MOBA_BLOCK))
    lg = jnp.where(mine, lg, NEG)
    lg_own = (lax.dot_general(qb, kn_ref[...].astype(BF16), _NT, preferred_element_type=F32)
              + far + bias_ref[0, 0, 0:s_new, 0:s_new])
    m = jnp.maximum(jnp.max(lg, axis=1, keepdims=True), jnp.max(lg_own, axis=1, keepdims=True))
    p = jnp.exp(lg - m)
    p_own = jnp.exp(lg_own - m)
    denom = jnp.sum(p, axis=1, keepdims=True) + jnp.sum(p_own, axis=1, keepdims=True)
    acc = (jnp.dot(p.astype(BF16), vb, preferred_element_type=F32)
           + jnp.dot(p_own.astype(BF16), vn_ref[...].astype(BF16), preferred_element_type=F32))
    o_ref[...] = acc / denom


def _sample_attention(q, k, v, cache_k, cache_v, page_table, sel_idx, bias, far_bias, s_new, own):
    db, n_pages = page_table.shape
    page_size = cache_k.shape[1]
    nkeys = s_new * MOBA_TOPK * MOBA_BLOCK
    new_spec = pl.BlockSpec((s_new, HEAD_DIM), lambda b, h, pt, ix: (b, h))
    grid_spec = pltpu.PrefetchScalarGridSpec(
        num_scalar_prefetch=2,
        grid=(db, N_HEADS),
        in_specs=[pl.BlockSpec(memory_space=pltpu.SMEM),
                  new_spec, new_spec, new_spec,
                  pl.BlockSpec((1, 2, MOBA_BLOCK, MOBA_BLOCK), lambda b, h, pt, ix: (h, 0, 0, 0)),
                  pl.BlockSpec(memory_space=pl.ANY),
                  pl.BlockSpec(memory_space=pl.ANY)],
        out_specs=new_spec,
        scratch_shapes=[pltpu.VMEM((2, nkeys, HEAD_DIM), F32),
                        pltpu.VMEM((2, nkeys, HEAD_DIM), F32),
                        pltpu.SemaphoreType.DMA((2, 2))])
    return pl.pallas_call(
        functools.partial(_sattn_kernel, n_pages=n_pages, page_size=page_size, own=own, s_new=s_new),
        grid_spec=grid_spec,
        out_shape=jax.ShapeDtypeStruct((db * s_new, ATTN_WIDTH), F32),
        compiler_params=_params(("arbitrary", "arbitrary")),
        name="sample_attn",
    )(page_table.reshape(-1), sel_idx.reshape(-1), far_bias, q, k, v, bias, cache_k, cache_v)


def _conv_kernel(pt_ref, halo_ref, cur_ref, cw_ref, cb_ref, lg_ref, lb_ref, *rest, tt, rc, zero_first, scan):
    i = pl.program_id(1)
    if scan:
        ck_hbm, o_ref, bsum_hbm, ext_scr, sh = rest[:5]
        rider = _ScanRider(pt_ref, ck_hbm, bsum_hbm, *rest[5:], **scan,
                           step=pl.program_id(0) * pl.num_programs(1) + i,
                           n_steps=pl.num_programs(0) * pl.num_programs(1))
        rider.prologue()
    else:
        o_ref, ext_scr, sh = rest
    if zero_first:
        @pl.when(i == 0)
        def _():
            ext_scr[0:CONV_HALO, :] = jnp.zeros((CONV_HALO, CONV_CH), F32)

        @pl.when(i > 0)
        def _():
            ext_scr[0:CONV_HALO, :] = halo_ref[...]
    else:
        ext_scr[0:CONV_HALO, :] = halo_ref[...]
    ext_scr[CONV_HALO:CONV_HALO + tt, :] = cur_ref[...]
    lead = CONV_HALO - (CONV_WIDTH - 1)

    for r in range(SUBLANES):
        span = tt + CONV_HALO - r
        for s0 in range(0, span, LANES):
            rows = min(LANES, span - s0)
            sh[r, s0:s0 + rows, :] = ext_scr[r + s0:r + s0 + rows, :]

    def chunk(c, carry):
        r0 = pl.multiple_of(c * rc, rc)
        acc = jnp.zeros((rc, CONV_CH), F32)
        for w in range(CONV_WIDTH):
            r = (lead + w) % SUBLANES
            a0 = lead + w - r
            acc = acc + sh[r, pl.ds(pl.multiple_of(r0 + a0, SUBLANES), rc), :] * cw_ref[w:w + 1, :]
        hc = acc + cb_ref[...]
        mu = jnp.mean(hc, axis=-1, keepdims=True)
        var = jnp.mean(jnp.square(hc - mu), axis=-1, keepdims=True)
        y = (hc - mu) * lax.rsqrt(var + EPS) * lg_ref[...] + lb_ref[...]
        o_ref[pl.ds(r0, rc), :] = (y * _sigmoid(y)).astype(o_ref.dtype)
        return carry

    n_chunks = tt // rc
    if scan:
        cps = scan["cps"]
        assert n_chunks % cps == 0
        for k in range(cps):
            rider.phase(k)
            lax.fori_loop(k * (n_chunks // cps), (k + 1) * (n_chunks // cps), chunk, 0)
        rider.finish()
    else:
        lax.fori_loop(0, n_chunks, chunk, 0)


def _conv_module(halo_src, halo_map, cur_src, conv_w, conv_b, ln_g, ln_b, n_seq, n_tiles, tt,
                 zero_first, out_dtype, page_table, cache_k=None, scan_part=None):
    rc = min(tt, 32)
    vec = pl.BlockSpec((1, CONV_CH), lambda b, i, pt: (0, 0))
    cur_map = lambda b, i, pt: (b * n_tiles + i, 0)
    any_spec = pl.BlockSpec(memory_space=pl.ANY)
    scan, scan_out, scan_scratch = (None, None, [])
    if scan_part is not None:
        scan, scan_out, scan_scratch = _scan_plan(page_table, cache_k, n_seq * n_tiles, scan_part)
    grid_spec = pltpu.PrefetchScalarGridSpec(
        num_scalar_prefetch=1,
        grid=(n_seq, n_tiles),
        in_specs=[pl.BlockSpec((CONV_HALO, CONV_CH), halo_map),
                  pl.BlockSpec((tt, CONV_CH), cur_map),
                  pl.BlockSpec((CONV_WIDTH, CONV_CH), lambda b, i, pt: (0, 0)),
                  vec, vec, vec] + ([any_spec] if scan else []),
        out_specs=[pl.BlockSpec((tt, CONV_CH), cur_map)] + ([any_spec] if scan else []),
        scratch_shapes=[pltpu.VMEM((CONV_HALO + tt, CONV_CH), F32),
                        pltpu.VMEM((SUBLANES, tt + CONV_HALO, CONV_CH), F32)] + scan_scratch)
    return pl.pallas_call(
        functools.partial(_conv_kernel, tt=tt, rc=rc, zero_first=zero_first, scan=scan),
        grid_spec=grid_spec,
        out_shape=[jax.ShapeDtypeStruct((n_seq * n_tiles * tt, CONV_CH), out_dtype)] + ([scan_out] if scan else []),
        compiler_params=_params(("arbitrary", "arbitrary")),
        name="conv_module",
    )(page_table.reshape(-1), halo_src, cur_src, conv_w, conv_b, ln_g, ln_b, *([cache_k] if scan else []))


def _mix_kernel(pt_ref, xn_ref, at_ref, cv_ref, wg_ref, bg_ref, wa_ref, wc_ref, *rest, tn, scan):
    if scan:
        ck_hbm, o_ref, bsum_hbm = rest[:3]
        rider = _ScanRider(pt_ref, ck_hbm, bsum_hbm, *rest[3:], **scan)
        rider.prologue()
        per_slot = _spread(scan["cps"], D_MODEL // tn)
        sub = iter(range(scan["cps"]))
    else:
        o_ref = rest[0]
    xn = xn_ref[...]
    at = at_ref[...].astype(BF16)
    cv = cv_ref[...].astype(BF16)
    for c0 in range(0, D_MODEL, tn):
        if scan:
            for _ in range(per_slot[c0 // tn]):
                rider.phase(next(sub))
        a_cols = slice(c0, c0 + tn)
        c_cols = slice(D_MODEL + c0, D_MODEL + c0 + tn)
        ga = _sigmoid(jnp.dot(xn, wg_ref[:, a_cols], preferred_element_type=F32) + bg_ref[:, a_cols])
        gc = _sigmoid(jnp.dot(xn, wg_ref[:, c_cols], preferred_element_type=F32) + bg_ref[:, c_cols])
        ya = jnp.dot(at, wa_ref[:, a_cols], preferred_element_type=F32)
        yc = jnp.dot(cv, wc_ref[:, a_cols], preferred_element_type=F32)
        o_ref[:, a_cols] = (ga * ya + gc * yc).astype(o_ref.dtype)
    if scan:
        rider.finish()


def _gate_mix(xn, attn, cva, w_gate, b_gate, w_attn_out, w_conv_out, tm, page_table, cache_k=None,
              scan_part=None, tn=512):
    n = xn.shape[0]
    row = lambda i, pt: (i, 0)
    any_spec = pl.BlockSpec(memory_space=pl.ANY)
    scan, scan_out, scan_scratch = (None, None, [])
    if scan_part is not None:
        scan, scan_out, scan_scratch = _scan_plan(page_table, cache_k, n // tm, scan_part)
    grid_spec = pltpu.PrefetchScalarGridSpec(
        num_scalar_prefetch=1,
        grid=(n // tm,),
        in_specs=[pl.BlockSpec((tm, D_MODEL), row),
                  pl.BlockSpec((tm, ATTN_WIDTH), row),
                  pl.BlockSpec((tm, CONV_CH), row),
                  _resident(w_gate.shape), _resident(b_gate.shape),
                  _resident(w_attn_out.shape), _resident(w_conv_out.shape)] + ([any_spec] if scan else []),
        out_specs=[pl.BlockSpec((tm, D_MODEL), row)] + ([any_spec] if scan else []),
        scratch_shapes=scan_scratch)
    return pl.pallas_call(
        functools.partial(_mix_kernel, tn=tn, scan=scan),
        grid_spec=grid_spec,
        out_shape=[jax.ShapeDtypeStruct((n, D_MODEL), BF16)] + ([scan_out] if scan else []),
        compiler_params=_params(("arbitrary",)),
        name="gate_mix",
    )(page_table.reshape(-1), xn, attn, cva, w_gate, b_gate, w_attn_out, w_conv_out, *([cache_k] if scan else []))


def _outproj_kernel(pt_ref, x_ref, mix_ref, wo_ref, g2_ref, wrh_ref, wrl_ref, br_ref, cin_ref, *rest, scan):
    if scan:
        ck_hbm, x1_ref, xn2_ref, rt_ref, cout_ref, bsum_hbm, run = rest[:7]
        rider = _ScanRider(pt_ref, ck_hbm, bsum_hbm, *rest[7:], **scan)
        rider.prologue()
        first = -(-scan["cps"] // 2)
        for k in range(first):
            rider.phase(k)
    else:
        x1_ref, xn2_ref, rt_ref, cout_ref, run = rest

    @pl.when(pl.program_id(0) == 0)
    def _():
        run[...] = cin_ref[...]

    x1 = x_ref[...] + jnp.dot(mix_ref[...], wo_ref[...], preferred_element_type=F32)
    x1_ref[...] = x1
    xn2 = x1 * lax.rsqrt(jnp.mean(x1 * x1, axis=-1, keepdims=True) + EPS) * g2_ref[...]
    xn2_ref[...] = _pack_bf16_pairs(xn2)
    if scan:
        for k in range(first, scan["cps"]):
            rider.phase(k)
    hi, lo = _split_bf16(xn2)
    wrh = wrh_ref[...]
    logits = (jnp.dot(hi, wrh, preferred_element_type=F32) + jnp.dot(lo, wrh, preferred_element_type=F32)
              + jnp.dot(hi, wrl_ref[...], preferred_element_type=F32) + br_ref[...])
    tm = logits.shape[0]
    col = lax.broadcasted_iota(jnp.int32, (tm, LANES), 1).astype(F32)
    big = float(LANES)

    def first_argmax(vals):
        mx = jnp.max(vals, axis=1, keepdims=True)
        return mx, jnp.min(jnp.where(vals == mx, col, big), axis=1, keepdims=True)

    gmask = col < N_GROUPS
    gmax, gidx = first_argmax(jnp.where(gmask, logits, -jnp.inf))
    p_g = 1.0 / jnp.sum(jnp.where(gmask, jnp.exp(logits - gmax), 0.0), axis=1, keepdims=True)
    e_lo = N_GROUPS + gidx * EXPERTS_PER_GROUP
    el = jnp.where((col >= e_lo) & (col < e_lo + EXPERTS_PER_GROUP), logits, -jnp.inf)
    v1, i1 = first_argmax(el)
    v2, i2 = first_argmax(jnp.where(col == i1, -jnp.inf, el))
    e21 = jnp.exp(v2 - v1)
    p1 = 1.0 / (1.0 + e21)
    p2 = e21 / (1.0 + e21)
    e1 = i1 - N_GROUPS
    e2 = i2 - N_GROUPS

    oh1 = (col == e1).astype(F32)
    oh2 = (col == e2).astype(F32)
    both = oh1 + oh2
    ri = lax.broadcasted_iota(jnp.int32, (tm, tm), 0)
    ci = lax.broadcasted_iota(jnp.int32, (tm, tm), 1)
    lower = (ci < ri).astype(BF16)
    before = jnp.dot(lower, both.astype(BF16), preferred_element_type=F32) + run[...]
    pos1 = jnp.sum(before * oh1, axis=1, keepdims=True)
    pos2 = jnp.sum(before * oh2, axis=1, keepdims=True)
    new_run = run[...] + jnp.sum(both, axis=0, keepdims=True)
    run[...] = new_run
    cout_ref[...] = new_run

    rt = jnp.zeros((tm, LANES), F32)
    for lane, val in enumerate((e1, e2, p_g * p1, p_g * p2, pos1, pos2)):
        rt = jnp.where(col == lane, val, rt)
    rt_ref[...] = rt
    if scan:
        rider.finish()


def _out_proj(x, mix, w_out, g2, wr_hi, wr_lo, br, cnt_in, tm, page_table, cache_k=None, scan_part=None):
    n = x.shape[0]
    row = lambda i, pt: (i, 0)
    const = lambda i, pt: (0, 0)
    any_spec = pl.BlockSpec(memory_space=pl.ANY)
    scan, scan_out, scan_scratch = (None, None, [])
    if scan_part is not None:
        scan, scan_out, scan_scratch = _scan_plan(page_table, cache_k, n // tm, scan_part)
    grid_spec = pltpu.PrefetchScalarGridSpec(
        num_scalar_prefetch=1,
        grid=(n // tm,),
        in_specs=[pl.BlockSpec((tm, D_MODEL), row),
                  pl.BlockSpec((tm, D_MODEL), row),
                  _resident((D_MODEL, D_MODEL)),
                  _resident((1, D_MODEL)),
                  _resident((D_MODEL, LANES)),
                  _resident((D_MODEL, LANES)),
                  _resident((1, LANES)),
                  _resident((1, LANES))] + ([any_spec] if scan else []),
        out_specs=[pl.BlockSpec((tm, D_MODEL), row),
                   pl.BlockSpec((tm, D_MODEL // 2), row),
                   pl.BlockSpec((tm, LANES), row),
                   pl.BlockSpec((1, LANES), const)] + ([any_spec] if scan else []),
        scratch_shapes=[pltpu.VMEM((1, LANES), F32)] + scan_scratch)
    return pl.pallas_call(
        functools.partial(_outproj_kernel, scan=scan),
        grid_spec=grid_spec,
        out_shape=[jax.ShapeDtypeStruct((n, D_MODEL), F32),
                   jax.ShapeDtypeStruct((n, D_MODEL // 2), jnp.uint32),
                   jax.ShapeDtypeStruct((n, LANES), F32),
                   jax.ShapeDtypeStruct((1, LANES), F32)] + ([scan_out] if scan else []),
        compiler_params=_params(("arbitrary",)),
        name="out_proj",
    )(page_table.reshape(-1), x, mix, w_out, g2, wr_hi, wr_lo, br, cnt_in, *([cache_k] if scan else []))


def _dispatch_kernel(dest_ref, pad0_ref, padn_ref, tail_ref, xp_ref, xs_ref, xb_out, zbuf, sem, zsem, *,
                     tm, np_tiles, n_rows):
    i = pl.program_id(0)

    def issue_rows(x_ref):
        def row_copy(t, d):
            return pltpu.make_async_copy(x_ref.at[pl.ds(t, 1), :], xb_out.at[pl.ds(d, 1), :], sem)

        def issue(t, carry):
            base = (i * tm + t) * 2
            row_copy(t, dest_ref[base]).start()
            row_copy(t, dest_ref[base + 1]).start()
            return carry

        lax.fori_loop(0, tm, issue, 0, unroll=DMA_UNROLL)

    @pl.when(i < np_tiles)
    def _():
        issue_rows(xp_ref)

    @pl.when(i >= np_tiles)
    def _():
        issue_rows(xs_ref)

    @pl.when(i == 0)
    def _():
        zbuf[...] = jnp.zeros(zbuf.shape, zbuf.dtype)

        def zero_row(d):
            return pltpu.make_async_copy(zbuf.at[pl.ds(0, 1), :], xb_out.at[pl.ds(d, 1), :], zsem)

        def zero_blk(d):
            return pltpu.make_async_copy(zbuf, xb_out.at[pl.ds(d, MOE_ROWS), :], zsem)

        def each(n, fn):
            def body(r, carry):
                fn(r)
                return carry
            lax.fori_loop(0, n, body, 0)

        n_tail = (n_rows - tail_ref[0]) // MOE_ROWS
        for e in range(N_EXPERTS):
            each(padn_ref[e], lambda r, e=e: zero_row(pad0_ref[e] + r).start())
        each(n_tail, lambda j: zero_blk(pl.multiple_of(tail_ref[0] + j * MOE_ROWS, MOE_ROWS)).start())
        for e in range(N_EXPERTS):
            each(padn_ref[e], lambda r: zero_row(0).wait())
        each(n_tail, lambda j: zero_blk(0).wait())

    for _ in range(2):
        pltpu.make_async_copy(xp_ref, xb_out.at[pl.ds(0, tm), :], sem).wait()


def _dispatch(xn2_p, xn2_s, dest, pad0, padn, tail, n_rows, tm):
    np_tiles = xn2_p.shape[0] // tm
    ns_tiles = xn2_s.shape[0] // tm
    width = xn2_p.shape[1]
    grid_spec = pltpu.PrefetchScalarGridSpec(
        num_scalar_prefetch=4,
        grid=(np_tiles + ns_tiles,),
        in_specs=[pl.BlockSpec((tm, width), lambda i, *_: (jnp.minimum(i, np_tiles - 1), 0)),
                  pl.BlockSpec((tm, width), lambda i, *_: (jnp.maximum(i - np_tiles, 0), 0))],
        out_specs=pl.BlockSpec(memory_space=pl.ANY),
        scratch_shapes=[pltpu.VMEM((MOE_ROWS, width), xn2_p.dtype),
                        pltpu.SemaphoreType.DMA(()),
                        pltpu.SemaphoreType.DMA(())])
    return pl.pallas_call(
        functools.partial(_dispatch_kernel, tm=tm, np_tiles=np_tiles, n_rows=n_rows),
        grid_spec=grid_spec,
        out_shape=jax.ShapeDtypeStruct((n_rows, width), xn2_p.dtype),
        compiler_params=_params(("arbitrary",)),
        name="moe_dispatch",
    )(dest, pad0, padn, tail, xn2_p, xn2_s)


def _moe_kernel(be_ref, nu_ref, x_ref, w1_ref, w3_ref, w2_ref, o_ref, w1b, w3b, w2b):
    g = pl.program_id(0)
    used = g < nu_ref[0]
    new_expert = (g == 0) | (be_ref[g] != be_ref[jnp.maximum(g - 1, 0)])

    @pl.when(used & new_expert)
    def _():
        w1b[...] = w1_ref[0].astype(BF16)
        w3b[...] = w3_ref[0].astype(BF16)
        w2b[...] = w2_ref[0].astype(BF16)

    @pl.when(used)
    def _():
        xb = _unpack_bf16_pairs(x_ref[...])
        a = jnp.dot(xb, w1b[...], preferred_element_type=F32)
        gt = jnp.dot(xb, w3b[...], preferred_element_type=F32)
        hmid = (a * _sigmoid(a) * gt).astype(BF16)
        o_ref[...] = jnp.dot(hmid, w2b[...], preferred_element_type=F32)

    @pl.when(jnp.logical_not(used))
    def _():
        o_ref[...] = jnp.zeros(o_ref.shape, F32)


def _moe_experts(xb, block_e, n_used, w1, w3, w2):
    n_blocks = xb.shape[0] // MOE_ROWS
    blk = lambda g, be, nu: (jnp.minimum(g, nu[0] - 1), 0)
    wsel = lambda g, be, nu: (be[jnp.minimum(g, nu[0] - 1)], 0, 0)
    grid_spec = pltpu.PrefetchScalarGridSpec(
        num_scalar_prefetch=2,
        grid=(n_blocks,),
        in_specs=[pl.BlockSpec((MOE_ROWS, xb.shape[1]), blk),
                  pl.BlockSpec((1, D_MODEL, D_EXPERT), wsel),
                  pl.BlockSpec((1, D_MODEL, D_EXPERT), wsel),
                  pl.BlockSpec((1, D_EXPERT, D_MODEL), wsel)],
        out_specs=pl.BlockSpec((MOE_ROWS, D_MODEL), lambda g, be, nu: (g, 0)),
        scratch_shapes=[pltpu.VMEM((D_MODEL, D_EXPERT), BF16),
                        pltpu.VMEM((D_MODEL, D_EXPERT), BF16),
                        pltpu.VMEM((D_EXPERT, D_MODEL), BF16)])
    return pl.pallas_call(
        _moe_kernel,
        grid_spec=grid_spec,
        out_shape=jax.ShapeDtypeStruct((xb.shape[0], D_MODEL), F32),
        compiler_params=_params(("arbitrary",)),
        name="moe_experts",
    )(block_e, n_used, xb, w1, w3, w2)


def _combine_kernel(dest_ref, x1_ref, rt_ref, gf_ref, yb_hbm, o_ref, ybuf, sem, *, tm):
    i = pl.program_id(0)
    slot = i % 2

    def issue_tile(tile, s):
        def row_copy(k, t, d):
            return pltpu.make_async_copy(yb_hbm.at[pl.ds(d, 1), :], ybuf.at[s, k, pl.ds(t, 1), :], sem.at[s])

        def issue(t, carry):
            base = (tile * tm + t) * 2
            row_copy(0, t, dest_ref[base]).start()
            row_copy(1, t, dest_ref[base + 1]).start()
            return carry

        lax.fori_loop(0, tm, issue, 0, unroll=DMA_UNROLL)

    @pl.when(i == 0)
    def _():
        issue_tile(i, slot)

    @pl.when(i + 1 < pl.num_programs(0))
    def _():
        issue_tile(i + 1, 1 - slot)

    for k in range(2):
        pltpu.make_async_copy(yb_hbm.at[pl.ds(0, tm), :], ybuf.at[slot, k], sem.at[slot]).wait()
    rt = rt_ref[...]
    moe = ybuf[slot, 0] * rt[:, 2:3] + ybuf[slot, 1] * rt[:, 3:4]
    xo = x1_ref[...] + moe
    o_ref[...] = xo * lax.rsqrt(jnp.mean(xo * xo, axis=-1, keepdims=True) + EPS) * gf_ref[...]


def _combine(x1, route, gf, yb, dest, tm):
    n = x1.shape[0]
    grid_spec = pltpu.PrefetchScalarGridSpec(
        num_scalar_prefetch=1,
        grid=(n // tm,),
        in_specs=[pl.BlockSpec((tm, D_MODEL), lambda i, d: (i, 0)),
                  pl.BlockSpec((tm, LANES), lambda i, d: (i, 0)),
                  pl.BlockSpec((1, D_MODEL), lambda i, d: (0, 0)),
                  pl.BlockSpec(memory_space=pl.ANY)],
        out_specs=pl.BlockSpec((tm, D_MODEL), lambda i, d: (i, 0)),
        scratch_shapes=[pltpu.VMEM((2, 2, tm, D_MODEL), F32),
                        pltpu.SemaphoreType.DMA((2,))])
    return pl.pallas_call(
        functools.partial(_combine_kernel, tm=tm),
        grid_spec=grid_spec,
        out_shape=jax.ShapeDtypeStruct((n, D_MODEL), F32),
        compiler_params=_params(("arbitrary",)),
        name="moe_combine",
    )(dest, x1, route, gf, yb)


def _dispatch_plan(routes, counts):
    counts = counts[0, :N_EXPERTS].astype(jnp.int32)
    pcounts = (counts + MOE_ROWS - 1) // MOE_ROWS * MOE_ROWS
    pend = jnp.cumsum(pcounts)
    pstart = pend - pcounts
    dests = []
    n_assign = 0
    for rt in routes:
        e = rt[:, 0:2].astype(jnp.int32)
        pos = rt[:, 4:6].astype(jnp.int32)
        onehot = e[:, :, None] == jnp.arange(N_EXPERTS, dtype=jnp.int32)[None, None, :]
        dests.append((jnp.sum(jnp.where(onehot, pstart[None, None, :], 0), axis=-1) + pos).reshape(-1))
        n_assign += e.size
    n_blocks = (n_assign + N_EXPERTS * (MOE_ROWS - 1) + MOE_ROWS - 1) // MOE_ROWS
    first_row = jnp.arange(n_blocks, dtype=jnp.int32) * MOE_ROWS
    block_e = jnp.minimum(jnp.sum((pend[None, :] <= first_row[:, None]).astype(jnp.int32), axis=1), N_EXPERTS - 1)
    n_used = (pend[-1] // MOE_ROWS).astype(jnp.int32).reshape(1)
    pads = (pstart + counts, pcounts - counts, pend[-1:])
    return dests, block_e, n_used, n_blocks * MOE_ROWS, pads


def kernel(x_prompt, x_sample, cache_k, cache_v, state_conv, page_table, rel_bias, norm1_g, w_in, w_attn_out,
           conv_w, conv_b, conv_ln_g, conv_ln_b, w_conv_out, w_gate, b_gate, w_out, norm2_g, router_g_w,
           router_g_b, router_e_w, router_e_b, moe_w1, moe_w3, moe_w2, norm_f_g):
    depth = norm1_g.shape[0]
    assert depth == 1
    batch, seq, _ = x_prompt.shape
    db, s_new, _ = x_sample.shape
    n_pages = page_table.shape[1]
    page_size = cache_k.shape[2]
    past_len = n_pages * page_size
    assert seq % MOBA_BLOCK == 0 and past_len % MOBA_BLOCK == 0 and s_new <= MOBA_BLOCK and s_new % SUBLANES == 0
    own_s = past_len // MOBA_BLOCK
    n_p = batch * seq
    n_s = db * s_new
    tm = ROW_TILE
    assert n_p % tm == 0 and n_s % tm == 0

    wi_b = w_in[0].astype(BF16)
    wr = jnp.concatenate([router_g_w[0], router_e_w[0]], axis=1)
    wr = jnp.pad(wr, ((0, 0), (0, LANES - wr.shape[1])))
    wr_hi, wr_lo = _split_bf16(wr)
    br = jnp.pad(jnp.concatenate([router_g_b[0], router_e_b[0]]), (0, LANES - N_GROUPS - N_EXPERTS))[None, :]
    far_bias = rel_bias[NUM_BUCKETS - 1, :]
    wg_b, wa_b, wc_b, wo_b = (w[0].astype(BF16) for w in (w_gate, w_attn_out, w_conv_out, w_out))
    cw = conv_w[0]
    bias = _bias_tiles(rel_bias)

    ck, cv = cache_k[0], cache_v[0]
    xn_p, q_p, kc_p, vc_p, u_p, ksum, k_p, v_p, bsum_a = _proj(
        x_prompt.reshape(n_p, D_MODEL), norm1_g, wi_b, tm, BF16, page_table, ck, (0, 1))
    xn_s, q_s, kc_s, vc_s, u_s, _, k_s, v_s = _proj(x_sample.reshape(n_s, D_MODEL), norm1_g, wi_b, tm, F32, page_table)

    nb = seq // MOBA_BLOCK
    km = ksum.reshape(batch, nb, N_HEADS, HEAD_DIM) / MOBA_BLOCK
    km = jnp.pad(km.transpose(0, 2, 1, 3), ((0, 0), (0, 0), (0, LANES - nb), (0, 0)))
    attn_p = _prompt_attention(q_p, kc_p, vc_p, km, bias, far_bias, batch, seq)

    tt = 512
    tiles = seq // tt
    per_halo = tt // CONV_HALO
    cva_p, bsum_b = _conv_module(u_p, lambda b, i, pt: (jnp.maximum((b * tiles + i) * per_halo - 1, 0), 0), u_p,
                                 cw, conv_b, conv_ln_g, conv_ln_b, batch, tiles, tt, True, BF16, page_table, ck, (1, 1))
    lead = CONV_HALO - (CONV_WIDTH - 1)
    halo_s = jnp.pad(state_conv[0], ((0, 0), (lead, 0), (0, 0))).reshape(db * CONV_HALO, CONV_CH)
    cva_s, = _conv_module(halo_s, lambda b, i, pt: (b, 0), u_s,
                          cw, conv_b, conv_ln_g, conv_ln_b, db, 1, s_new, False, F32, page_table)

    mix_p, bsum_c = _gate_mix(xn_p, attn_p, cva_p, wg_b, b_gate, wa_b, wc_b, tm, page_table, ck, (2, 2))
    zero_cnt = jnp.zeros((1, LANES), F32)
    x1_p, xn2_p, rt_p, cnt_p = _out_proj(x_prompt.reshape(n_p, D_MODEL), mix_p, wo_b, norm2_g, wr_hi, wr_lo,
                                         br, zero_cnt, tm, page_table)

    bsum = jnp.concatenate([bsum_a, bsum_b, bsum_c], axis=0)
    k_new = k_s.reshape(db, s_new, N_HEADS, HEAD_DIM)
    sums = jnp.concatenate([bsum, jnp.sum(k_new, axis=1)[:, None]], axis=1)
    km_s = jnp.pad((sums / MOBA_BLOCK).transpose(0, 2, 1, 3), ((0, 0), (0, 0), (0, LANES - own_s - 1), (0, 0)))
    sel_idx = _sample_select(q_s, km_s, db, s_new, own_s)[..., :MOBA_TOPK]
    attn_s = _sample_attention(q_s, kc_s, vc_s, ck, cv, page_table, sel_idx, bias, far_bias, s_new, own_s)

    mix_s, = _gate_mix(xn_s, attn_s, cva_s, wg_b, b_gate, wa_b, wc_b, tm, page_table)
    x1_s, xn2_s, rt_s, cnt = _out_proj(x_sample.reshape(n_s, D_MODEL), mix_s, wo_b, norm2_g, wr_hi, wr_lo, br,
                                       cnt_p, tm, page_table)

    (dest_p, dest_s), block_e, n_used, n_rows, pads = _dispatch_plan((rt_p, rt_s), cnt)
    xb = _dispatch(xn2_p, xn2_s, jnp.concatenate([dest_p, dest_s]), *pads, n_rows, tm)
    yb = _moe_experts(xb, block_e, n_used, moe_w1[0], moe_w3[0], moe_w2[0])
    gf = norm_f_g[None, :]
    y_p = _combine(x1_p, rt_p, gf, yb, dest_p, tm)
    y_s = _combine(x1_s, rt_s, gf, yb, dest_s, tm)

    conv_prompt = u_p.reshape(batch, seq, CONV_CH)[:, seq - (CONV_WIDTH - 1):]
    conv_sample = jnp.concatenate([state_conv[0], u_s.reshape(db, s_new, CONV_CH)], axis=1)[:, -(CONV_WIDTH - 1):]
    shp_p = (1, batch, seq, N_HEADS, HEAD_DIM)
    shp_s = (1, db, s_new, N_HEADS, HEAD_DIM)
    return (y_p.reshape(batch, seq, D_MODEL), y_s.reshape(db, s_new, D_MODEL),
            k_p.reshape(shp_p), v_p.reshape(shp_p), conv_prompt[None],
            k_s.reshape(shp_s), v_s.reshape(shp_s), conv_sample[None])
```

```python
import functools
import math

import jax
import jax.numpy as jnp
from jax import lax
from jax.experimental import pallas as pl
from jax.experimental.pallas import tpu as pltpu

D_MODEL = 2048
HEAD_DIM = 128
N_HEADS = 8
ATTN_WIDTH = N_HEADS * HEAD_DIM
CONV_CH = 1024
CONV_WIDTH = 31
MOBA_BLOCK = 256
MOBA_TOPK = 3
NUM_BUCKETS = 32
MAX_DISTANCE = 128
N_GROUPS = 4
EXPERTS_PER_GROUP = 8
N_EXPERTS = N_GROUPS * EXPERTS_PER_GROUP
D_EXPERT = 512
EPS = 1e-6
NEG = -1e30

LANES = 128
SUBLANES = 8
CONV_HALO = 32
ROW_TILE = 256
PAGES_PER_CHUNK = 8
DMA_UNROLL = 8
MOE_ROWS = 256
KEY_CHUNK = 512
VMEM_LIMIT = 56 * 1024 * 1024

F32 = jnp.float32
BF16 = jnp.bfloat16
_NT = (((1,), (1,)), ((), ()))


def _params(sem, vmem=VMEM_LIMIT):
    return pltpu.CompilerParams(dimension_semantics=sem, vmem_limit_bytes=vmem)


def _sigmoid(x):
    return 1.0 / (1.0 + jnp.exp(-x))


def _split_bf16(x):
    hi = x.astype(BF16)
    return hi, (x - hi.astype(F32)).astype(BF16)


def _pack_bf16_pairs(x):
    c = x.shape[1] // 2
    lo = lax.bitcast_convert_type(x[:, :c].astype(BF16).astype(F32), jnp.uint32)
    hi = lax.bitcast_convert_type(x[:, c:].astype(BF16).astype(F32), jnp.uint32)
    return (lo >> 16) | (hi & jnp.uint32(0xFFFF0000))


def _unpack_bf16_pairs(p):
    lo = lax.bitcast_convert_type(p << 16, F32).astype(BF16)
    hi = lax.bitcast_convert_type(p & jnp.uint32(0xFFFF0000), F32).astype(BF16)
    return jnp.concatenate([lo, hi], axis=1)


def _resident(shape):
    return pl.BlockSpec(shape, lambda *_: (0,) * len(shape), pipeline_mode=pl.Buffered(1))


class _ScanRider:
    def __init__(self, pt_ref, ck_hbm, bsum_hbm, pgbuf, sumst, psem, osem, *, cps, chunk0, chunks_per_seq,
                 step=None, n_steps=None):
        self.pt_ref, self.ck_hbm, self.bsum_hbm = pt_ref, ck_hbm, bsum_hbm
        self.pgbuf, self.sumst, self.psem, self.osem = pgbuf, sumst, psem, osem
        self.cps, self.chunk0, self.chunks_per_seq = cps, chunk0, chunks_per_seq
        self.step = pl.program_id(0) if step is None else step
        self.n_steps = pl.num_programs(0) if n_steps is None else n_steps
        self.n_local = self.n_steps * cps
        self.bps = sumst.shape[1]

    def _chunk_copies(self, c_local, slot):
        c = self.chunk0 + c_local
        n_pages = self.chunks_per_seq * PAGES_PER_CHUNK
        p0 = (c // self.chunks_per_seq) * n_pages + (c % self.chunks_per_seq) * PAGES_PER_CHUNK
        return [pltpu.make_async_copy(self.ck_hbm.at[self.pt_ref[p0 + r]], self.pgbuf.at[slot, r],
                                      self.psem.at[slot]) for r in range(PAGES_PER_CHUNK)]

    def _sum_copy(self, c_local, k):
        dst = self.bsum_hbm.at[c_local // self.chunks_per_seq,
                               pl.ds((c_local % self.chunks_per_seq) * self.bps, self.bps)]
        return pltpu.make_async_copy(self.sumst.at[k], dst, self.osem)

    def prologue(self):
        @pl.when(self.step == 0)
        def _():
            for cp in self._chunk_copies(0, 0):
                cp.start()

    def phase(self, k):
        c_local = self.step * self.cps + k
        slot = k % 2
        for cp in self._chunk_copies(jnp.minimum(c_local + 1, self.n_local - 1), 1 - slot):
            cp.start()
        for cp in self._chunk_copies(c_local, slot):
            cp.wait()
        ppb = PAGES_PER_CHUNK // self.bps
        page_size = self.pgbuf.shape[2]

        def page_sum(p):
            page = self.pgbuf[slot, p].reshape(page_size // SUBLANES, SUBLANES, N_HEADS, HEAD_DIM)
            return jnp.sum(jnp.sum(page, axis=0), axis=0)

        for t in range(self.bps):
            tot = page_sum(t * ppb)
            for r in range(1, ppb):
                tot = tot + page_sum(t * ppb + r)
            self.sumst[k, t] = tot
        self._sum_copy(c_local, k).start()

    def finish(self):
        for k in range(self.cps):
            self._sum_copy(0, k).wait()

        @pl.when(self.step == self.n_steps - 1)
        def _():
            for cp in self._chunk_copies(self.n_local - 1, self.cps % 2):
                cp.wait()


def _scan_plan(page_table, cache_k, n_steps, part):
    db, n_pages = page_table.shape
    page_size = cache_k.shape[1]
    ppb = MOBA_BLOCK // page_size
    assert MOBA_BLOCK % page_size == 0 and PAGES_PER_CHUNK % ppb == 0 and n_pages % PAGES_PER_CHUNK == 0
    bps = PAGES_PER_CHUNK // ppb
    chunks_per_seq = n_pages // PAGES_PER_CHUNK
    n_chunks = db * chunks_per_seq
    assert n_chunks % 4 == 0
    chunk0, per_part = part[0] * (n_chunks // 4), part[1] * (n_chunks // 4)
    assert per_part % n_steps == 0 and chunk0 % chunks_per_seq == 0 and per_part % chunks_per_seq == 0
    cps = per_part // n_steps
    assert cps % 2 == 0
    cfg = dict(cps=cps, chunk0=chunk0, chunks_per_seq=chunks_per_seq)
    out_shape = jax.ShapeDtypeStruct((per_part // chunks_per_seq, n_pages // ppb, N_HEADS, HEAD_DIM), F32)
    scratch = [pltpu.VMEM((2, PAGES_PER_CHUNK, page_size, N_HEADS, HEAD_DIM), F32),
               pltpu.VMEM((cps, bps, N_HEADS, HEAD_DIM), F32),
               pltpu.SemaphoreType.DMA((2,)),
               pltpu.SemaphoreType.DMA(())]
    return cfg, out_shape, scratch


def _spread(n_items, n_slots):
    return [len(range(s, n_items, n_slots)) for s in range(n_slots)]


def _proj_kernel(pt_ref, x_ref, g_ref, w_ref, *rest, tm, scan):
    if scan:
        ck_hbm, rest = rest[0], rest[1:]
    xn_ref, q_ref, kc_ref, vc_ref, u_ref, ks_ref, k_hbm, v_hbm = rest[:8]
    rest = rest[8:]
    if scan:
        bsum_hbm, rest = rest[0], rest[1:]
    kst, vst, sem = rest[:3]
    rider = _ScanRider(pt_ref, ck_hbm, bsum_hbm, *rest[3:], **scan) if scan else None
    sub = iter(range(scan["cps"])) if scan else iter(())
    per_slot = _spread(scan["cps"], 4) if scan else [0] * 4

    def ride(slot_i):
        for _ in range(per_slot[slot_i]):
            rider.phase(next(sub))

    i = pl.program_id(0)
    aw = ATTN_WIDTH
    if scan:
        rider.prologue()

    def head_copies(src, dst_hbm, row_tile, s):
        r0 = pl.multiple_of(row_tile * tm, tm)
        return [pltpu.make_async_copy(src.at[:, h * HEAD_DIM:(h + 1) * HEAD_DIM],
                                      dst_hbm.at[pl.ds(r0, tm), h, :], sem.at[s]) for h in range(N_HEADS)]

    x = x_ref[...]
    xb = (x * lax.rsqrt(jnp.mean(x * x, axis=-1, keepdims=True) + EPS) * g_ref[...]).astype(BF16)
    xn_ref[...] = xb
    q_ref[...] = jnp.dot(xb, w_ref[:, 0:aw], preferred_element_type=F32)

    def emit(acc, stage, hbm, c_ref, s):
        @pl.when(i > 0)
        def _():
            for cp in head_copies(stage, hbm, i - 1, s):
                cp.wait()
        stage[...] = acc
        c_ref[...] = acc.astype(c_ref.dtype)
        for cp in head_copies(stage, hbm, i, s):
            cp.start()

    ride(0)
    k = jnp.dot(xb, w_ref[:, aw:2 * aw], preferred_element_type=F32)
    emit(k, kst, k_hbm, kc_ref, 0)
    ks_ref[0] = k.reshape(tm // MOBA_BLOCK, MOBA_BLOCK, aw).sum(axis=1)
    ride(1)
    emit(jnp.dot(xb, w_ref[:, 2 * aw:3 * aw], preferred_element_type=F32), vst, v_hbm, vc_ref, 1)
    ride(2)
    ua = jnp.dot(xb, w_ref[:, 3 * aw:3 * aw + CONV_CH], preferred_element_type=F32)
    ride(3)
    ug = jnp.dot(xb, w_ref[:, 3 * aw + CONV_CH:], preferred_element_type=F32)
    u_ref[...] = ua * _sigmoid(ug)
    if scan:
        rider.finish()

    @pl.when(i == pl.num_programs(0) - 1)
    def _():
        for cp in head_copies(kst, k_hbm, i, 0) + head_copies(vst, v_hbm, i, 1):
            cp.wait()


def _proj(x, g1, w_in, tm, kv_dtype, page_table, cache_k=None, scan_part=None):
    n = x.shape[0]
    nblk = tm // MOBA_BLOCK
    row = lambda i, pt: (i, 0)
    any_spec = pl.BlockSpec(memory_space=pl.ANY)
    scan, scan_out, scan_scratch = (None, None, [])
    if scan_part is not None:
        scan, scan_out, scan_scratch = _scan_plan(page_table, cache_k, n // tm, scan_part)
    grid_spec = pltpu.PrefetchScalarGridSpec(
        num_scalar_prefetch=1,
        grid=(n // tm,),
        in_specs=[pl.BlockSpec((tm, D_MODEL), row),
                  _resident((1, D_MODEL)),
                  _resident(w_in.shape)] + ([any_spec] if scan else []),
        out_specs=[pl.BlockSpec((tm, D_MODEL), row),
                   pl.BlockSpec((tm, ATTN_WIDTH), row),
                   pl.BlockSpec((tm, ATTN_WIDTH), row),
                   pl.BlockSpec((tm, ATTN_WIDTH), row),
                   pl.BlockSpec((tm, CONV_CH), row),
                   pl.BlockSpec((1, nblk, ATTN_WIDTH), lambda i, pt: (i, 0, 0)),
                   any_spec, any_spec] + ([any_spec] if scan else []),
        scratch_shapes=[pltpu.VMEM((tm, ATTN_WIDTH), F32),
                        pltpu.VMEM((tm, ATTN_WIDTH), F32),
                        pltpu.SemaphoreType.DMA((2,))] + scan_scratch)
    return pl.pallas_call(
        functools.partial(_proj_kernel, tm=tm, scan=scan),
        grid_spec=grid_spec,
        out_shape=[jax.ShapeDtypeStruct((n, D_MODEL), BF16),
                   jax.ShapeDtypeStruct((n, ATTN_WIDTH), F32),
                   jax.ShapeDtypeStruct((n, ATTN_WIDTH), kv_dtype),
                   jax.ShapeDtypeStruct((n, ATTN_WIDTH), kv_dtype),
                   jax.ShapeDtypeStruct((n, CONV_CH), F32),
                   jax.ShapeDtypeStruct((n // tm, nblk, ATTN_WIDTH), F32),
                   jax.ShapeDtypeStruct((n, N_HEADS, HEAD_DIM), F32),
                   jax.ShapeDtypeStruct((n, N_HEADS, HEAD_DIM), F32)] + ([scan_out] if scan else []),
        compiler_params=_params(("arbitrary",)),
        name="proj",
    )(page_table.reshape(-1), x, g1, w_in, *([cache_k] if scan else []))


def _t5_bucket_py(n):
    max_exact = NUM_BUCKETS // 2
    if n < max_exact:
        return n
    return min(max_exact + int(math.log(n / max_exact) / math.log(MAX_DISTANCE / max_exact)
                               * (NUM_BUCKETS - max_exact)), NUM_BUCKETS - 1)


assert _t5_bucket_py(MOBA_BLOCK + 1) == NUM_BUCKETS - 1


def _far_parts(far, shape):
    hi, lo = _split_bf16(jnp.full(shape, far, F32))
    return hi.astype(F32), lo.astype(F32)


def _bias_kernel(rb_ref, o_ref):
    h = pl.program_id(0)
    shape = (MOBA_BLOCK, MOBA_BLOCK)
    qi = lax.broadcasted_iota(jnp.int32, shape, 0)
    kj = lax.broadcasted_iota(jnp.int32, shape, 1)
    max_exact = NUM_BUCKETS // 2
    far_hi, far_lo = _far_parts(rb_ref[NUM_BUCKETS - 1, h], shape)
    for c in range(2):
        dist = qi - kj + c * MOBA_BLOCK
        nn = jnp.maximum(dist, 0)
        nf = jnp.maximum(nn, 1).astype(F32)
        large = max_exact + (jnp.log(nf / max_exact) / math.log(MAX_DISTANCE / max_exact)
                             * (NUM_BUCKETS - max_exact)).astype(jnp.int32)
        large = jnp.minimum(large, NUM_BUCKETS - 1)
        bkt = jnp.where(nn < max_exact, nn, large)
        val = jnp.zeros(shape, F32)
        for t in range(NUM_BUCKETS):
            val = jnp.where(bkt == t, rb_ref[t, h], val)
        val = val - (far_hi + far_lo)
        if c == 0:
            val = jnp.where(dist >= 0, val, NEG)
        o_ref[0, c] = val


def _bias_tiles(rel_bias):
    return pl.pallas_call(
        _bias_kernel,
        grid=(N_HEADS,),
        in_specs=[pl.BlockSpec(memory_space=pltpu.SMEM)],
        out_specs=pl.BlockSpec((1, 2, MOBA_BLOCK, MOBA_BLOCK), lambda h: (h, 0, 0, 0)),
        out_shape=jax.ShapeDtypeStruct((N_HEADS, 2, MOBA_BLOCK, MOBA_BLOCK), F32),
        compiler_params=_params(("arbitrary",)),
        name="bias_tiles",
    )(rel_bias)


def _attn_kernel(far_ref, q_ref, k_ref, v_ref, km_ref, bias_ref, o_ref, ka, va, qa, s_scr, p_scr, *, nb):
    h = pl.program_id(1)
    blk = MOBA_BLOCK
    seq = nb * blk
    nbp = -(-nb // SUBLANES) * SUBLANES

    row_blk = lax.broadcasted_iota(jnp.int32, (seq, LANES), 0) // blk
    col = lax.broadcasted_iota(jnp.int32, (seq, LANES), 1)
    ka[:, 0:HEAD_DIM] = k_ref[...]
    ka[:, HEAD_DIM:] = jnp.where((col == row_blk) | (col == nbp) | (col == nbp + 1), 1.0, 0.0).astype(BF16)
    va[:, 0:HEAD_DIM] = v_ref[...]
    va[:, HEAD_DIM:] = jnp.ones((seq, LANES), BF16)

    far_hi, far_lo = _far_parts(far_ref[h], (SUBLANES, blk))
    r8 = lax.broadcasted_iota(jnp.int32, (SUBLANES, blk), 0)
    far_rows = jnp.where(r8 == 0, far_hi, jnp.where(r8 == 1, far_lo, 0.0))
    rowi = lax.broadcasted_iota(jnp.int32, (nbp, blk), 0)
    km = km_ref[0, 0, 0:nbp, :]

    for qi in range(nb):
        par = qi % 2
        n = (qi + 1) * blk
        qt = q_ref[qi * blk:n, :]

        if qi > MOBA_TOPK:
            st = lax.dot_general(km, qt, _NT, precision=lax.Precision.HIGHEST, preferred_element_type=F32)
            rank = jnp.zeros((nbp, blk), jnp.int32)
            for i in range(qi):
                si = st[i:i + 1, :]
                rank = rank + ((si > st) | ((si == st) & (rowi > i))).astype(jnp.int32)
            dropped = (rowi < qi) & (rank >= MOBA_TOPK)
            selv = jnp.where(dropped, NEG, 0.0)
        else:
            selv = jnp.zeros((nbp, blk), F32)
        aug = jnp.concatenate([selv, far_rows, jnp.zeros((LANES - nbp - SUBLANES, blk), F32)], axis=0)
        qa[par, :, HEAD_DIM:] = aug.T.astype(BF16)
        qa[par, :, 0:HEAD_DIM] = (qt * (HEAD_DIM ** -0.5)).astype(BF16)

        m_acc = jnp.full((blk, LANES), NEG, F32)
        for c0 in range(0, n, KEY_CHUNK):
            w = min(KEY_CHUNK, n - c0)
            sc = lax.dot_general(qa[par], ka[c0:c0 + w, :], _NT, preferred_element_type=F32)
            pieces = []
            for t0 in range(0, w, blk):
                piece = sc[:, t0:t0 + blk]
                kb = (c0 + t0) // blk
                if kb == qi:
                    piece = piece + bias_ref[0, 0]
                elif kb == qi - 1:
                    piece = piece + bias_ref[0, 1]
                pieces.append(piece)
                for l0 in range(0, blk, LANES):
                    m_acc = jnp.maximum(m_acc, piece[:, l0:l0 + LANES])
            s_scr[par, :, c0:c0 + w] = pieces[0] if len(pieces) == 1 else jnp.concatenate(pieces, axis=1)
        m = jnp.max(m_acc, axis=1, keepdims=True)
        for c0 in range(0, n, KEY_CHUNK):
            w = min(KEY_CHUNK, n - c0)
            p_scr[par, :, c0:c0 + w] = jnp.exp(s_scr[par, :, c0:c0 + w] - m).astype(BF16)
        out = jnp.dot(p_scr[par, :, 0:n], va[0:n, :], preferred_element_type=F32)
        o_ref[qi * blk:n, :] = (out[:, 0:HEAD_DIM] / out[:, HEAD_DIM:HEAD_DIM + 1]).astype(o_ref.dtype)


def _prompt_attention(q, kc, vc, kmeans, bias, far_bias, batch, seq):
    nb = seq // MOBA_BLOCK
    nbp = -(-nb // SUBLANES) * SUBLANES
    assert nbp + SUBLANES <= LANES and (MOBA_BLOCK % KEY_CHUNK == 0 or KEY_CHUNK % MOBA_BLOCK == 0)
    seq_blk = lambda b, h: (b, h)
    return pl.pallas_call(
        functools.partial(_attn_kernel, nb=nb),
        grid=(batch, N_HEADS),
        in_specs=[pl.BlockSpec(memory_space=pltpu.SMEM),
                  pl.BlockSpec((seq, HEAD_DIM), seq_blk),
                  pl.BlockSpec((seq, HEAD_DIM), seq_blk),
                  pl.BlockSpec((seq, HEAD_DIM), seq_blk),
                  pl.BlockSpec((1, 1, LANES, HEAD_DIM), lambda b, h: (b, h, 0, 0)),
                  pl.BlockSpec((1, 2, MOBA_BLOCK, MOBA_BLOCK), lambda b, h: (h, 0, 0, 0))],
        out_specs=pl.BlockSpec((seq, HEAD_DIM), seq_blk),
        out_shape=jax.ShapeDtypeStruct((batch * seq, ATTN_WIDTH), BF16),
        scratch_shapes=[pltpu.VMEM((seq, 2 * HEAD_DIM), BF16),
                        pltpu.VMEM((seq, 2 * HEAD_DIM), BF16),
                        pltpu.VMEM((2, MOBA_BLOCK, 2 * HEAD_DIM), BF16),
                        pltpu.VMEM((2, MOBA_BLOCK, seq), F32),
                        pltpu.VMEM((2, MOBA_BLOCK, seq), BF16)],
        compiler_params=_params(("parallel", "parallel")),
        name="prompt_attn",
    )(far_bias, q, kc, vc, kmeans, bias)


def _sselect_kernel(q_ref, km_ref, o_ref, *, own):
    s_new = q_ref.shape[0]
    col = lax.broadcasted_iota(jnp.int32, (s_new, LANES), 1)
    colf = col.astype(F32)
    for h in range(N_HEADS):
        qh = q_ref[:, h * HEAD_DIM:(h + 1) * HEAD_DIM]
        s = lax.dot_general(qh, km_ref[0, h], _NT, precision=lax.Precision.HIGHEST, preferred_element_type=F32)
        s = jnp.where(col < own, s, NEG)
        out = jnp.zeros((s_new, LANES), F32)
        for t in range(MOBA_TOPK):
            mx = jnp.max(s, axis=1, keepdims=True)
            arg = jnp.min(jnp.where(s == mx, colf, float(LANES)), axis=1, keepdims=True)
            out = jnp.where(col == t, arg, out)
            s = jnp.where(colf == arg, -jnp.inf, s)
        o_ref[0, h] = out.astype(jnp.int32)


def _sample_select(q, kmeans_s, db, s_new, own):
    assert MOBA_TOPK <= own < LANES
    return pl.pallas_call(
        functools.partial(_sselect_kernel, own=own),
        grid=(db,),
        in_specs=[pl.BlockSpec((s_new, ATTN_WIDTH), lambda b: (b, 0)),
                  pl.BlockSpec((1, N_HEADS, LANES, HEAD_DIM), lambda b: (b, 0, 0, 0))],
        out_specs=pl.BlockSpec((1, N_HEADS, s_new, LANES), lambda b: (b, 0, 0, 0)),
        out_shape=jax.ShapeDtypeStruct((db, N_HEADS, s_new, LANES), jnp.int32),
        compiler_params=_params(("parallel",)),
        name="sample_select",
    )(q, kmeans_s)


def _sattn_kernel(pt_ref, idx_ref, far_ref, q_ref, kn_ref, vn_ref, bias_ref, ck_hbm, cv_hbm, o_ref,
                  kbuf, vbuf, sem, *, n_pages, page_size, own, s_new):
    b = pl.program_id(0)
    h = pl.program_id(1)
    nh = pl.num_programs(1)
    step = b * nh + h
    nsteps = pl.num_programs(0) * nh
    slot = step % 2
    ppb = MOBA_BLOCK // page_size
    nsel = s_new * MOBA_TOPK
    nkeys = nsel * MOBA_BLOCK

    def copies(step_, slot_):
        b_ = step_ // nh
        h_ = step_ % nh
        out = []
        for c in range(nsel):
            blk = idx_ref[step_ * nsel + c]
            for r in range(ppb):
                phys = pt_ref[b_ * n_pages + blk * ppb + r]
                dst = pl.ds((c * ppb + r) * page_size, page_size)
                out.append(pltpu.make_async_copy(ck_hbm.at[phys, :, h_, :], kbuf.at[slot_, dst, :], sem.at[slot_, 0]))
                out.append(pltpu.make_async_copy(cv_hbm.at[phys, :, h_, :], vbuf.at[slot_, dst, :], sem.at[slot_, 1]))
        return out

    @pl.when(step == 0)
    def _():
        for cp in copies(step, slot):
            cp.start()

    @pl.when(step + 1 < nsteps)
    def _():
        for cp in copies(step + 1, 1 - slot):
            cp.start()

    for cp in copies(step, slot):
        cp.wait()

    far_hi, far_lo = _far_parts(far_ref[h], (s_new, LANES))
    far = (far_hi + far_lo)[:, 0:1]
    qb = (q_ref[...] * (HEAD_DIM ** -0.5)).astype(BF16)
    kb = kbuf[slot].astype(BF16)
    vb = vbuf[slot].astype(BF16)
    lg = lax.dot_general(qb, kb, _NT, preferred_element_type=F32) + far
    pieces = []
    for c in range(nsel):
        is_prev = idx_ref[step * nsel + c] == own - 1
        row = bias_ref[0, 1, (c // MOBA_TOPK):(c // MOBA_TOPK) + 1, :]
        pieces.append(jnp.where(is_prev, row, 0.0))
    lg = lg + jnp.concatenate(pieces, axis=1)
    col = lax.broadcasted_iota(jnp.int32, (s_new, nkeys), 1)
    row_i = lax.broadcasted_iota(jnp.int32, (s_new, nkeys), 0)
    mine = (col >= row_i * (MOBA_TOPK * MOBA_BLOCK)) & (col < (row_i + 1) * (MOBA_TOPK * MOBA_BLOCK))
    lg = jnp.where(mine, lg, NEG)
    lg_own = (lax.dot_general(qb, kn_ref[...].astype(BF16), _NT, preferred_element_type=F32)
              + far + bias_ref[0, 0, 0:s_new, 0:s_new])
    m = jnp.maximum(jnp.max(lg, axis=1, keepdims=True), jnp.max(lg_own, axis=1, keepdims=True))
    p = jnp.exp(lg - m)
    p_own = jnp.exp(lg_own - m)
    denom = jnp.sum(p, axis=1, keepdims=True) + jnp.sum(p_own, axis=1, keepdims=True)
    acc = (jnp.dot(p.astype(BF16), vb, preferred_element_type=F32)
           + jnp.dot(p_own.astype(BF16), vn_ref[...].astype(BF16), preferred_element_type=F32))
    o_ref[...] = acc / denom


def _sample_attention(q, k, v, cache_k, cache_v, page_table, sel_idx, bias, far_bias, s_new, own):
    db, n_pages = page_table.shape
    page_size = cache_k.shape[1]
    nkeys = s_new * MOBA_TOPK * MOBA_BLOCK
    new_spec = pl.BlockSpec((s_new, HEAD_DIM), lambda b, h, pt, ix: (b, h))
    grid_spec = pltpu.PrefetchScalarGridSpec(
        num_scalar_prefetch=2,
        grid=(db, N_HEADS),
        in_specs=[pl.BlockSpec(memory_space=pltpu.SMEM),
                  new_spec, new_spec, new_spec,
                  pl.BlockSpec((1, 2, MOBA_BLOCK, MOBA_BLOCK), lambda b, h, pt, ix: (h, 0, 0, 0)),
                  pl.BlockSpec(memory_space=pl.ANY),
                  pl.BlockSpec(memory_space=pl.ANY)],
        out_specs=new_spec,
        scratch_shapes=[pltpu.VMEM((2, nkeys, HEAD_DIM), F32),
                        pltpu.VMEM((2, nkeys, HEAD_DIM), F32),
                        pltpu.SemaphoreType.DMA((2, 2))])
    return pl.pallas_call(
        functools.partial(_sattn_kernel, n_pages=n_pages, page_size=page_size, own=own, s_new=s_new),
        grid_spec=grid_spec,
        out_shape=jax.ShapeDtypeStruct((db * s_new, ATTN_WIDTH), F32),
        compiler_params=_params(("arbitrary", "arbitrary")),
        name="sample_attn",
    )(page_table.reshape(-1), sel_idx.reshape(-1), far_bias, q, k, v, bias, cache_k, cache_v)


def _conv_kernel(pt_ref, halo_ref, cur_ref, cw_ref, cb_ref, lg_ref, lb_ref, *rest, tt, rc, zero_first, scan):
    i = pl.program_id(1)
    if scan:
        ck_hbm, o_ref, bsum_hbm, ext_scr, sh = rest[:5]
        rider = _ScanRider(pt_ref, ck_hbm, bsum_hbm, *rest[5:], **scan,
                           step=pl.program_id(0) * pl.num_programs(1) + i,
                           n_steps=pl.num_programs(0) * pl.num_programs(1))
        rider.prologue()
    else:
        o_ref, ext_scr, sh = rest
    if zero_first:
        @pl.when(i == 0)
        def _():
            ext_scr[0:CONV_HALO, :] = jnp.zeros((CONV_HALO, CONV_CH), F32)

        @pl.when(i > 0)
        def _():
            ext_scr[0:CONV_HALO, :] = halo_ref[...]
    else:
        ext_scr[0:CONV_HALO, :] = halo_ref[...]
    ext_scr[CONV_HALO:CONV_HALO + tt, :] = cur_ref[...]
    lead = CONV_HALO - (CONV_WIDTH - 1)

    for r in range(SUBLANES):
        span = tt + CONV_HALO - r
        for s0 in range(0, span, LANES):
            rows = min(LANES, span - s0)
            sh[r, s0:s0 + rows, :] = ext_scr[r + s0:r + s0 + rows, :]

    def chunk(c, carry):
        r0 = pl.multiple_of(c * rc, rc)
        acc = jnp.zeros((rc, CONV_CH), F32)
        for w in range(CONV_WIDTH):
            r = (lead + w) % SUBLANES
            a0 = lead + w - r
            acc = acc + sh[r, pl.ds(pl.multiple_of(r0 + a0, SUBLANES), rc), :] * cw_ref[w:w + 1, :]
        hc = acc + cb_ref[...]
        mu = jnp.mean(hc, axis=-1, keepdims=True)
        var = jnp.mean(jnp.square(hc - mu), axis=-1, keepdims=True)
        y = (hc - mu) * lax.rsqrt(var + EPS) * lg_ref[...] + lb_ref[...]
        o_ref[pl.ds(r0, rc), :] = (y * _sigmoid(y)).astype(o_ref.dtype)
        return carry

    n_chunks = tt // rc
    if scan:
        cps = scan["cps"]
        assert n_chunks % cps == 0
        for k in range(cps):
            rider.phase(k)
            lax.fori_loop(k * (n_chunks // cps), (k + 1) * (n_chunks // cps), chunk, 0)
        rider.finish()
    else:
        lax.fori_loop(0, n_chunks, chunk, 0)


def _conv_module(halo_src, halo_map, cur_src, conv_w, conv_b, ln_g, ln_b, n_seq, n_tiles, tt,
                 zero_first, out_dtype, page_table, cache_k=None, scan_part=None):
    rc = min(tt, 32)
    vec = pl.BlockSpec((1, CONV_CH), lambda b, i, pt: (0, 0))
    cur_map = lambda b, i, pt: (b * n_tiles + i, 0)
    any_spec = pl.BlockSpec(memory_space=pl.ANY)
    scan, scan_out, scan_scratch = (None, None, [])
    if scan_part is not None:
        scan, scan_out, scan_scratch = _scan_plan(page_table, cache_k, n_seq * n_tiles, scan_part)
    grid_spec = pltpu.PrefetchScalarGridSpec(
        num_scalar_prefetch=1,
        grid=(n_seq, n_tiles),
        in_specs=[pl.BlockSpec((CONV_HALO, CONV_CH), halo_map),
                  pl.BlockSpec((tt, CONV_CH), cur_map),
                  pl.BlockSpec((CONV_WIDTH, CONV_CH), lambda b, i, pt: (0, 0)),
                  vec, vec, vec] + ([any_spec] if scan else []),
        out_specs=[pl.BlockSpec((tt, CONV_CH), cur_map)] + ([any_spec] if scan else []),
        scratch_shapes=[pltpu.VMEM((CONV_HALO + tt, CONV_CH), F32),
                        pltpu.VMEM((SUBLANES, tt + CONV_HALO, CONV_CH), F32)] + scan_scratch)
    return pl.pallas_call(
        functools.partial(_conv_kernel, tt=tt, rc=rc, zero_first=zero_first, scan=scan),
        grid_spec=grid_spec,
        out_shape=[jax.ShapeDtypeStruct((n_seq * n_tiles * tt, CONV_CH), out_dtype)] + ([scan_out] if scan else []),
        compiler_params=_params(("arbitrary", "arbitrary")),
        name="conv_module",
    )(page_table.reshape(-1), halo_src, cur_src, conv_w, conv_b, ln_g, ln_b, *([cache_k] if scan else []))


def _mix_kernel(pt_ref, xn_ref, at_ref, cv_ref, wg_ref, bg_ref, wa_ref, wc_ref, *rest, tn, scan):
    if scan:
        ck_hbm, o_ref, bsum_hbm = rest[:3]
        rider = _ScanRider(pt_ref, ck_hbm, bsum_hbm, *rest[3:], **scan)
        rider.prologue()
        per_slot = _spread(scan["cps"], D_MODEL // tn)
        sub = iter(range(scan["cps"]))
    else:
        o_ref = rest[0]
    xn = xn_ref[...]
    at = at_ref[...].astype(BF16)
    cv = cv_ref[...].astype(BF16)
    for c0 in range(0, D_MODEL, tn):
        if scan:
            for _ in range(per_slot[c0 // tn]):
                rider.phase(next(sub))
        a_cols = slice(c0, c0 + tn)
        c_cols = slice(D_MODEL + c0, D_MODEL + c0 + tn)
        ga = _sigmoid(jnp.dot(xn, wg_ref[:, a_cols], preferred_element_type=F32) + bg_ref[:, a_cols])
        gc = _sigmoid(jnp.dot(xn, wg_ref[:, c_cols], preferred_element_type=F32) + bg_ref[:, c_cols])
        ya = jnp.dot(at, wa_ref[:, a_cols], preferred_element_type=F32)
        yc = jnp.dot(cv, wc_ref[:, a_cols], preferred_element_type=F32)
        o_ref[:, a_cols] = (ga * ya + gc * yc).astype(o_ref.dtype)
    if scan:
        rider.finish()


def _gate_mix(xn, attn, cva, w_gate, b_gate, w_attn_out, w_conv_out, tm, page_table, cache_k=None,
              scan_part=None, tn=512):
    n = xn.shape[0]
    row = lambda i, pt: (i, 0)
    any_spec = pl.BlockSpec(memory_space=pl.ANY)
    scan, scan_out, scan_scratch = (None, None, [])
    if scan_part is not None:
        scan, scan_out, scan_scratch = _scan_plan(page_table, cache_k, n // tm, scan_part)
    grid_spec = pltpu.PrefetchScalarGridSpec(
        num_scalar_prefetch=1,
        grid=(n // tm,),
        in_specs=[pl.BlockSpec((tm, D_MODEL), row),
                  pl.BlockSpec((tm, ATTN_WIDTH), row),
                  pl.BlockSpec((tm, CONV_CH), row),
                  _resident(w_gate.shape), _resident(b_gate.shape),
                  _resident(w_attn_out.shape), _resident(w_conv_out.shape)] + ([any_spec] if scan else []),
        out_specs=[pl.BlockSpec((tm, D_MODEL), row)] + ([any_spec] if scan else []),
        scratch_shapes=scan_scratch)
    return pl.pallas_call(
        functools.partial(_mix_kernel, tn=tn, scan=scan),
        grid_spec=grid_spec,
        out_shape=[jax.ShapeDtypeStruct((n, D_MODEL), BF16)] + ([scan_out] if scan else []),
        compiler_params=_params(("arbitrary",)),
        name="gate_mix",
    )(page_table.reshape(-1), xn, attn, cva, w_gate, b_gate, w_attn_out, w_conv_out, *([cache_k] if scan else []))


def _outproj_kernel(pt_ref, x_ref, mix_ref, wo_ref, g2_ref, wrh_ref, wrl_ref, br_ref, cin_ref, *rest, scan):
    if scan:
        ck_hbm, x1_ref, xn2_ref, rt_ref, cout_ref, bsum_hbm, run = rest[:7]
        rider = _ScanRider(pt_ref, ck_hbm, bsum_hbm, *rest[7:], **scan)
        rider.prologue()
        first = -(-scan["cps"] // 2)
        for k in range(first):
            rider.phase(k)
    else:
        x1_ref, xn2_ref, rt_ref, cout_ref, run = rest

    @pl.when(pl.program_id(0) == 0)
    def _():
        run[...] = cin_ref[...]

    x1 = x_ref[...] + jnp.dot(mix_ref[...], wo_ref[...], preferred_element_type=F32)
    x1_ref[...] = x1
    xn2 = x1 * lax.rsqrt(jnp.mean(x1 * x1, axis=-1, keepdims=True) + EPS) * g2_ref[...]
    xn2_ref[...] = _pack_bf16_pairs(xn2)
    if scan:
        for k in range(first, scan["cps"]):
            rider.phase(k)
    hi, lo = _split_bf16(xn2)
    wrh = wrh_ref[...]
    logits = (jnp.dot(hi, wrh, preferred_element_type=F32) + jnp.dot(lo, wrh, preferred_element_type=F32)
              + jnp.dot(hi, wrl_ref[...], preferred_element_type=F32) + br_ref[...])
    tm = logits.shape[0]
    col = lax.broadcasted_iota(jnp.int32, (tm, LANES), 1).astype(F32)
    big = float(LANES)

    def first_argmax(vals):
        mx = jnp.max(vals, axis=1, keepdims=True)
        return mx, jnp.min(jnp.where(vals == mx, col, big), axis=1, keepdims=True)

    gmask = col < N_GROUPS
    gmax, gidx = first_argmax(jnp.where(gmask, logits, -jnp.inf))
    p_g = 1.0 / jnp.sum(jnp.where(gmask, jnp.exp(logits - gmax), 0.0), axis=1, keepdims=True)
    e_lo = N_GROUPS + gidx * EXPERTS_PER_GROUP
    el = jnp.where((col >= e_lo) & (col < e_lo + EXPERTS_PER_GROUP), logits, -jnp.inf)
    v1, i1 = first_argmax(el)
    v2, i2 = first_argmax(jnp.where(col == i1, -jnp.inf, el))
    e21 = jnp.exp(v2 - v1)
    p1 = 1.0 / (1.0 + e21)
    p2 = e21 / (1.0 + e21)
    e1 = i1 - N_GROUPS
    e2 = i2 - N_GROUPS

    oh1 = (col == e1).astype(F32)
    oh2 = (col == e2).astype(F32)
    both = oh1 + oh2
    ri = lax.broadcasted_iota(jnp.int32, (tm, tm), 0)
    ci = lax.broadcasted_iota(jnp.int32, (tm, tm), 1)
    lower = (ci < ri).astype(BF16)
    before = jnp.dot(lower, both.astype(BF16), preferred_element_type=F32) + run[...]
    pos1 = jnp.sum(before * oh1, axis=1, keepdims=True)
    pos2 = jnp.sum(before * oh2, axis=1, keepdims=True)
    new_run = run[...] + jnp.sum(both, axis=0, keepdims=True)
    run[...] = new_run
    cout_ref[...] = new_run

    rt = jnp.zeros((tm, LANES), F32)
    for lane, val in enumerate((e1, e2, p_g * p1, p_g * p2, pos1, pos2)):
        rt = jnp.where(col == lane, val, rt)
    rt_ref[...] = rt
    if scan:
        rider.finish()


def _out_proj(x, mix, w_out, g2, wr_hi, wr_lo, br, cnt_in, tm, page_table, cache_k=None, scan_part=None):
    n = x.shape[0]
    row = lambda i, pt: (i, 0)
    const = lambda i, pt: (0, 0)
    any_spec = pl.BlockSpec(memory_space=pl.ANY)
    scan, scan_out, scan_scratch = (None, None, [])
    if scan_part is not None:
        scan, scan_out, scan_scratch = _scan_plan(page_table, cache_k, n // tm, scan_part)
    grid_spec = pltpu.PrefetchScalarGridSpec(
        num_scalar_prefetch=1,
        grid=(n // tm,),
        in_specs=[pl.BlockSpec((tm, D_MODEL), row),
                  pl.BlockSpec((tm, D_MODEL), row),
                  _resident((D_MODEL, D_MODEL)),
                  _resident((1, D_MODEL)),
                  _resident((D_MODEL, LANES)),
                  _resident((D_MODEL, LANES)),
                  _resident((1, LANES)),
                  _resident((1, LANES))] + ([any_spec] if scan else []),
        out_specs=[pl.BlockSpec((tm, D_MODEL), row),
                   pl.BlockSpec((tm, D_MODEL // 2), row),
                   pl.BlockSpec((tm, LANES), row),
                   pl.BlockSpec((1, LANES), const)] + ([any_spec] if scan else []),
        scratch_shapes=[pltpu.VMEM((1, LANES), F32)] + scan_scratch)
    return pl.pallas_call(
        functools.partial(_outproj_kernel, scan=scan),
        grid_spec=grid_spec,
        out_shape=[jax.ShapeDtypeStruct((n, D_MODEL), F32),
                   jax.ShapeDtypeStruct((n, D_MODEL // 2), jnp.uint32),
                   jax.ShapeDtypeStruct((n, LANES), F32),
                   jax.ShapeDtypeStruct((1, LANES), F32)] + ([scan_out] if scan else []),
        compiler_params=_params(("arbitrary",)),
        name="out_proj",
    )(page_table.reshape(-1), x, mix, w_out, g2, wr_hi, wr_lo, br, cnt_in, *([cache_k] if scan else []))


def _dispatch_kernel(dest_ref, pad0_ref, padn_ref, tail_ref, xp_ref, xs_ref, xb_out, zbuf, sem, zsem, *,
                     tm, np_tiles, n_rows):
    i = pl.program_id(0)

    def issue_rows(x_ref):
        def row_copy(t, d):
            return pltpu.make_async_copy(x_ref.at[pl.ds(t, 1), :], xb_out.at[pl.ds(d, 1), :], sem)

        def issue(t, carry):
            base = (i * tm + t) * 2
            row_copy(t, dest_ref[base]).start()
            row_copy(t, dest_ref[base + 1]).start()
            return carry

        lax.fori_loop(0, tm, issue, 0, unroll=DMA_UNROLL)

    @pl.when(i < np_tiles)
    def _():
        issue_rows(xp_ref)

    @pl.when(i >= np_tiles)
    def _():
        issue_rows(xs_ref)

    @pl.when(i == 0)
    def _():
        zbuf[...] = jnp.zeros(zbuf.shape, zbuf.dtype)

        def zero_row(d):
            return pltpu.make_async_copy(zbuf.at[pl.ds(0, 1), :], xb_out.at[pl.ds(d, 1), :], zsem)

        def zero_blk(d):
            return pltpu.make_async_copy(zbuf, xb_out.at[pl.ds(d, MOE_ROWS), :], zsem)

        def each(n, fn):
            def body(r, carry):
                fn(r)
                return carry
            lax.fori_loop(0, n, body, 0)

        n_tail = (n_rows - tail_ref[0]) // MOE_ROWS
        for e in range(N_EXPERTS):
            each(padn_ref[e], lambda r, e=e: zero_row(pad0_ref[e] + r).start())
        each(n_tail, lambda j: zero_blk(pl.multiple_of(tail_ref[0] + j * MOE_ROWS, MOE_ROWS)).start())
        for e in range(N_EXPERTS):
            each(padn_ref[e], lambda r: zero_row(0).wait())
        each(n_tail, lambda j: zero_blk(0).wait())

    for _ in range(2):
        pltpu.make_async_copy(xp_ref, xb_out.at[pl.ds(0, tm), :], sem).wait()


def _dispatch(xn2_p, xn2_s, dest, pad0, padn, tail, n_rows, tm):
    np_tiles = xn2_p.shape[0] // tm
    ns_tiles = xn2_s.shape[0] // tm
    width = xn2_p.shape[1]
    grid_spec = pltpu.PrefetchScalarGridSpec(
        num_scalar_prefetch=4,
        grid=(np_tiles + ns_tiles,),
        in_specs=[pl.BlockSpec((tm, width), lambda i, *_: (jnp.minimum(i, np_tiles - 1), 0)),
                  pl.BlockSpec((tm, width), lambda i, *_: (jnp.maximum(i - np_tiles, 0), 0))],
        out_specs=pl.BlockSpec(memory_space=pl.ANY),
        scratch_shapes=[pltpu.VMEM((MOE_ROWS, width), xn2_p.dtype),
                        pltpu.SemaphoreType.DMA(()),
                        pltpu.SemaphoreType.DMA(())])
    return pl.pallas_call(
        functools.partial(_dispatch_kernel, tm=tm, np_tiles=np_tiles, n_rows=n_rows),
        grid_spec=grid_spec,
        out_shape=jax.ShapeDtypeStruct((n_rows, width), xn2_p.dtype),
        compiler_params=_params(("arbitrary",)),
        name="moe_dispatch",
    )(dest, pad0, padn, tail, xn2_p, xn2_s)


def _moe_kernel(be_ref, nu_ref, seg_ref, nxt_ref, x_ref, w1_hbm, w3_hbm, w2_hbm, o_ref,
                w1f, w3f, w2f, w1b, w3b, w2b, sem):
    g = pl.program_id(0)
    used = g < nu_ref[0]
    e = be_ref[g]
    new_expert = (g == 0) | (e != be_ref[jnp.maximum(g - 1, 0)])
    slot = seg_ref[g] % 2

    def fetch(expert, s):
        return [pltpu.make_async_copy(w1_hbm.at[expert], w1f.at[s], sem.at[s]),
                pltpu.make_async_copy(w3_hbm.at[expert], w3f.at[s], sem.at[s]),
                pltpu.make_async_copy(w2_hbm.at[expert], w2f.at[s], sem.at[s])]

    @pl.when(g == 0)
    def _():
        for cp in fetch(e, slot):
            cp.start()

    @pl.when(used & new_expert)
    def _():
        for cp in fetch(e, slot):
            cp.wait()
        w1b[...] = w1f[slot].astype(BF16)
        w3b[...] = w3f[slot].astype(BF16)
        w2b[...] = w2f[slot].astype(BF16)

        @pl.when(nxt_ref[e] < N_EXPERTS)
        def _():
            for cp in fetch(nxt_ref[e], 1 - slot):
                cp.start()

    @pl.when(used)
    def _():
        xb = _unpack_bf16_pairs(x_ref[...])
        a = jnp.dot(xb, w1b[...], preferred_element_type=F32)
        gt = jnp.dot(xb, w3b[...], preferred_element_type=F32)
        hmid = (a * _sigmoid(a) * gt).astype(BF16)
        o_ref[...] = jnp.dot(hmid, w2b[...], preferred_element_type=F32)

    @pl.when(jnp.logical_not(used))
    def _():
        o_ref[...] = jnp.zeros(o_ref.shape, F32)


def _moe_experts(xb, block_e, n_used, segment, next_expert, w1, w3, w2):
    n_blocks = xb.shape[0] // MOE_ROWS
    any_spec = pl.BlockSpec(memory_space=pl.ANY)
    grid_spec = pltpu.PrefetchScalarGridSpec(
        num_scalar_prefetch=4,
        grid=(n_blocks,),
        in_specs=[pl.BlockSpec((MOE_ROWS, xb.shape[1]), lambda g, be, nu, *_: (jnp.minimum(g, nu[0] - 1), 0)),
                  any_spec, any_spec, any_spec],
        out_specs=pl.BlockSpec((MOE_ROWS, D_MODEL), lambda g, *_: (g, 0)),
        scratch_shapes=[pltpu.VMEM((2, D_MODEL, D_EXPERT), F32),
                        pltpu.VMEM((2, D_MODEL, D_EXPERT), F32),
                        pltpu.VMEM((2, D_EXPERT, D_MODEL), F32),
                        pltpu.VMEM((D_MODEL, D_EXPERT), BF16),
                        pltpu.VMEM((D_MODEL, D_EXPERT), BF16),
                        pltpu.VMEM((D_EXPERT, D_MODEL), BF16),
                        pltpu.SemaphoreType.DMA((2,))])
    return pl.pallas_call(
        _moe_kernel,
        grid_spec=grid_spec,
        out_shape=jax.ShapeDtypeStruct((xb.shape[0], D_MODEL), F32),
        compiler_params=_params(("arbitrary",)),
        name="moe_experts",
    )(block_e, n_used, segment, next_expert, xb, w1, w3, w2)


def _combine_kernel(dest_ref, x1_ref, rt_ref, gf_ref, yb_hbm, o_ref, ybuf, sem, *, tm):
    i = pl.program_id(0)
    slot = i % 2

    def issue_tile(tile, s):
        def row_copy(k, t, d):
            return pltpu.make_async_copy(yb_hbm.at[pl.ds(d, 1), :], ybuf.at[s, k, pl.ds(t, 1), :], sem.at[s])

        def issue(t, carry):
            base = (tile * tm + t) * 2
            row_copy(0, t, dest_ref[base]).start()
            row_copy(1, t, dest_ref[base + 1]).start()
            return carry

        lax.fori_loop(0, tm, issue, 0, unroll=DMA_UNROLL)

    @pl.when(i == 0)
    def _():
        issue_tile(i, slot)

    @pl.when(i + 1 < pl.num_programs(0))
    def _():
        issue_tile(i + 1, 1 - slot)

    for k in range(2):
        pltpu.make_async_copy(yb_hbm.at[pl.ds(0, tm), :], ybuf.at[slot, k], sem.at[slot]).wait()
    rt = rt_ref[...]
    moe = ybuf[slot, 0] * rt[:, 2:3] + ybuf[slot, 1] * rt[:, 3:4]
    xo = x1_ref[...] + moe
    o_ref[...] = xo * lax.rsqrt(jnp.mean(xo * xo, axis=-1, keepdims=True) + EPS) * gf_ref[...]


def _combine(x1, route, gf, yb, dest, tm):
    n = x1.shape[0]
    grid_spec = pltpu.PrefetchScalarGridSpec(
        num_scalar_prefetch=1,
        grid=(n // tm,),
        in_specs=[pl.BlockSpec((tm, D_MODEL), lambda i, d: (i, 0)),
                  pl.BlockSpec((tm, LANES), lambda i, d: (i, 0)),
                  pl.BlockSpec((1, D_MODEL), lambda i, d: (0, 0)),
                  pl.BlockSpec(memory_space=pl.ANY)],
        out_specs=pl.BlockSpec((tm, D_MODEL), lambda i, d: (i, 0)),
        scratch_shapes=[pltpu.VMEM((2, 2, tm, D_MODEL), F32),
                        pltpu.SemaphoreType.DMA((2,))])
    return pl.pallas_call(
        functools.partial(_combine_kernel, tm=tm),
        grid_spec=grid_spec,
        out_shape=jax.ShapeDtypeStruct((n, D_MODEL), F32),
        compiler_params=_params(("arbitrary",)),
        name="moe_combine",
    )(dest, x1, route, gf, yb)


def _dispatch_plan(routes, counts):
    counts = counts[0, :N_EXPERTS].astype(jnp.int32)
    pcounts = (counts + MOE_ROWS - 1) // MOE_ROWS * MOE_ROWS
    pend = jnp.cumsum(pcounts)
    pstart = pend - pcounts
    dests = []
    n_assign = 0
    for rt in routes:
        e = rt[:, 0:2].astype(jnp.int32)
        pos = rt[:, 4:6].astype(jnp.int32)
        onehot = e[:, :, None] == jnp.arange(N_EXPERTS, dtype=jnp.int32)[None, None, :]
        dests.append((jnp.sum(jnp.where(onehot, pstart[None, None, :], 0), axis=-1) + pos).reshape(-1))
        n_assign += e.size
    n_blocks = (n_assign + N_EXPERTS * (MOE_ROWS - 1) + MOE_ROWS - 1) // MOE_ROWS
    first_row = jnp.arange(n_blocks, dtype=jnp.int32) * MOE_ROWS
    block_e = jnp.minimum(jnp.sum((pend[None, :] <= first_row[:, None]).astype(jnp.int32), axis=1), N_EXPERTS - 1)
    n_used = (pend[-1] // MOE_ROWS).astype(jnp.int32).reshape(1)
    pads = (pstart + counts, pcounts - counts, pend[-1:])
    changed = jnp.concatenate([jnp.zeros((1,), jnp.int32), (block_e[1:] != block_e[:-1]).astype(jnp.int32)])
    segment = jnp.cumsum(changed).astype(jnp.int32)
    ids = jnp.arange(N_EXPERTS, dtype=jnp.int32)
    later_nonempty = (ids[None, :] > ids[:, None]) & (counts[None, :] > 0)
    next_expert = jnp.min(jnp.where(later_nonempty, ids[None, :], N_EXPERTS), axis=1).astype(jnp.int32)
    return dests, block_e.astype(jnp.int32), n_used, n_blocks * MOE_ROWS, pads, segment, next_expert


def kernel(x_prompt, x_sample, cache_k, cache_v, state_conv, page_table, rel_bias, norm1_g, w_in, w_attn_out,
           conv_w, conv_b, conv_ln_g, conv_ln_b, w_conv_out, w_gate, b_gate, w_out, norm2_g, router_g_w,
           router_g_b, router_e_w, router_e_b, moe_w1, moe_w3, moe_w2, norm_f_g):
    depth = norm1_g.shape[0]
    assert depth == 1
    batch, seq, _ = x_prompt.shape
    db, s_new, _ = x_sample.shape
    n_pages = page_table.shape[1]
    page_size = cache_k.shape[2]
    past_len = n_pages * page_size
    assert seq % MOBA_BLOCK == 0 and past_len % MOBA_BLOCK == 0 and s_new <= MOBA_BLOCK and s_new % SUBLANES == 0
    own_s = past_len // MOBA_BLOCK
    n_p = batch * seq
    n_s = db * s_new
    tm = ROW_TILE
    assert n_p % tm == 0 and n_s % tm == 0

    wi_b = w_in[0].astype(BF16)
    wr = jnp.concatenate([router_g_w[0], router_e_w[0]], axis=1)
    wr = jnp.pad(wr, ((0, 0), (0, LANES - wr.shape[1])))
    wr_hi, wr_lo = _split_bf16(wr)
    br = jnp.pad(jnp.concatenate([router_g_b[0], router_e_b[0]]), (0, LANES - N_GROUPS - N_EXPERTS))[None, :]
    far_bias = rel_bias[NUM_BUCKETS - 1, :]
    wg_b, wa_b, wc_b, wo_b = (w[0].astype(BF16) for w in (w_gate, w_attn_out, w_conv_out, w_out))
    cw = conv_w[0]
    bias = _bias_tiles(rel_bias)

    ck, cv = cache_k[0], cache_v[0]
    xn_p, q_p, kc_p, vc_p, u_p, ksum, k_p, v_p, bsum_a = _proj(
        x_prompt.reshape(n_p, D_MODEL), norm1_g, wi_b, tm, BF16, page_table, ck, (0, 1))
    xn_s, q_s, kc_s, vc_s, u_s, _, k_s, v_s = _proj(x_sample.reshape(n_s, D_MODEL), norm1_g, wi_b, tm, F32, page_table)

    nb = seq // MOBA_BLOCK
    km = ksum.reshape(batch, nb, N_HEADS, HEAD_DIM) / MOBA_BLOCK
    km = jnp.pad(km.transpose(0, 2, 1, 3), ((0, 0), (0, 0), (0, LANES - nb), (0, 0)))
    attn_p = _prompt_attention(q_p, kc_p, vc_p, km, bias, far_bias, batch, seq)

    tt = 512
    tiles = seq // tt
    per_halo = tt // CONV_HALO
    cva_p, bsum_b = _conv_module(u_p, lambda b, i, pt: (jnp.maximum((b * tiles + i) * per_halo - 1, 0), 0), u_p,
                                 cw, conv_b, conv_ln_g, conv_ln_b, batch, tiles, tt, True, BF16, page_table, ck, (1, 1))
    lead = CONV_HALO - (CONV_WIDTH - 1)
    halo_s = jnp.pad(state_conv[0], ((0, 0), (lead, 0), (0, 0))).reshape(db * CONV_HALO, CONV_CH)
    cva_s, = _conv_module(halo_s, lambda b, i, pt: (b, 0), u_s,
                          cw, conv_b, conv_ln_g, conv_ln_b, db, 1, s_new, False, F32, page_table)

    mix_p, bsum_c = _gate_mix(xn_p, attn_p, cva_p, wg_b, b_gate, wa_b, wc_b, tm, page_table, ck, (2, 2))
    zero_cnt = jnp.zeros((1, LANES), F32)
    x1_p, xn2_p, rt_p, cnt_p = _out_proj(x_prompt.reshape(n_p, D_MODEL), mix_p, wo_b, norm2_g, wr_hi, wr_lo,
                                         br, zero_cnt, tm, page_table)

    bsum = jnp.concatenate([bsum_a, bsum_b, bsum_c], axis=0)
    k_new = k_s.reshape(db, s_new, N_HEADS, HEAD_DIM)
    sums = jnp.concatenate([bsum, jnp.sum(k_new, axis=1)[:, None]], axis=1)
    km_s = jnp.pad((sums / MOBA_BLOCK).transpose(0, 2, 1, 3), ((0, 0), (0, 0), (0, LANES - own_s - 1), (0, 0)))
    sel_idx = _sample_select(q_s, km_s, db, s_new, own_s)[..., :MOBA_TOPK]
    attn_s = _sample_attention(q_s, kc_s, vc_s, ck, cv, page_table, sel_idx, bias, far_bias, s_new, own_s)

    mix_s, = _gate_mix(xn_s, attn_s, cva_s, wg_b, b_gate, wa_b, wc_b, tm, page_table)
    x1_s, xn2_s, rt_s, cnt = _out_proj(x_sample.reshape(n_s, D_MODEL), mix_s, wo_b, norm2_g, wr_hi, wr_lo, br,
                                       cnt_p, tm, page_table)

    (dest_p, dest_s), block_e, n_used, n_rows, pads, segment, next_expert = _dispatch_plan((rt_p, rt_s), cnt)
    xb = _dispatch(xn2_p, xn2_s, jnp.concatenate([dest_p, dest_s]), *pads, n_rows, tm)
    yb = _moe_experts(xb, block_e, n_used, segment, next_expert, moe_w1[0], moe_w3[0], moe_w2[0])
    gf = norm_f_g[None, :]
    y_p = _combine(x1_p, rt_p, gf, yb, dest_p, tm)
    y_s = _combine(x1_s, rt_s, gf, yb, dest_s, tm)

    conv_prompt = u_p.reshape(batch, seq, CONV_CH)[:, seq - (CONV_WIDTH - 1):]
    conv_sample = jnp.concatenate([state_conv[0], u_s.reshape(db, s_new, CONV_CH)], axis=1)[:, -(CONV_WIDTH - 1):]
    shp_p = (1, batch, seq, N_HEADS, HEAD_DIM)
    shp_s = (1, db, s_new, N_HEADS, HEAD_DIM)
    return (y_p.reshape(batch, seq, D_MODEL), y_s.reshape(db, s_new, D_MODEL),
            k_p.reshape(shp_p), v_p.reshape(shp_p), conv_prompt[None],
            k_s.reshape(shp_s), v_s.reshape(shp_s), conv_sample[None])
```

```python
import functools
import math

import jax
import jax.numpy as jnp
from jax import lax
from jax.experimental import pallas as pl
from jax.experimental.pallas import tpu as pltpu

D_MODEL = 2048
HEAD_DIM = 128
N_HEADS = 8
ATTN_WIDTH = N_HEADS * HEAD_DIM
CONV_CH = 1024
CONV_WIDTH = 31
MOBA_BLOCK = 256
MOBA_TOPK = 3
NUM_BUCKETS = 32
MAX_DISTANCE = 128
N_GROUPS = 4
EXPERTS_PER_GROUP = 8
N_EXPERTS = N_GROUPS * EXPERTS_PER_GROUP
D_EXPERT = 512
EPS = 1e-6
NEG = -1e30

LANES = 128
SUBLANES = 8
CONV_HALO = 32
ROW_TILE = 256
PAGES_PER_CHUNK = 8
DMA_UNROLL = 8
MOE_ROWS = 256
KEY_CHUNK = 512
VMEM_LIMIT = 56 * 1024 * 1024

F32 = jnp.float32
BF16 = jnp.bfloat16
_NT = (((1,), (1,)), ((), ()))


def _params(sem, vmem=VMEM_LIMIT):
    return pltpu.CompilerParams(dimension_semantics=sem, vmem_limit_bytes=vmem)


def _sigmoid(x):
    return 1.0 / (1.0 + jnp.exp(-x))


def _split_bf16(x):
    hi = x.astype(BF16)
    return hi, (x - hi.astype(F32)).astype(BF16)


def _pack_bf16_pairs(x):
    c = x.shape[1] // 2
    lo = lax.bitcast_convert_type(x[:, :c].astype(BF16).astype(F32), jnp.uint32)
    hi = lax.bitcast_convert_type(x[:, c:].astype(BF16).astype(F32), jnp.uint32)
    return (lo >> 16) | (hi & jnp.uint32(0xFFFF0000))


def _unpack_bf16_pairs(p):
    lo = lax.bitcast_convert_type(p << 16, F32).astype(BF16)
    hi = lax.bitcast_convert_type(p & jnp.uint32(0xFFFF0000), F32).astype(BF16)
    return jnp.concatenate([lo, hi], axis=1)


def _resident(shape):
    return pl.BlockSpec(shape, lambda *_: (0,) * len(shape), pipeline_mode=pl.Buffered(1))


class _ScanRider:
    def __init__(self, pt_ref, ck_hbm, bsum_hbm, pgbuf, sumst, psem, osem, *, cps, chunk0, chunks_per_seq,
                 step=None, n_steps=None):
        self.pt_ref, self.ck_hbm, self.bsum_hbm = pt_ref, ck_hbm, bsum_hbm
        self.pgbuf, self.sumst, self.psem, self.osem = pgbuf, sumst, psem, osem
        self.cps, self.chunk0, self.chunks_per_seq = cps, chunk0, chunks_per_seq
        self.step = pl.program_id(0) if step is None else step
        self.n_steps = pl.num_programs(0) if n_steps is None else n_steps
        self.n_local = self.n_steps * cps
        self.bps = sumst.shape[1]

    def _chunk_copies(self, c_local, slot):
        c = self.chunk0 + c_local
        n_pages = self.chunks_per_seq * PAGES_PER_CHUNK
        p0 = (c // self.chunks_per_seq) * n_pages + (c % self.chunks_per_seq) * PAGES_PER_CHUNK
        return [pltpu.make_async_copy(self.ck_hbm.at[self.pt_ref[p0 + r]], self.pgbuf.at[slot, r],
                                      self.psem.at[slot]) for r in range(PAGES_PER_CHUNK)]

    def _sum_copy(self, c_local, k):
        dst = self.bsum_hbm.at[c_local // self.chunks_per_seq,
                               pl.ds((c_local % self.chunks_per_seq) * self.bps, self.bps)]
        return pltpu.make_async_copy(self.sumst.at[k], dst, self.osem)

    def prologue(self):
        @pl.when(self.step == 0)
        def _():
            for cp in self._chunk_copies(0, 0):
                cp.start()

    def phase(self, k):
        c_local = self.step * self.cps + k
        slot = k % 2
        for cp in self._chunk_copies(jnp.minimum(c_local + 1, self.n_local - 1), 1 - slot):
            cp.start()
        for cp in self._chunk_copies(c_local, slot):
            cp.wait()
        ppb = PAGES_PER_CHUNK // self.bps
        page_size = self.pgbuf.shape[2]

        def page_sum(p):
            page = self.pgbuf[slot, p].reshape(page_size // SUBLANES, SUBLANES, N_HEADS, HEAD_DIM)
            return jnp.sum(jnp.sum(page, axis=0), axis=0)

        for t in range(self.bps):
            tot = page_sum(t * ppb)
            for r in range(1, ppb):
                tot = tot + page_sum(t * ppb + r)
            self.sumst[k, t] = tot
        self._sum_copy(c_local, k).start()

    def finish(self):
        for k in range(self.cps):
            self._sum_copy(0, k).wait()

        @pl.when(self.step == self.n_steps - 1)
        def _():
            for cp in self._chunk_copies(self.n_local - 1, self.cps % 2):
                cp.wait()


def _scan_plan(page_table, cache_k, n_steps, part):
    db, n_pages = page_table.shape
    page_size = cache_k.shape[1]
    ppb = MOBA_BLOCK // page_size
    assert MOBA_BLOCK % page_size == 0 and PAGES_PER_CHUNK % ppb == 0 and n_pages % PAGES_PER_CHUNK == 0
    bps = PAGES_PER_CHUNK // ppb
    chunks_per_seq = n_pages // PAGES_PER_CHUNK
    n_chunks = db * chunks_per_seq
    assert n_chunks % 4 == 0
    chunk0, per_part = part[0] * (n_chunks // 4), part[1] * (n_chunks // 4)
    assert per_part % n_steps == 0 and chunk0 % chunks_per_seq == 0 and per_part % chunks_per_seq == 0
    cps = per_part // n_steps
    assert cps % 2 == 0
    cfg = dict(cps=cps, chunk0=chunk0, chunks_per_seq=chunks_per_seq)
    out_shape = jax.ShapeDtypeStruct((per_part // chunks_per_seq, n_pages // ppb, N_HEADS, HEAD_DIM), F32)
    scratch = [pltpu.VMEM((2, PAGES_PER_CHUNK, page_size, N_HEADS, HEAD_DIM), F32),
               pltpu.VMEM((cps, bps, N_HEADS, HEAD_DIM), F32),
               pltpu.SemaphoreType.DMA((2,)),
               pltpu.SemaphoreType.DMA(())]
    return cfg, out_shape, scratch


def _spread(n_items, n_slots):
    return [len(range(s, n_items, n_slots)) for s in range(n_slots)]


def _proj_kernel(pt_ref, x_ref, g_ref, w_ref, *rest, tm, scan):
    if scan:
        ck_hbm, rest = rest[0], rest[1:]
    xn_ref, q_ref, kc_ref, vc_ref, u_ref, ks_ref, k3_ref, v3_ref = rest[:8]
    rest = rest[8:]
    if scan:
        bsum_hbm, rest = rest[0], rest[1:]
    rider = _ScanRider(pt_ref, ck_hbm, bsum_hbm, *rest, **scan) if scan else None
    sub = iter(range(scan["cps"])) if scan else iter(())
    per_slot = _spread(scan["cps"], 4) if scan else [0] * 4

    def ride(slot_i):
        for _ in range(per_slot[slot_i]):
            rider.phase(next(sub))

    aw = ATTN_WIDTH
    if scan:
        rider.prologue()

    x = x_ref[...]
    xb = (x * lax.rsqrt(jnp.mean(x * x, axis=-1, keepdims=True) + EPS) * g_ref[...]).astype(BF16)
    xn_ref[...] = xb
    q_ref[...] = jnp.dot(xb, w_ref[:, 0:aw], preferred_element_type=F32)

    def emit(acc, c_ref, out3_ref):
        c_ref[...] = acc.astype(c_ref.dtype)
        out3_ref[...] = pltpu.einshape("t(hd)->thd", acc, h=N_HEADS)

    ride(0)
    k = jnp.dot(xb, w_ref[:, aw:2 * aw], preferred_element_type=F32)
    emit(k, kc_ref, k3_ref)
    ks_ref[0] = k.reshape(tm // MOBA_BLOCK, MOBA_BLOCK, aw).sum(axis=1)
    ride(1)
    emit(jnp.dot(xb, w_ref[:, 2 * aw:3 * aw], preferred_element_type=F32), vc_ref, v3_ref)
    ride(2)
    ua = jnp.dot(xb, w_ref[:, 3 * aw:3 * aw + CONV_CH], preferred_element_type=F32)
    ride(3)
    ug = jnp.dot(xb, w_ref[:, 3 * aw + CONV_CH:], preferred_element_type=F32)
    u_ref[...] = ua * _sigmoid(ug)
    if scan:
        rider.finish()


def _proj(x, g1, w_in, tm, kv_dtype, page_table, cache_k=None, scan_part=None):
    n = x.shape[0]
    nblk = tm // MOBA_BLOCK
    row = lambda i, pt: (i, 0)
    any_spec = pl.BlockSpec(memory_space=pl.ANY)
    scan, scan_out, scan_scratch = (None, None, [])
    if scan_part is not None:
        scan, scan_out, scan_scratch = _scan_plan(page_table, cache_k, n // tm, scan_part)
    grid_spec = pltpu.PrefetchScalarGridSpec(
        num_scalar_prefetch=1,
        grid=(n // tm,),
        in_specs=[pl.BlockSpec((tm, D_MODEL), row),
                  _resident((1, D_MODEL)),
                  _resident(w_in.shape)] + ([any_spec] if scan else []),
        out_specs=[pl.BlockSpec((tm, D_MODEL), row),
                   pl.BlockSpec((tm, ATTN_WIDTH), row),
                   pl.BlockSpec((tm, ATTN_WIDTH), row),
                   pl.BlockSpec((tm, ATTN_WIDTH), row),
                   pl.BlockSpec((tm, CONV_CH), row),
                   pl.BlockSpec((1, nblk, ATTN_WIDTH), lambda i, pt: (i, 0, 0)),
                   pl.BlockSpec((tm, N_HEADS, HEAD_DIM), lambda i, pt: (i, 0, 0)),
                   pl.BlockSpec((tm, N_HEADS, HEAD_DIM), lambda i, pt: (i, 0, 0))] + ([any_spec] if scan else []),
        scratch_shapes=scan_scratch)
    return pl.pallas_call(
        functools.partial(_proj_kernel, tm=tm, scan=scan),
        grid_spec=grid_spec,
        out_shape=[jax.ShapeDtypeStruct((n, D_MODEL), BF16),
                   jax.ShapeDtypeStruct((n, ATTN_WIDTH), F32),
                   jax.ShapeDtypeStruct((n, ATTN_WIDTH), kv_dtype),
                   jax.ShapeDtypeStruct((n, ATTN_WIDTH), kv_dtype),
                   jax.ShapeDtypeStruct((n, CONV_CH), F32),
                   jax.ShapeDtypeStruct((n // tm, nblk, ATTN_WIDTH), F32),
                   jax.ShapeDtypeStruct((n, N_HEADS, HEAD_DIM), F32),
                   jax.ShapeDtypeStruct((n, N_HEADS, HEAD_DIM), F32)] + ([scan_out] if scan else []),
        compiler_params=_params(("arbitrary",)),
        name="proj",
    )(page_table.reshape(-1), x, g1, w_in, *([cache_k] if scan else []))


def _t5_bucket_py(n):
    max_exact = NUM_BUCKETS // 2
    if n < max_exact:
        return n
    return min(max_exact + int(math.log(n / max_exact) / math.log(MAX_DISTANCE / max_exact)
                               * (NUM_BUCKETS - max_exact)), NUM_BUCKETS - 1)


assert _t5_bucket_py(MOBA_BLOCK + 1) == NUM_BUCKETS - 1


def _far_parts(far, shape):
    hi, lo = _split_bf16(jnp.full(shape, far, F32))
    return hi.astype(F32), lo.astype(F32)


def _bias_kernel(rb_ref, o_ref):
    h = pl.program_id(0)
    shape = (MOBA_BLOCK, MOBA_BLOCK)
    qi = lax.broadcasted_iota(jnp.int32, shape, 0)
    kj = lax.broadcasted_iota(jnp.int32, shape, 1)
    max_exact = NUM_BUCKETS // 2
    far_hi, far_lo = _far_parts(rb_ref[NUM_BUCKETS - 1, h], shape)
    for c in range(2):
        dist = qi - kj + c * MOBA_BLOCK
        nn = jnp.maximum(dist, 0)
        nf = jnp.maximum(nn, 1).astype(F32)
        large = max_exact + (jnp.log(nf / max_exact) / math.log(MAX_DISTANCE / max_exact)
                             * (NUM_BUCKETS - max_exact)).astype(jnp.int32)
        large = jnp.minimum(large, NUM_BUCKETS - 1)
        bkt = jnp.where(nn < max_exact, nn, large)
        val = jnp.zeros(shape, F32)
        for t in range(NUM_BUCKETS):
            val = jnp.where(bkt == t, rb_ref[t, h], val)
        val = val - (far_hi + far_lo)
        if c == 0:
            val = jnp.where(dist >= 0, val, NEG)
        o_ref[0, c] = val


def _bias_tiles(rel_bias):
    return pl.pallas_call(
        _bias_kernel,
        grid=(N_HEADS,),
        in_specs=[pl.BlockSpec(memory_space=pltpu.SMEM)],
        out_specs=pl.BlockSpec((1, 2, MOBA_BLOCK, MOBA_BLOCK), lambda h: (h, 0, 0, 0)),
        out_shape=jax.ShapeDtypeStruct((N_HEADS, 2, MOBA_BLOCK, MOBA_BLOCK), F32),
        compiler_params=_params(("arbitrary",)),
        name="bias_tiles",
    )(rel_bias)


def _attn_kernel(far_ref, q_ref, k_ref, v_ref, km_ref, bias_ref, o_ref, ka, va, qa, s_scr, p_scr, *, nb):
    h = pl.program_id(1)
    blk = MOBA_BLOCK
    seq = nb * blk
    nbp = -(-nb // SUBLANES) * SUBLANES

    row_blk = lax.broadcasted_iota(jnp.int32, (seq, LANES), 0) // blk
    col = lax.broadcasted_iota(jnp.int32, (seq, LANES), 1)
    ka[:, 0:HEAD_DIM] = k_ref[...]
    ka[:, HEAD_DIM:] = jnp.where((col == row_blk) | (col == nbp) | (col == nbp + 1), 1.0, 0.0).astype(BF16)
    va[:, 0:HEAD_DIM] = v_ref[...]
    va[:, HEAD_DIM:] = jnp.ones((seq, LANES), BF16)

    far_hi, far_lo = _far_parts(far_ref[h], (SUBLANES, blk))
    r8 = lax.broadcasted_iota(jnp.int32, (SUBLANES, blk), 0)
    far_rows = jnp.where(r8 == 0, far_hi, jnp.where(r8 == 1, far_lo, 0.0))
    rowi = lax.broadcasted_iota(jnp.int32, (nbp, blk), 0)
    km = km_ref[0, 0, 0:nbp, :]

    for qi in range(nb):
        par = qi % 2
        n = (qi + 1) * blk
        qt = q_ref[qi * blk:n, :]

        if qi > MOBA_TOPK:
            st = lax.dot_general(km, qt, _NT, precision=lax.Precision.HIGHEST, preferred_element_type=F32)
            rank = jnp.zeros((nbp, blk), jnp.int32)
            for i in range(qi):
                si = st[i:i + 1, :]
                rank = rank + ((si > st) | ((si == st) & (rowi > i))).astype(jnp.int32)
            dropped = (rowi < qi) & (rank >= MOBA_TOPK)
            selv = jnp.where(dropped, NEG, 0.0)
        else:
            selv = jnp.zeros((nbp, blk), F32)
        aug = jnp.concatenate([selv, far_rows, jnp.zeros((LANES - nbp - SUBLANES, blk), F32)], axis=0)
        qa[par, :, HEAD_DIM:] = aug.T.astype(BF16)
        qa[par, :, 0:HEAD_DIM] = (qt * (HEAD_DIM ** -0.5)).astype(BF16)

        m_acc = jnp.full((blk, LANES), NEG, F32)
        for c0 in range(0, n, KEY_CHUNK):
            w = min(KEY_CHUNK, n - c0)
            sc = lax.dot_general(qa[par], ka[c0:c0 + w, :], _NT, preferred_element_type=F32)
            pieces = []
            for t0 in range(0, w, blk):
                piece = sc[:, t0:t0 + blk]
                kb = (c0 + t0) // blk
                if kb == qi:
                    piece = piece + bias_ref[0, 0]
                elif kb == qi - 1:
                    piece = piece + bias_ref[0, 1]
                pieces.append(piece)
                for l0 in range(0, blk, LANES):
                    m_acc = jnp.maximum(m_acc, piece[:, l0:l0 + LANES])
            s_scr[par, :, c0:c0 + w] = pieces[0] if len(pieces) == 1 else jnp.concatenate(pieces, axis=1)
        m = jnp.max(m_acc, axis=1, keepdims=True)
        for c0 in range(0, n, KEY_CHUNK):
            w = min(KEY_CHUNK, n - c0)
            p_scr[par, :, c0:c0 + w] = jnp.exp(s_scr[par, :, c0:c0 + w] - m).astype(BF16)
        out = jnp.dot(p_scr[par, :, 0:n], va[0:n, :], preferred_element_type=F32)
        o_ref[qi * blk:n, :] = (out[:, 0:HEAD_DIM] / out[:, HEAD_DIM:HEAD_DIM + 1]).astype(o_ref.dtype)


def _prompt_attention(q, kc, vc, kmeans, bias, far_bias, batch, seq):
    nb = seq // MOBA_BLOCK
    nbp = -(-nb // SUBLANES) * SUBLANES
    assert nbp + SUBLANES <= LANES and (MOBA_BLOCK % KEY_CHUNK == 0 or KEY_CHUNK % MOBA_BLOCK == 0)
    seq_blk = lambda b, h: (b, h)
    return pl.pallas_call(
        functools.partial(_attn_kernel, nb=nb),
        grid=(batch, N_HEADS),
        in_specs=[pl.BlockSpec(memory_space=pltpu.SMEM),
                  pl.BlockSpec((seq, HEAD_DIM), seq_blk),
                  pl.BlockSpec((seq, HEAD_DIM), seq_blk),
                  pl.BlockSpec((seq, HEAD_DIM), seq_blk),
                  pl.BlockSpec((1, 1, LANES, HEAD_DIM), lambda b, h: (b, h, 0, 0)),
                  pl.BlockSpec((1, 2, MOBA_BLOCK, MOBA_BLOCK), lambda b, h: (h, 0, 0, 0))],
        out_specs=pl.BlockSpec((seq, HEAD_DIM), seq_blk),
        out_shape=jax.ShapeDtypeStruct((batch * seq, ATTN_WIDTH), BF16),
        scratch_shapes=[pltpu.VMEM((seq, 2 * HEAD_DIM), BF16),
                        pltpu.VMEM((seq, 2 * HEAD_DIM), BF16),
                        pltpu.VMEM((2, MOBA_BLOCK, 2 * HEAD_DIM), BF16),
                        pltpu.VMEM((2, MOBA_BLOCK, seq), F32),
                        pltpu.VMEM((2, MOBA_BLOCK, seq), BF16)],
        compiler_params=_params(("parallel", "parallel")),
        name="prompt_attn",
    )(far_bias, q, kc, vc, kmeans, bias)


def _sselect_kernel(q_ref, km_ref, o_ref, *, own):
    s_new = q_ref.shape[0]
    col = lax.broadcasted_iota(jnp.int32, (s_new, LANES), 1)
    colf = col.astype(F32)
    for h in range(N_HEADS):
        qh = q_ref[:, h * HEAD_DIM:(h + 1) * HEAD_DIM]
        s = lax.dot_general(qh, km_ref[0, h], _NT, precision=lax.Precision.HIGHEST, preferred_element_type=F32)
        s = jnp.where(col < own, s, NEG)
        out = jnp.zeros((s_new, LANES), F32)
        for t in range(MOBA_TOPK):
            mx = jnp.max(s, axis=1, keepdims=True)
            arg = jnp.min(jnp.where(s == mx, colf, float(LANES)), axis=1, keepdims=True)
            out = jnp.where(col == t, arg, out)
            s = jnp.where(colf == arg, -jnp.inf, s)
        o_ref[0, h] = out.astype(jnp.int32)


def _sample_select(q, kmeans_s, db, s_new, own):
    assert MOBA_TOPK <= own < LANES
    return pl.pallas_call(
        functools.partial(_sselect_kernel, own=own),
        grid=(db,),
        in_specs=[pl.BlockSpec((s_new, ATTN_WIDTH), lambda b: (b, 0)),
                  pl.BlockSpec((1, N_HEADS, LANES, HEAD_DIM), lambda b: (b, 0, 0, 0))],
        out_specs=pl.BlockSpec((1, N_HEADS, s_new, LANES), lambda b: (b, 0, 0, 0)),
        out_shape=jax.ShapeDtypeStruct((db, N_HEADS, s_new, LANES), jnp.int32),
        compiler_params=_params(("parallel",)),
        name="sample_select",
    )(q, kmeans_s)


def _sattn_kernel(pt_ref, idx_ref, far_ref, q_ref, kn_ref, vn_ref, bias_ref, ck_hbm, cv_hbm, o_ref,
                  kbuf, vbuf, sem, *, n_pages, page_size, own, s_new):
    b = pl.program_id(0)
    h = pl.program_id(1)
    nh = pl.num_programs(1)
    step = b * nh + h
    nsteps = pl.num_programs(0) * nh
    slot = step % 2
    ppb = MOBA_BLOCK // page_size
    nsel = s_new * MOBA_TOPK
    nkeys = nsel * MOBA_BLOCK

    def copies(step_, slot_):
        b_ = step_ // nh
        h_ = step_ % nh
        out = []
        for c in range(nsel):
            blk = idx_ref[step_ * nsel + c]
            for r in range(ppb):
                phys = pt_ref[b_ * n_pages + blk * ppb + r]
                dst = pl.ds((c * ppb + r) * page_size, page_size)
                out.append(pltpu.make_async_copy(ck_hbm.at[phys, :, h_, :], kbuf.at[slot_, dst, :], sem.at[slot_, 0]))
                out.append(pltpu.make_async_copy(cv_hbm.at[phys, :, h_, :], vbuf.at[slot_, dst, :], sem.at[slot_, 1]))
        return out

    @pl.when(step == 0)
    def _():
        for cp in copies(step, slot):
            cp.start()

    @pl.when(step + 1 < nsteps)
    def _():
        for cp in copies(step + 1, 1 - slot):
            cp.start()

    for cp in copies(step, slot):
        cp.wait()

    far_hi, far_lo = _far_parts(far_ref[h], (s_new, LANES))
    far = (far_hi + far_lo)[:, 0:1]
    qb = (q_ref[...] * (HEAD_DIM ** -0.5)).astype(BF16)
    kb = kbuf[slot].astype(BF16)
    vb = vbuf[slot].astype(BF16)
    lg = lax.dot_general(qb, kb, _NT, preferred_element_type=F32) + far
    pieces = []
    for c in range(nsel):
        is_prev = idx_ref[step * nsel + c] == own - 1
        row = bias_ref[0, 1, (c // MOBA_TOPK):(c // MOBA_TOPK) + 1, :]
        pieces.append(jnp.where(is_prev, row, 0.0))
    lg = lg + jnp.concatenate(pieces, axis=1)
    col = lax.broadcasted_iota(jnp.int32, (s_new, nkeys), 1)
    row_i = lax.broadcasted_iota(jnp.int32, (s_new, nkeys), 0)
    mine = (col >= row_i * (MOBA_TOPK * MOBA_BLOCK)) & (col < (row_i + 1) * (MOBA_TOPK * MOBA_BLOCK))
    lg = jnp.where(mine, lg, NEG)
    lg_own = (lax.dot_general(qb, kn_ref[...].astype(BF16), _NT, preferred_element_type=F32)
              + far + bias_ref[0, 0, 0:s_new, 0:s_new])
    m = jnp.maximum(jnp.max(lg, axis=1, keepdims=True), jnp.max(lg_own, axis=1, keepdims=True))
    p = jnp.exp(lg - m)
    p_own = jnp.exp(lg_own - m)
    denom = jnp.sum(p, axis=1, keepdims=True) + jnp.sum(p_own, axis=1, keepdims=True)
    acc = (jnp.dot(p.astype(BF16), vb, preferred_element_type=F32)
           + jnp.dot(p_own.astype(BF16), vn_ref[...].astype(BF16), preferred_element_type=F32))
    o_ref[...] = acc / denom


def _sample_attention(q, k, v, cache_k, cache_v, page_table, sel_idx, bias, far_bias, s_new, own):
    db, n_pages = page_table.shape
    page_size = cache_k.shape[1]
    nkeys = s_new * MOBA_TOPK * MOBA_BLOCK
    new_spec = pl.BlockSpec((s_new, HEAD_DIM), lambda b, h, pt, ix: (b, h))
    grid_spec = pltpu.PrefetchScalarGridSpec(
        num_scalar_prefetch=2,
        grid=(db, N_HEADS),
        in_specs=[pl.BlockSpec(memory_space=pltpu.SMEM),
                  new_spec, new_spec, new_spec,
                  pl.BlockSpec((1, 2, MOBA_BLOCK, MOBA_BLOCK), lambda b, h, pt, ix: (h, 0, 0, 0)),
                  pl.BlockSpec(memory_space=pl.ANY),
                  pl.BlockSpec(memory_space=pl.ANY)],
        out_specs=new_spec,
        scratch_shapes=[pltpu.VMEM((2, nkeys, HEAD_DIM), F32),
                        pltpu.VMEM((2, nkeys, HEAD_DIM), F32),
                        pltpu.SemaphoreType.DMA((2, 2))])
    return pl.pallas_call(
        functools.partial(_sattn_kernel, n_pages=n_pages, page_size=page_size, own=own, s_new=s_new),
        grid_spec=grid_spec,
        out_shape=jax.ShapeDtypeStruct((db * s_new, ATTN_WIDTH), F32),
        compiler_params=_params(("arbitrary", "arbitrary")),
        name="sample_attn",
    )(page_table.reshape(-1), sel_idx.reshape(-1), far_bias, q, k, v, bias, cache_k, cache_v)


def _conv_kernel(pt_ref, halo_ref, cur_ref, cw_ref, cb_ref, lg_ref, lb_ref, *rest, tt, rc, zero_first, scan):
    i = pl.program_id(1)
    if scan:
        ck_hbm, o_ref, bsum_hbm, ext_scr, sh = rest[:5]
        rider = _ScanRider(pt_ref, ck_hbm, bsum_hbm, *rest[5:], **scan,
                           step=pl.program_id(0) * pl.num_programs(1) + i,
                           n_steps=pl.num_programs(0) * pl.num_programs(1))
        rider.prologue()
    else:
        o_ref, ext_scr, sh = rest
    if zero_first:
        @pl.when(i == 0)
        def _():
            ext_scr[0:CONV_HALO, :] = jnp.zeros((CONV_HALO, CONV_CH), F32)

        @pl.when(i > 0)
        def _():
            ext_scr[0:CONV_HALO, :] = halo_ref[...]
    else:
        ext_scr[0:CONV_HALO, :] = halo_ref[...]
    ext_scr[CONV_HALO:CONV_HALO + tt, :] = cur_ref[...]
    lead = CONV_HALO - (CONV_WIDTH - 1)

    for r in range(SUBLANES):
        span = tt + CONV_HALO - r
        for s0 in range(0, span, LANES):
            rows = min(LANES, span - s0)
            sh[r, s0:s0 + rows, :] = ext_scr[r + s0:r + s0 + rows, :]

    def chunk(c, carry):
        r0 = pl.multiple_of(c * rc, rc)
        acc = jnp.zeros((rc, CONV_CH), F32)
        for w in range(CONV_WIDTH):
            r = (lead + w) % SUBLANES
            a0 = lead + w - r
            acc = acc + sh[r, pl.ds(pl.multiple_of(r0 + a0, SUBLANES), rc), :] * cw_ref[w:w + 1, :]
        hc = acc + cb_ref[...]
        mu = jnp.mean(hc, axis=-1, keepdims=True)
        var = jnp.mean(jnp.square(hc - mu), axis=-1, keepdims=True)
        y = (hc - mu) * lax.rsqrt(var + EPS) * lg_ref[...] + lb_ref[...]
        o_ref[pl.ds(r0, rc), :] = (y * _sigmoid(y)).astype(o_ref.dtype)
        return carry

    n_chunks = tt // rc
    if scan:
        cps = scan["cps"]
        assert n_chunks % cps == 0
        for k in range(cps):
            rider.phase(k)
            lax.fori_loop(k * (n_chunks // cps), (k + 1) * (n_chunks // cps), chunk, 0)
        rider.finish()
    else:
        lax.fori_loop(0, n_chunks, chunk, 0)


def _conv_module(halo_src, halo_map, cur_src, conv_w, conv_b, ln_g, ln_b, n_seq, n_tiles, tt,
                 zero_first, out_dtype, page_table, cache_k=None, scan_part=None):
    rc = min(tt, 32)
    vec = pl.BlockSpec((1, CONV_CH), lambda b, i, pt: (0, 0))
    cur_map = lambda b, i, pt: (b * n_tiles + i, 0)
    any_spec = pl.BlockSpec(memory_space=pl.ANY)
    scan, scan_out, scan_scratch = (None, None, [])
    if scan_part is not None:
        scan, scan_out, scan_scratch = _scan_plan(page_table, cache_k, n_seq * n_tiles, scan_part)
    grid_spec = pltpu.PrefetchScalarGridSpec(
        num_scalar_prefetch=1,
        grid=(n_seq, n_tiles),
        in_specs=[pl.BlockSpec((CONV_HALO, CONV_CH), halo_map),
                  pl.BlockSpec((tt, CONV_CH), cur_map),
                  pl.BlockSpec((CONV_WIDTH, CONV_CH), lambda b, i, pt: (0, 0)),
                  vec, vec, vec] + ([any_spec] if scan else []),
        out_specs=[pl.BlockSpec((tt, CONV_CH), cur_map)] + ([any_spec] if scan else []),
        scratch_shapes=[pltpu.VMEM((CONV_HALO + tt, CONV_CH), F32),
                        pltpu.VMEM((SUBLANES, tt + CONV_HALO, CONV_CH), F32)] + scan_scratch)
    return pl.pallas_call(
        functools.partial(_conv_kernel, tt=tt, rc=rc, zero_first=zero_first, scan=scan),
        grid_spec=grid_spec,
        out_shape=[jax.ShapeDtypeStruct((n_seq * n_tiles * tt, CONV_CH), out_dtype)] + ([scan_out] if scan else []),
        compiler_params=_params(("arbitrary", "arbitrary")),
        name="conv_module",
    )(page_table.reshape(-1), halo_src, cur_src, conv_w, conv_b, ln_g, ln_b, *([cache_k] if scan else []))


def _mix_kernel(pt_ref, xn_ref, at_ref, cv_ref, wg_ref, bg_ref, wa_ref, wc_ref, *rest, tn, scan):
    if scan:
        ck_hbm, o_ref, bsum_hbm = rest[:3]
        rider = _ScanRider(pt_ref, ck_hbm, bsum_hbm, *rest[3:], **scan)
        rider.prologue()
        per_slot = _spread(scan["cps"], D_MODEL // tn)
        sub = iter(range(scan["cps"]))
    else:
        o_ref = rest[0]
    xn = xn_ref[...]
    at = at_ref[...].astype(BF16)
    cv = cv_ref[...].astype(BF16)
    for c0 in range(0, D_MODEL, tn):
        if scan:
            for _ in range(per_slot[c0 // tn]):
                rider.phase(next(sub))
        a_cols = slice(c0, c0 + tn)
        c_cols = slice(D_MODEL + c0, D_MODEL + c0 + tn)
        ga = _sigmoid(jnp.dot(xn, wg_ref[:, a_cols], preferred_element_type=F32) + bg_ref[:, a_cols])
        gc = _sigmoid(jnp.dot(xn, wg_ref[:, c_cols], preferred_element_type=F32) + bg_ref[:, c_cols])
        ya = jnp.dot(at, wa_ref[:, a_cols], preferred_element_type=F32)
        yc = jnp.dot(cv, wc_ref[:, a_cols], preferred_element_type=F32)
        o_ref[:, a_cols] = (ga * ya + gc * yc).astype(o_ref.dtype)
    if scan:
        rider.finish()


def _gate_mix(xn, attn, cva, w_gate, b_gate, w_attn_out, w_conv_out, tm, page_table, cache_k=None,
              scan_part=None, tn=512):
    n = xn.shape[0]
    row = lambda i, pt: (i, 0)
    any_spec = pl.BlockSpec(memory_space=pl.ANY)
    scan, scan_out, scan_scratch = (None, None, [])
    if scan_part is not None:
        scan, scan_out, scan_scratch = _scan_plan(page_table, cache_k, n // tm, scan_part)
    grid_spec = pltpu.PrefetchScalarGridSpec(
        num_scalar_prefetch=1,
        grid=(n // tm,),
        in_specs=[pl.BlockSpec((tm, D_MODEL), row),
                  pl.BlockSpec((tm, ATTN_WIDTH), row),
                  pl.BlockSpec((tm, CONV_CH), row),
                  _resident(w_gate.shape), _resident(b_gate.shape),
                  _resident(w_attn_out.shape), _resident(w_conv_out.shape)] + ([any_spec] if scan else []),
        out_specs=[pl.BlockSpec((tm, D_MODEL), row)] + ([any_spec] if scan else []),
        scratch_shapes=scan_scratch)
    return pl.pallas_call(
        functools.partial(_mix_kernel, tn=tn, scan=scan),
        grid_spec=grid_spec,
        out_shape=[jax.ShapeDtypeStruct((n, D_MODEL), BF16)] + ([scan_out] if scan else []),
        compiler_params=_params(("arbitrary",)),
        name="gate_mix",
    )(page_table.reshape(-1), xn, attn, cva, w_gate, b_gate, w_attn_out, w_conv_out, *([cache_k] if scan else []))


def _outproj_kernel(pt_ref, x_ref, mix_ref, wo_ref, g2_ref, wrh_ref, wrl_ref, br_ref, cin_ref, *rest, scan):
    if scan:
        ck_hbm, x1_ref, xn2_ref, rt_ref, cout_ref, bsum_hbm, run = rest[:7]
        rider = _ScanRider(pt_ref, ck_hbm, bsum_hbm, *rest[7:], **scan)
        rider.prologue()
        first = -(-scan["cps"] // 2)
        for k in range(first):
            rider.phase(k)
    else:
        x1_ref, xn2_ref, rt_ref, cout_ref, run = rest

    @pl.when(pl.program_id(0) == 0)
    def _():
        run[...] = cin_ref[...]

    x1 = x_ref[...] + jnp.dot(mix_ref[...], wo_ref[...], preferred_element_type=F32)
    x1_ref[...] = x1
    xn2 = x1 * lax.rsqrt(jnp.mean(x1 * x1, axis=-1, keepdims=True) + EPS) * g2_ref[...]
    xn2_ref[...] = _pack_bf16_pairs(xn2)
    if scan:
        for k in range(first, scan["cps"]):
            rider.phase(k)
    hi, lo = _split_bf16(xn2)
    wrh = wrh_ref[...]
    logits = (jnp.dot(hi, wrh, preferred_element_type=F32) + jnp.dot(lo, wrh, preferred_element_type=F32)
              + jnp.dot(hi, wrl_ref[...], preferred_element_type=F32) + br_ref[...])
    tm = logits.shape[0]
    col = lax.broadcasted_iota(jnp.int32, (tm, LANES), 1).astype(F32)
    big = float(LANES)

    def first_argmax(vals):
        mx = jnp.max(vals, axis=1, keepdims=True)
        return mx, jnp.min(jnp.where(vals == mx, col, big), axis=1, keepdims=True)

    gmask = col < N_GROUPS
    gmax, gidx = first_argmax(jnp.where(gmask, logits, -jnp.inf))
    p_g = 1.0 / jnp.sum(jnp.where(gmask, jnp.exp(logits - gmax), 0.0), axis=1, keepdims=True)
    e_lo = N_GROUPS + gidx * EXPERTS_PER_GROUP
    el = jnp.where((col >= e_lo) & (col < e_lo + EXPERTS_PER_GROUP), logits, -jnp.inf)
    v1, i1 = first_argmax(el)
    v2, i2 = first_argmax(jnp.where(col == i1, -jnp.inf, el))
    e21 = jnp.exp(v2 - v1)
    p1 = 1.0 / (1.0 + e21)
    p2 = e21 / (1.0 + e21)
    e1 = i1 - N_GROUPS
    e2 = i2 - N_GROUPS

    oh1 = (col == e1).astype(F32)
    oh2 = (col == e2).astype(F32)
    both = oh1 + oh2
    ri = lax.broadcasted_iota(jnp.int32, (tm, tm), 0)
    ci = lax.broadcasted_iota(jnp.int32, (tm, tm), 1)
    lower = (ci < ri).astype(BF16)
    before = jnp.dot(lower, both.astype(BF16), preferred_element_type=F32) + run[...]
    pos1 = jnp.sum(before * oh1, axis=1, keepdims=True)
    pos2 = jnp.sum(before * oh2, axis=1, keepdims=True)
    new_run = run[...] + jnp.sum(both, axis=0, keepdims=True)
    run[...] = new_run
    cout_ref[...] = new_run

    rt = jnp.zeros((tm, LANES), F32)
    for lane, val in enumerate((e1, e2, p_g * p1, p_g * p2, pos1, pos2)):
        rt = jnp.where(col == lane, val, rt)
    rt_ref[...] = rt
    if scan:
        rider.finish()


def _out_proj(x, mix, w_out, g2, wr_hi, wr_lo, br, cnt_in, tm, page_table, cache_k=None, scan_part=None):
    n = x.shape[0]
    row = lambda i, pt: (i, 0)
    const = lambda i, pt: (0, 0)
    any_spec = pl.BlockSpec(memory_space=pl.ANY)
    scan, scan_out, scan_scratch = (None, None, [])
    if scan_part is not None:
        scan, scan_out, scan_scratch = _scan_plan(page_table, cache_k, n // tm, scan_part)
    grid_spec = pltpu.PrefetchScalarGridSpec(
        num_scalar_prefetch=1,
        grid=(n // tm,),
        in_specs=[pl.BlockSpec((tm, D_MODEL), row),
                  pl.BlockSpec((tm, D_MODEL), row),
                  _resident((D_MODEL, D_MODEL)),
                  _resident((1, D_MODEL)),
                  _resident((D_MODEL, LANES)),
                  _resident((D_MODEL, LANES)),
                  _resident((1, LANES)),
                  _resident((1, LANES))] + ([any_spec] if scan else []),
        out_specs=[pl.BlockSpec((tm, D_MODEL), row),
                   pl.BlockSpec((tm, D_MODEL // 2), row),
                   pl.BlockSpec((tm, LANES), row),
                   pl.BlockSpec((1, LANES), const)] + ([any_spec] if scan else []),
        scratch_shapes=[pltpu.VMEM((1, LANES), F32)] + scan_scratch)
    return pl.pallas_call(
        functools.partial(_outproj_kernel, scan=scan),
        grid_spec=grid_spec,
        out_shape=[jax.ShapeDtypeStruct((n, D_MODEL), F32),
                   jax.ShapeDtypeStruct((n, D_MODEL // 2), jnp.uint32),
                   jax.ShapeDtypeStruct((n, LANES), F32),
                   jax.ShapeDtypeStruct((1, LANES), F32)] + ([scan_out] if scan else []),
        compiler_params=_params(("arbitrary",)),
        name="out_proj",
    )(page_table.reshape(-1), x, mix, w_out, g2, wr_hi, wr_lo, br, cnt_in, *([cache_k] if scan else []))


def _dispatch_kernel(dest_ref, pad0_ref, padn_ref, tail_ref, xp_ref, xs_ref, xb_out, zbuf, sem, zsem, *,
                     tm, np_tiles, n_rows):
    i = pl.program_id(0)

    def issue_rows(x_ref):
        def row_copy(t, d):
            return pltpu.make_async_copy(x_ref.at[pl.ds(t, 1), :], xb_out.at[pl.ds(d, 1), :], sem)

        def issue(t, carry):
            base = (i * tm + t) * 2
            row_copy(t, dest_ref[base]).start()
            row_copy(t, dest_ref[base + 1]).start()
            return carry

        lax.fori_loop(0, tm, issue, 0, unroll=DMA_UNROLL)

    @pl.when(i < np_tiles)
    def _():
        issue_rows(xp_ref)

    @pl.when(i >= np_tiles)
    def _():
        issue_rows(xs_ref)

    @pl.when(i == 0)
    def _():
        zbuf[...] = jnp.zeros(zbuf.shape, zbuf.dtype)

        def zero_row(d):
            return pltpu.make_async_copy(zbuf.at[pl.ds(0, 1), :], xb_out.at[pl.ds(d, 1), :], zsem)

        def zero_blk(d):
            return pltpu.make_async_copy(zbuf, xb_out.at[pl.ds(d, MOE_ROWS), :], zsem)

        def each(n, fn):
            def body(r, carry):
                fn(r)
                return carry
            lax.fori_loop(0, n, body, 0)

        n_tail = (n_rows - tail_ref[0]) // MOE_ROWS
        for e in range(N_EXPERTS):
            each(padn_ref[e], lambda r, e=e: zero_row(pad0_ref[e] + r).start())
        each(n_tail, lambda j: zero_blk(pl.multiple_of(tail_ref[0] + j * MOE_ROWS, MOE_ROWS)).start())
        for e in range(N_EXPERTS):
            each(padn_ref[e], lambda r: zero_row(0).wait())
        each(n_tail, lambda j: zero_blk(0).wait())

    for _ in range(2):
        pltpu.make_async_copy(xp_ref, xb_out.at[pl.ds(0, tm), :], sem).wait()


def _dispatch(xn2_p, xn2_s, dest, pad0, padn, tail, n_rows, tm):
    np_tiles = xn2_p.shape[0] // tm
    ns_tiles = xn2_s.shape[0] // tm
    width = xn2_p.shape[1]
    grid_spec = pltpu.PrefetchScalarGridSpec(
        num_scalar_prefetch=4,
        grid=(np_tiles + ns_tiles,),
        in_specs=[pl.BlockSpec((tm, width), lambda i, *_: (jnp.minimum(i, np_tiles - 1), 0)),
                  pl.BlockSpec((tm, width), lambda i, *_: (jnp.maximum(i - np_tiles, 0), 0))],
        out_specs=pl.BlockSpec(memory_space=pl.ANY),
        scratch_shapes=[pltpu.VMEM((MOE_ROWS, width), xn2_p.dtype),
                        pltpu.SemaphoreType.DMA(()),
                        pltpu.SemaphoreType.DMA(())])
    return pl.pallas_call(
        functools.partial(_dispatch_kernel, tm=tm, np_tiles=np_tiles, n_rows=n_rows),
        grid_spec=grid_spec,
        out_shape=jax.ShapeDtypeStruct((n_rows, width), xn2_p.dtype),
        compiler_params=_params(("arbitrary",)),
        name="moe_dispatch",
    )(dest, pad0, padn, tail, xn2_p, xn2_s)


def _moe_kernel(be_ref, nu_ref, seg_ref, nxt_ref, x_ref, w1_hbm, w3_hbm, w2_hbm, o_ref,
                w1f, w3f, w2f, w1b, w3b, w2b, sem):
    g = pl.program_id(0)
    used = g < nu_ref[0]
    e = be_ref[g]
    new_expert = (g == 0) | (e != be_ref[jnp.maximum(g - 1, 0)])
    slot = seg_ref[g] % 2

    def fetch(expert, s):
        return [pltpu.make_async_copy(w1_hbm.at[expert], w1f.at[s], sem.at[s]),
                pltpu.make_async_copy(w3_hbm.at[expert], w3f.at[s], sem.at[s]),
                pltpu.make_async_copy(w2_hbm.at[expert], w2f.at[s], sem.at[s])]

    @pl.when(g == 0)
    def _():
        for cp in fetch(e, slot):
            cp.start()

    @pl.when(used & new_expert)
    def _():
        for cp in fetch(e, slot):
            cp.wait()
        w1b[...] = w1f[slot].astype(BF16)
        w3b[...] = w3f[slot].astype(BF16)
        w2b[...] = w2f[slot].astype(BF16)

        @pl.when(nxt_ref[e] < N_EXPERTS)
        def _():
            for cp in fetch(nxt_ref[e], 1 - slot):
                cp.start()

    @pl.when(used)
    def _():
        xb = _unpack_bf16_pairs(x_ref[...])
        a = jnp.dot(xb, w1b[...], preferred_element_type=F32)
        gt = jnp.dot(xb, w3b[...], preferred_element_type=F32)
        hmid = (a * _sigmoid(a) * gt).astype(BF16)
        o_ref[...] = jnp.dot(hmid, w2b[...], preferred_element_type=F32)

    @pl.when(jnp.logical_not(used))
    def _():
        o_ref[...] = jnp.zeros(o_ref.shape, F32)


def _moe_experts(xb, block_e, n_used, segment, next_expert, w1, w3, w2):
    n_blocks = xb.shape[0] // MOE_ROWS
    any_spec = pl.BlockSpec(memory_space=pl.ANY)
    grid_spec = pltpu.PrefetchScalarGridSpec(
        num_scalar_prefetch=4,
        grid=(n_blocks,),
        in_specs=[pl.BlockSpec((MOE_ROWS, xb.shape[1]), lambda g, be, nu, *_: (jnp.minimum(g, nu[0] - 1), 0)),
                  any_spec, any_spec, any_spec],
        out_specs=pl.BlockSpec((MOE_ROWS, D_MODEL), lambda g, *_: (g, 0)),
        scratch_shapes=[pltpu.VMEM((2, D_MODEL, D_EXPERT), F32),
                        pltpu.VMEM((2, D_MODEL, D_EXPERT), F32),
                        pltpu.VMEM((2, D_EXPERT, D_MODEL), F32),
                        pltpu.VMEM((D_MODEL, D_EXPERT), BF16),
                        pltpu.VMEM((D_MODEL, D_EXPERT), BF16),
                        pltpu.VMEM((D_EXPERT, D_MODEL), BF16),
                        pltpu.SemaphoreType.DMA((2,))])
    return pl.pallas_call(
        _moe_kernel,
        grid_spec=grid_spec,
        out_shape=jax.ShapeDtypeStruct((xb.shape[0], D_MODEL), F32),
        compiler_params=_params(("arbitrary",)),
        name="moe_experts",
    )(block_e, n_used, segment, next_expert, xb, w1, w3, w2)


def _combine_kernel(dest_ref, x1_ref, rt_ref, gf_ref, yb_hbm, o_ref, ybuf, sem, *, tm):
    i = pl.program_id(0)
    slot = i % 2

    def issue_tile(tile, s):
        def row_copy(k, t, d):
            return pltpu.make_async_copy(yb_hbm.at[pl.ds(d, 1), :], ybuf.at[s, k, pl.ds(t, 1), :], sem.at[s])

        def issue(t, carry):
            base = (tile * tm + t) * 2
            row_copy(0, t, dest_ref[base]).start()
            row_copy(1, t, dest_ref[base + 1]).start()
            return carry

        lax.fori_loop(0, tm, issue, 0, unroll=DMA_UNROLL)

    @pl.when(i == 0)
    def _():
        issue_tile(i, slot)

    @pl.when(i + 1 < pl.num_programs(0))
    def _():
        issue_tile(i + 1, 1 - slot)

    for k in range(2):
        pltpu.make_async_copy(yb_hbm.at[pl.ds(0, tm), :], ybuf.at[slot, k], sem.at[slot]).wait()
    rt = rt_ref[...]
    moe = ybuf[slot, 0] * rt[:, 2:3] + ybuf[slot, 1] * rt[:, 3:4]
    xo = x1_ref[...] + moe
    o_ref[...] = xo * lax.rsqrt(jnp.mean(xo * xo, axis=-1, keepdims=True) + EPS) * gf_ref[...]


def _combine(x1, route, gf, yb, dest, tm):
    n = x1.shape[0]
    grid_spec = pltpu.PrefetchScalarGridSpec(
        num_scalar_prefetch=1,
        grid=(n // tm,),
        in_specs=[pl.BlockSpec((tm, D_MODEL), lambda i, d: (i, 0)),
                  pl.BlockSpec((tm, LANES), lambda i, d: (i, 0)),
                  pl.BlockSpec((1, D_MODEL), lambda i, d: (0, 0)),
                  pl.BlockSpec(memory_space=pl.ANY)],
        out_specs=pl.BlockSpec((tm, D_MODEL), lambda i, d: (i, 0)),
        scratch_shapes=[pltpu.VMEM((2, 2, tm, D_MODEL), F32),
                        pltpu.SemaphoreType.DMA((2,))])
    return pl.pallas_call(
        functools.partial(_combine_kernel, tm=tm),
        grid_spec=grid_spec,
        out_shape=jax.ShapeDtypeStruct((n, D_MODEL), F32),
        compiler_params=_params(("arbitrary",)),
        name="moe_combine",
    )(dest, x1, route, gf, yb)


def _dispatch_plan(routes, counts):
    counts = counts[0, :N_EXPERTS].astype(jnp.int32)
    pcounts = (counts + MOE_ROWS - 1) // MOE_ROWS * MOE_ROWS
    pend = jnp.cumsum(pcounts)
    pstart = pend - pcounts
    dests = []
    n_assign = 0
    for rt in routes:
        e = rt[:, 0:2].astype(jnp.int32)
        pos = rt[:, 4:6].astype(jnp.int32)
        onehot = e[:, :, None] == jnp.arange(N_EXPERTS, dtype=jnp.int32)[None, None, :]
        dests.append((jnp.sum(jnp.where(onehot, pstart[None, None, :], 0), axis=-1) + pos).reshape(-1))
        n_assign += e.size
    n_blocks = (n_assign + N_EXPERTS * (MOE_ROWS - 1) + MOE_ROWS - 1) // MOE_ROWS
    first_row = jnp.arange(n_blocks, dtype=jnp.int32) * MOE_ROWS
    block_e = jnp.minimum(jnp.sum((pend[None, :] <= first_row[:, None]).astype(jnp.int32), axis=1), N_EXPERTS - 1)
    n_used = (pend[-1] // MOE_ROWS).astype(jnp.int32).reshape(1)
    pads = (pstart + counts, pcounts - counts, pend[-1:])
    changed = jnp.concatenate([jnp.zeros((1,), jnp.int32), (block_e[1:] != block_e[:-1]).astype(jnp.int32)])
    segment = jnp.cumsum(changed).astype(jnp.int32)
    ids = jnp.arange(N_EXPERTS, dtype=jnp.int32)
    later_nonempty = (ids[None, :] > ids[:, None]) & (counts[None, :] > 0)
    next_expert = jnp.min(jnp.where(later_nonempty, ids[None, :], N_EXPERTS), axis=1).astype(jnp.int32)
    return dests, block_e.astype(jnp.int32), n_used, n_blocks * MOE_ROWS, pads, segment, next_expert


def kernel(x_prompt, x_sample, cache_k, cache_v, state_conv, page_table, rel_bias, norm1_g, w_in, w_attn_out,
           conv_w, conv_b, conv_ln_g, conv_ln_b, w_conv_out, w_gate, b_gate, w_out, norm2_g, router_g_w,
           router_g_b, router_e_w, router_e_b, moe_w1, moe_w3, moe_w2, norm_f_g):
    depth = norm1_g.shape[0]
    assert depth == 1
    batch, seq, _ = x_prompt.shape
    db, s_new, _ = x_sample.shape
    n_pages = page_table.shape[1]
    page_size = cache_k.shape[2]
    past_len = n_pages * page_size
    assert seq % MOBA_BLOCK == 0 and past_len % MOBA_BLOCK == 0 and s_new <= MOBA_BLOCK and s_new % SUBLANES == 0
    own_s = past_len // MOBA_BLOCK
    n_p = batch * seq
    n_s = db * s_new
    tm = ROW_TILE
    assert n_p % tm == 0 and n_s % tm == 0

    wi_b = w_in[0].astype(BF16)
    wr = jnp.concatenate([router_g_w[0], router_e_w[0]], axis=1)
    wr = jnp.pad(wr, ((0, 0), (0, LANES - wr.shape[1])))
    wr_hi, wr_lo = _split_bf16(wr)
    br = jnp.pad(jnp.concatenate([router_g_b[0], router_e_b[0]]), (0, LANES - N_GROUPS - N_EXPERTS))[None, :]
    far_bias = rel_bias[NUM_BUCKETS - 1, :]
    wg_b, wa_b, wc_b, wo_b = (w[0].astype(BF16) for w in (w_gate, w_attn_out, w_conv_out, w_out))
    cw = conv_w[0]
    bias = _bias_tiles(rel_bias)

    ck, cv = cache_k[0], cache_v[0]
    xn_p, q_p, kc_p, vc_p, u_p, ksum, k_p, v_p, bsum_a = _proj(
        x_prompt.reshape(n_p, D_MODEL), norm1_g, wi_b, tm, BF16, page_table, ck, (0, 1))
    xn_s, q_s, kc_s, vc_s, u_s, _, k_s, v_s = _proj(x_sample.reshape(n_s, D_MODEL), norm1_g, wi_b, tm, F32, page_table)

    nb = seq // MOBA_BLOCK
    km = ksum.reshape(batch, nb, N_HEADS, HEAD_DIM) / MOBA_BLOCK
    km = jnp.pad(km.transpose(0, 2, 1, 3), ((0, 0), (0, 0), (0, LANES - nb), (0, 0)))
    attn_p = _prompt_attention(q_p, kc_p, vc_p, km, bias, far_bias, batch, seq)

    tt = 512
    tiles = seq // tt
    per_halo = tt // CONV_HALO
    cva_p, bsum_b = _conv_module(u_p, lambda b, i, pt: (jnp.maximum((b * tiles + i) * per_halo - 1, 0), 0), u_p,
                                 cw, conv_b, conv_ln_g, conv_ln_b, batch, tiles, tt, True, BF16, page_table, ck, (1, 1))
    lead = CONV_HALO - (CONV_WIDTH - 1)
    halo_s = jnp.pad(state_conv[0], ((0, 0), (lead, 0), (0, 0))).reshape(db * CONV_HALO, CONV_CH)
    cva_s, = _conv_module(halo_s, lambda b, i, pt: (b, 0), u_s,
                          cw, conv_b, conv_ln_g, conv_ln_b, db, 1, s_new, False, F32, page_table)

    mix_p, bsum_c = _gate_mix(xn_p, attn_p, cva_p, wg_b, b_gate, wa_b, wc_b, tm, page_table, ck, (2, 2))
    zero_cnt = jnp.zeros((1, LANES), F32)
    x1_p, xn2_p, rt_p, cnt_p = _out_proj(x_prompt.reshape(n_p, D_MODEL), mix_p, wo_b, norm2_g, wr_hi, wr_lo,
                                         br, zero_cnt, tm, page_table)

    bsum = jnp.concatenate([bsum_a, bsum_b, bsum_c], axis=0)
    k_new = k_s.reshape(db, s_new, N_HEADS, HEAD_DIM)
    sums = jnp.concatenate([bsum, jnp.sum(k_new, axis=1)[:, None]], axis=1)
    km_s = jnp.pad((sums / MOBA_BLOCK).transpose(0, 2, 1, 3), ((0, 0), (0, 0), (0, LANES - own_s - 1), (0, 0)))
    sel_idx = _sample_select(q_s, km_s, db, s_new, own_s)[..., :MOBA_TOPK]
    attn_s = _sample_attention(q_s, kc_s, vc_s, ck, cv, page_table, sel_idx, bias, far_bias, s_new, own_s)

    mix_s, = _gate_mix(xn_s, attn_s, cva_s, wg_b, b_gate, wa_b, wc_b, tm, page_table)
    x1_s, xn2_s, rt_s, cnt = _out_proj(x_sample.reshape(n_s, D_MODEL), mix_s, wo_b, norm2_g, wr_hi, wr_lo, br,
                                       cnt_p, tm, page_table)

    (dest_p, dest_s), block_e, n_used, n_rows, pads, segment, next_expert = _dispatch_plan((rt_p, rt_s), cnt)
    xb = _dispatch(xn2_p, xn2_s, jnp.concatenate([dest_p, dest_s]), *pads, n_rows, tm)
    yb = _moe_experts(xb, block_e, n_used, segment, next_expert, moe_w1[0], moe_w3[0], moe_w2[0])
    gf = norm_f_g[None, :]
    y_p = _combine(x1_p, rt_p, gf, yb, dest_p, tm)
    y_s = _combine(x1_s, rt_s, gf, yb, dest_s, tm)

    conv_prompt = u_p.reshape(batch, seq, CONV_CH)[:, seq - (CONV_WIDTH - 1):]
    conv_sample = jnp.concatenate([state_conv[0], u_s.reshape(db, s_new, CONV_CH)], axis=1)[:, -(CONV_WIDTH - 1):]
    shp_p = (1, batch, seq, N_HEADS, HEAD_DIM)
    shp_s = (1, db, s_new, N_HEADS, HEAD_DIM)
    return (y_p.reshape(batch, seq, D_MODEL), y_s.reshape(db, s_new, D_MODEL),
            k_p.reshape(shp_p), v_p.reshape(shp_p), conv_prompt[None],
            k_s.reshape(shp_s), v_s.reshape(shp_s), conv_sample[None])
```

```python
import functools
import math

import jax
import jax.numpy as jnp
from jax import lax
from jax.experimental import pallas as pl
from jax.experimental.pallas import tpu as pltpu

D_MODEL = 2048
HEAD_DIM = 128
N_HEADS = 8
ATTN_WIDTH = N_HEADS * HEAD_DIM
CONV_CH = 1024
CONV_WIDTH = 31
MOBA_BLOCK = 256
MOBA_TOPK = 3
NUM_BUCKETS = 32
MAX_DISTANCE = 128
N_GROUPS = 4
EXPERTS_PER_GROUP = 8
N_EXPERTS = N_GROUPS * EXPERTS_PER_GROUP
D_EXPERT = 512
EPS = 1e-6
NEG = -1e30

LANES = 128
SUBLANES = 8
CONV_HALO = 32
ROW_TILE = 256
PAGES_PER_CHUNK = 8
DMA_UNROLL = 8
MOE_ROWS = 256
KEY_CHUNK = 512
VMEM_LIMIT = 56 * 1024 * 1024

F32 = jnp.float32
BF16 = jnp.bfloat16
_NT = (((1,), (1,)), ((), ()))


def _params(sem, vmem=VMEM_LIMIT):
    return pltpu.CompilerParams(dimension_semantics=sem, vmem_limit_bytes=vmem)


def _sigmoid(x):
    return 1.0 / (1.0 + jnp.exp(-x))


def _split_bf16(x):
    hi = x.astype(BF16)
    return hi, (x - hi.astype(F32)).astype(BF16)


def _pack_bf16_pairs(x):
    c = x.shape[1] // 2
    lo = lax.bitcast_convert_type(x[:, :c].astype(BF16).astype(F32), jnp.uint32)
    hi = lax.bitcast_convert_type(x[:, c:].astype(BF16).astype(F32), jnp.uint32)
    return (lo >> 16) | (hi & jnp.uint32(0xFFFF0000))


def _unpack_bf16_pairs(p):
    lo = lax.bitcast_convert_type(p << 16, F32).astype(BF16)
    hi = lax.bitcast_convert_type(p & jnp.uint32(0xFFFF0000), F32).astype(BF16)
    return jnp.concatenate([lo, hi], axis=1)


def _resident(shape):
    return pl.BlockSpec(shape, lambda *_: (0,) * len(shape), pipeline_mode=pl.Buffered(1))


class _ScanRider:
    def __init__(self, pt_ref, ck_hbm, bsum_hbm, pgbuf, sumst, psem, osem, *, cps, chunk0, chunks_per_seq,
                 step=None, n_steps=None):
        self.pt_ref, self.ck_hbm, self.bsum_hbm = pt_ref, ck_hbm, bsum_hbm
        self.pgbuf, self.sumst, self.psem, self.osem = pgbuf, sumst, psem, osem
        self.cps, self.chunk0, self.chunks_per_seq = cps, chunk0, chunks_per_seq
        self.step = pl.program_id(0) if step is None else step
        self.n_steps = pl.num_programs(0) if n_steps is None else n_steps
        self.n_local = self.n_steps * cps
        self.bps = sumst.shape[1]

    def _chunk_copies(self, c_local, slot):
        c = self.chunk0 + c_local
        n_pages = self.chunks_per_seq * PAGES_PER_CHUNK
        p0 = (c // self.chunks_per_seq) * n_pages + (c % self.chunks_per_seq) * PAGES_PER_CHUNK
        return [pltpu.make_async_copy(self.ck_hbm.at[self.pt_ref[p0 + r]], self.pgbuf.at[slot, r],
                                      self.psem.at[slot]) for r in range(PAGES_PER_CHUNK)]

    def _sum_copy(self, c_local, k):
        dst = self.bsum_hbm.at[c_local // self.chunks_per_seq,
                               pl.ds((c_local % self.chunks_per_seq) * self.bps, self.bps)]
        return pltpu.make_async_copy(self.sumst.at[k], dst, self.osem)

    def prologue(self):
        @pl.when(self.step == 0)
        def _():
            for cp in self._chunk_copies(0, 0):
                cp.start()

    def phase(self, k):
        c_local = self.step * self.cps + k
        slot = k % 2
        for cp in self._chunk_copies(jnp.minimum(c_local + 1, self.n_local - 1), 1 - slot):
            cp.start()
        for cp in self._chunk_copies(c_local, slot):
            cp.wait()
        ppb = PAGES_PER_CHUNK // self.bps
        page_size = self.pgbuf.shape[2]

        def page_sum(p):
            page = self.pgbuf[slot, p].reshape(page_size // SUBLANES, SUBLANES, N_HEADS, HEAD_DIM)
            return jnp.sum(jnp.sum(page, axis=0), axis=0)

        for t in range(self.bps):
            tot = page_sum(t * ppb)
            for r in range(1, ppb):
                tot = tot + page_sum(t * ppb + r)
            self.sumst[k, t] = tot
        self._sum_copy(c_local, k).start()

    def finish(self):
        for k in range(self.cps):
            self._sum_copy(0, k).wait()

        @pl.when(self.step == self.n_steps - 1)
        def _():
            for cp in self._chunk_copies(self.n_local - 1, self.cps % 2):
                cp.wait()


def _scan_plan(page_table, cache_k, n_steps, part):
    db, n_pages = page_table.shape
    page_size = cache_k.shape[1]
    ppb = MOBA_BLOCK // page_size
    assert MOBA_BLOCK % page_size == 0 and PAGES_PER_CHUNK % ppb == 0 and n_pages % PAGES_PER_CHUNK == 0
    bps = PAGES_PER_CHUNK // ppb
    chunks_per_seq = n_pages // PAGES_PER_CHUNK
    n_chunks = db * chunks_per_seq
    assert n_chunks % 4 == 0
    chunk0, per_part = part[0] * (n_chunks // 4), part[1] * (n_chunks // 4)
    assert per_part % n_steps == 0 and chunk0 % chunks_per_seq == 0 and per_part % chunks_per_seq == 0
    cps = per_part // n_steps
    assert cps % 2 == 0
    cfg = dict(cps=cps, chunk0=chunk0, chunks_per_seq=chunks_per_seq)
    out_shape = jax.ShapeDtypeStruct((per_part // chunks_per_seq, n_pages // ppb, N_HEADS, HEAD_DIM), F32)
    scratch = [pltpu.VMEM((2, PAGES_PER_CHUNK, page_size, N_HEADS, HEAD_DIM), F32),
               pltpu.VMEM((cps, bps, N_HEADS, HEAD_DIM), F32),
               pltpu.SemaphoreType.DMA((2,)),
               pltpu.SemaphoreType.DMA(())]
    return cfg, out_shape, scratch


def _spread(n_items, n_slots):
    return [len(range(s, n_items, n_slots)) for s in range(n_slots)]


def _proj_kernel(pt_ref, x_ref, g_ref, w_ref, *rest, tm, scan):
    if scan:
        ck_hbm, rest = rest[0], rest[1:]
    xn_ref, q_ref, kc_ref, vc_ref, u_ref, ks_ref, k_hbm, v_hbm = rest[:8]
    rest = rest[8:]
    if scan:
        bsum_hbm, rest = rest[0], rest[1:]
    kst, vst, sem = rest[:3]
    rider = _ScanRider(pt_ref, ck_hbm, bsum_hbm, *rest[3:], **scan) if scan else None
    sub = iter(range(scan["cps"])) if scan else iter(())
    per_slot = _spread(scan["cps"], 4) if scan else [0] * 4

    def ride(slot_i):
        for _ in range(per_slot[slot_i]):
            rider.phase(next(sub))

    i = pl.program_id(0)
    aw = ATTN_WIDTH
    if scan:
        rider.prologue()

    def head_copies(src, dst_hbm, row_tile, s):
        r0 = pl.multiple_of(row_tile * tm, tm)
        return [pltpu.make_async_copy(src.at[:, h * HEAD_DIM:(h + 1) * HEAD_DIM],
                                      dst_hbm.at[pl.ds(r0, tm), h, :], sem.at[s]) for h in range(N_HEADS)]

    x = x_ref[...]
    xb = (x * lax.rsqrt(jnp.mean(x * x, axis=-1, keepdims=True) + EPS) * g_ref[...]).astype(BF16)
    xn_ref[...] = xb
    q_ref[...] = jnp.dot(xb, w_ref[:, 0:aw], preferred_element_type=F32)

    def emit(acc, stage, hbm, c_ref, s):
        @pl.when(i > 0)
        def _():
            for cp in head_copies(stage, hbm, i - 1, s):
                cp.wait()
        stage[...] = acc
        c_ref[...] = acc.astype(c_ref.dtype)
        for cp in head_copies(stage, hbm, i, s):
            cp.start()

    ride(0)
    k = jnp.dot(xb, w_ref[:, aw:2 * aw], preferred_element_type=F32)
    emit(k, kst, k_hbm, kc_ref, 0)
    ks_ref[0] = k.reshape(tm // MOBA_BLOCK, MOBA_BLOCK, aw).sum(axis=1)
    ride(1)
    emit(jnp.dot(xb, w_ref[:, 2 * aw:3 * aw], preferred_element_type=F32), vst, v_hbm, vc_ref, 1)
    ride(2)
    ua = jnp.dot(xb, w_ref[:, 3 * aw:3 * aw + CONV_CH], preferred_element_type=F32)
    ride(3)
    ug = jnp.dot(xb, w_ref[:, 3 * aw + CONV_CH:], preferred_element_type=F32)
    u_ref[...] = ua * _sigmoid(ug)
    if scan:
        rider.finish()

    @pl.when(i == pl.num_programs(0) - 1)
    def _():
        for cp in head_copies(kst, k_hbm, i, 0) + head_copies(vst, v_hbm, i, 1):
            cp.wait()


def _proj(x, g1, w_in, tm, kv_dtype, page_table, cache_k=None, scan_part=None):
    n = x.shape[0]
    nblk = tm // MOBA_BLOCK
    row = lambda i, pt: (i, 0)
    any_spec = pl.BlockSpec(memory_space=pl.ANY)
    scan, scan_out, scan_scratch = (None, None, [])
    if scan_part is not None:
        scan, scan_out, scan_scratch = _scan_plan(page_table, cache_k, n // tm, scan_part)
    grid_spec = pltpu.PrefetchScalarGridSpec(
        num_scalar_prefetch=1,
        grid=(n // tm,),
        in_specs=[pl.BlockSpec((tm, D_MODEL), row),
                  _resident((1, D_MODEL)),
                  _resident(w_in.shape)] + ([any_spec] if scan else []),
        out_specs=[pl.BlockSpec((tm, D_MODEL), row),
                   pl.BlockSpec((tm, ATTN_WIDTH), row),
                   pl.BlockSpec((tm, ATTN_WIDTH), row),
                   pl.BlockSpec((tm, ATTN_WIDTH), row),
                   pl.BlockSpec((tm, CONV_CH), row),
                   pl.BlockSpec((1, nblk, ATTN_WIDTH), lambda i, pt: (i, 0, 0)),
                   any_spec, any_spec] + ([any_spec] if scan else []),
        scratch_shapes=[pltpu.VMEM((tm, ATTN_WIDTH), F32),
                        pltpu.VMEM((tm, ATTN_WIDTH), F32),
                        pltpu.SemaphoreType.DMA((2,))] + scan_scratch)
    return pl.pallas_call(
        functools.partial(_proj_kernel, tm=tm, scan=scan),
        grid_spec=grid_spec,
        out_shape=[jax.ShapeDtypeStruct((n, D_MODEL), BF16),
                   jax.ShapeDtypeStruct((n, ATTN_WIDTH), F32),
                   jax.ShapeDtypeStruct((n, ATTN_WIDTH), kv_dtype),
                   jax.ShapeDtypeStruct((n, ATTN_WIDTH), kv_dtype),
                   jax.ShapeDtypeStruct((n, CONV_CH), F32),
                   jax.ShapeDtypeStruct((n // tm, nblk, ATTN_WIDTH), F32),
                   jax.ShapeDtypeStruct((n, N_HEADS, HEAD_DIM), F32),
                   jax.ShapeDtypeStruct((n, N_HEADS, HEAD_DIM), F32)] + ([scan_out] if scan else []),
        compiler_params=_params(("arbitrary",)),
        name="proj",
    )(page_table.reshape(-1), x, g1, w_in, *([cache_k] if scan else []))


def _t5_bucket_py(n):
    max_exact = NUM_BUCKETS // 2
    if n < max_exact:
        return n
    return min(max_exact + int(math.log(n / max_exact) / math.log(MAX_DISTANCE / max_exact)
                               * (NUM_BUCKETS - max_exact)), NUM_BUCKETS - 1)


assert _t5_bucket_py(MOBA_BLOCK + 1) == NUM_BUCKETS - 1


def _far_parts(far, shape):
    hi, lo = _split_bf16(jnp.full(shape, far, F32))
    return hi.astype(F32), lo.astype(F32)


def _bias_kernel(rb_ref, o_ref):
    h = pl.program_id(0)
    shape = (MOBA_BLOCK, MOBA_BLOCK)
    qi = lax.broadcasted_iota(jnp.int32, shape, 0)
    kj = lax.broadcasted_iota(jnp.int32, shape, 1)
    max_exact = NUM_BUCKETS // 2
    far_hi, far_lo = _far_parts(rb_ref[NUM_BUCKETS - 1, h], shape)
    for c in range(2):
        dist = qi - kj + c * MOBA_BLOCK
        nn = jnp.maximum(dist, 0)
        nf = jnp.maximum(nn, 1).astype(F32)
        large = max_exact + (jnp.log(nf / max_exact) / math.log(MAX_DISTANCE / max_exact)
                             * (NUM_BUCKETS - max_exact)).astype(jnp.int32)
        large = jnp.minimum(large, NUM_BUCKETS - 1)
        bkt = jnp.where(nn < max_exact, nn, large)
        val = jnp.zeros(shape, F32)
        for t in range(NUM_BUCKETS):
            val = jnp.where(bkt == t, rb_ref[t, h], val)
        val = val - (far_hi + far_lo)
        if c == 0:
            val = jnp.where(dist >= 0, val, NEG)
        o_ref[0, c] = val


def _bias_tiles(rel_bias):
    return pl.pallas_call(
        _bias_kernel,
        grid=(N_HEADS,),
        in_specs=[pl.BlockSpec(memory_space=pltpu.SMEM)],
        out_specs=pl.BlockSpec((1, 2, MOBA_BLOCK, MOBA_BLOCK), lambda h: (h, 0, 0, 0)),
        out_shape=jax.ShapeDtypeStruct((N_HEADS, 2, MOBA_BLOCK, MOBA_BLOCK), F32),
        compiler_params=_params(("arbitrary",)),
        name="bias_tiles",
    )(rel_bias)


def _attn_kernel(far_ref, q_ref, k_ref, v_ref, km_ref, bias_ref, o_ref, ka, va, qa, s_scr, p_scr, *, nb):
    h = pl.program_id(1)
    blk = MOBA_BLOCK
    seq = nb * blk
    nbp = -(-nb // SUBLANES) * SUBLANES

    row_blk = lax.broadcasted_iota(jnp.int32, (seq, LANES), 0) // blk
    col = lax.broadcasted_iota(jnp.int32, (seq, LANES), 1)
    ka[:, 0:HEAD_DIM] = k_ref[...]
    ka[:, HEAD_DIM:] = jnp.where((col == row_blk) | (col == nbp) | (col == nbp + 1), 1.0, 0.0).astype(BF16)
    va[:, 0:HEAD_DIM] = v_ref[...]
    va[:, HEAD_DIM:] = jnp.ones((seq, LANES), BF16)

    far_hi, far_lo = _far_parts(far_ref[h], (SUBLANES, blk))
    r8 = lax.broadcasted_iota(jnp.int32, (SUBLANES, blk), 0)
    far_rows = jnp.where(r8 == 0, far_hi, jnp.where(r8 == 1, far_lo, 0.0))
    rowi = lax.broadcasted_iota(jnp.int32, (nbp, blk), 0)
    km = km_ref[0, 0, 0:nbp, :]

    for qi in range(nb):
        par = qi % 2
        n = (qi + 1) * blk
        qt = q_ref[qi * blk:n, :]

        if qi > MOBA_TOPK:
            st = lax.dot_general(km, qt, _NT, precision=lax.Precision.HIGHEST, preferred_element_type=F32)
            rank = jnp.zeros((nbp, blk), jnp.int32)
            for i in range(qi):
                si = st[i:i + 1, :]
                rank = rank + ((si > st) | ((si == st) & (rowi > i))).astype(jnp.int32)
            dropped = (rowi < qi) & (rank >= MOBA_TOPK)
            selv = jnp.where(dropped, NEG, 0.0)
        else:
            selv = jnp.zeros((nbp, blk), F32)
        aug = jnp.concatenate([selv, far_rows, jnp.zeros((LANES - nbp - SUBLANES, blk), F32)], axis=0)
        qa[par, :, HEAD_DIM:] = aug.T.astype(BF16)
        qa[par, :, 0:HEAD_DIM] = (qt * (HEAD_DIM ** -0.5)).astype(BF16)

        m_acc = jnp.full((blk, LANES), NEG, F32)
        for c0 in range(0, n, KEY_CHUNK):
            w = min(KEY_CHUNK, n - c0)
            sc = lax.dot_general(qa[par], ka[c0:c0 + w, :], _NT, preferred_element_type=F32)
            pieces = []
            for t0 in range(0, w, blk):
                piece = sc[:, t0:t0 + blk]
                kb = (c0 + t0) // blk
                if kb == qi:
                    piece = piece + bias_ref[0, 0]
                elif kb == qi - 1:
                    piece = piece + bias_ref[0, 1]
                pieces.append(piece)
                for l0 in range(0, blk, LANES):
                    m_acc = jnp.maximum(m_acc, piece[:, l0:l0 + LANES])
            s_scr[par, :, c0:c0 + w] = pieces[0] if len(pieces) == 1 else jnp.concatenate(pieces, axis=1)
        m = jnp.max(m_acc, axis=1, keepdims=True)
        for c0 in range(0, n, KEY_CHUNK):
            w = min(KEY_CHUNK, n - c0)
            p_scr[par, :, c0:c0 + w] = jnp.exp(s_scr[par, :, c0:c0 + w] - m).astype(BF16)
        out = jnp.dot(p_scr[par, :, 0:n], va[0:n, :], preferred_element_type=F32)
        o_ref[qi * blk:n, :] = (out[:, 0:HEAD_DIM] / out[:, HEAD_DIM:HEAD_DIM + 1]).astype(o_ref.dtype)


def _prompt_attention(q, kc, vc, kmeans, bias, far_bias, batch, seq):
    nb = seq // MOBA_BLOCK
    nbp = -(-nb // SUBLANES) * SUBLANES
    assert nbp + SUBLANES <= LANES and (MOBA_BLOCK % KEY_CHUNK == 0 or KEY_CHUNK % MOBA_BLOCK == 0)
    seq_blk = lambda b, h: (b, h)
    return pl.pallas_call(
        functools.partial(_attn_kernel, nb=nb),
        grid=(batch, N_HEADS),
        in_specs=[pl.BlockSpec(memory_space=pltpu.SMEM),
                  pl.BlockSpec((seq, HEAD_DIM), seq_blk),
                  pl.BlockSpec((seq, HEAD_DIM), seq_blk),
                  pl.BlockSpec((seq, HEAD_DIM), seq_blk),
                  pl.BlockSpec((1, 1, LANES, HEAD_DIM), lambda b, h: (b, h, 0, 0)),
                  pl.BlockSpec((1, 2, MOBA_BLOCK, MOBA_BLOCK), lambda b, h: (h, 0, 0, 0))],
        out_specs=pl.BlockSpec((seq, HEAD_DIM), seq_blk),
        out_shape=jax.ShapeDtypeStruct((batch * seq, ATTN_WIDTH), BF16),
        scratch_shapes=[pltpu.VMEM((seq, 2 * HEAD_DIM), BF16),
                        pltpu.VMEM((seq, 2 * HEAD_DIM), BF16),
                        pltpu.VMEM((2, MOBA_BLOCK, 2 * HEAD_DIM), BF16),
                        pltpu.VMEM((2, MOBA_BLOCK, seq), F32),
                        pltpu.VMEM((2, MOBA_BLOCK, seq), BF16)],
        compiler_params=_params(("parallel", "parallel")),
        name="prompt_attn",
    )(far_bias, q, kc, vc, kmeans, bias)


def _sselect_kernel(q_ref, km_ref, o_ref, *, own):
    s_new = q_ref.shape[0]
    col = lax.broadcasted_iota(jnp.int32, (s_new, LANES), 1)
    colf = col.astype(F32)
    for h in range(N_HEADS):
        qh = q_ref[:, h * HEAD_DIM:(h + 1) * HEAD_DIM]
        s = lax.dot_general(qh, km_ref[0, h], _NT, precision=lax.Precision.HIGHEST, preferred_element_type=F32)
        s = jnp.where(col < own, s, NEG)
        out = jnp.zeros((s_new, LANES), F32)
        for t in range(MOBA_TOPK):
            mx = jnp.max(s, axis=1, keepdims=True)
            arg = jnp.min(jnp.where(s == mx, colf, float(LANES)), axis=1, keepdims=True)
            out = jnp.where(col == t, arg, out)
            s = jnp.where(colf == arg, -jnp.inf, s)
        o_ref[0, h] = out.astype(jnp.int32)


def _sample_select(q, kmeans_s, db, s_new, own):
    assert MOBA_TOPK <= own < LANES
    return pl.pallas_call(
        functools.partial(_sselect_kernel, own=own),
        grid=(db,),
        in_specs=[pl.BlockSpec((s_new, ATTN_WIDTH), lambda b: (b, 0)),
                  pl.BlockSpec((1, N_HEADS, LANES, HEAD_DIM), lambda b: (b, 0, 0, 0))],
        out_specs=pl.BlockSpec((1, N_HEADS, s_new, LANES), lambda b: (b, 0, 0, 0)),
        out_shape=jax.ShapeDtypeStruct((db, N_HEADS, s_new, LANES), jnp.int32),
        compiler_params=_params(("parallel",)),
        name="sample_select",
    )(q, kmeans_s)


def _sattn_kernel(pt_ref, idx_ref, far_ref, q_ref, kn_ref, vn_ref, bias_ref, ck_hbm, cv_hbm, o_ref,
                  kbuf, vbuf, sem, *, n_pages, page_size, own, s_new):
    b = pl.program_id(0)
    h = pl.program_id(1)
    nh = pl.num_programs(1)
    step = b * nh + h
    nsteps = pl.num_programs(0) * nh
    slot = step % 2
    ppb = MOBA_BLOCK // page_size
    nsel = s_new * MOBA_TOPK
    nkeys = nsel * MOBA_BLOCK

    def copies(step_, slot_):
        b_ = step_ // nh
        h_ = step_ % nh
        out = []
        for c in range(nsel):
            blk = idx_ref[step_ * nsel + c]
            for r in range(ppb):
                phys = pt_ref[b_ * n_pages + blk * ppb + r]
                dst = pl.ds((c * ppb + r) * page_size, page_size)
                out.append(pltpu.make_async_copy(ck_hbm.at[phys, :, h_, :], kbuf.at[slot_, dst, :], sem.at[slot_, 0]))
                out.append(pltpu.make_async_copy(cv_hbm.at[phys, :, h_, :], vbuf.at[slot_, dst, :], sem.at[slot_, 1]))
        return out

    @pl.when(step == 0)
    def _():
        for cp in copies(step, slot):
            cp.start()

    @pl.when(step + 1 < nsteps)
    def _():
        for cp in copies(step + 1, 1 - slot):
            cp.start()

    for cp in copies(step, slot):
        cp.wait()

    far_hi, far_lo = _far_parts(far_ref[h], (s_new, LANES))
    far = (far_hi + far_lo)[:, 0:1]
    qb = (q_ref[...] * (HEAD_DIM ** -0.5)).astype(BF16)
    kb = kbuf[slot].astype(BF16)
    vb = vbuf[slot].astype(BF16)
    lg = lax.dot_general(qb, kb, _NT, preferred_element_type=F32) + far
    pieces = []
    for c in range(nsel):
        is_prev = idx_ref[step * nsel + c] == own - 1
        row = bias_ref[0, 1, (c // MOBA_TOPK):(c // MOBA_TOPK) + 1, :]
        pieces.append(jnp.where(is_prev, row, 0.0))
    lg = lg + jnp.concatenate(pieces, axis=1)
    col = lax.broadcasted_iota(jnp.int32, (s_new, nkeys), 1)
    row_i = lax.broadcasted_iota(jnp.int32, (s_new, nkeys), 0)
    mine = (col >= row_i * (MOBA_TOPK * MOBA_BLOCK)) & (col < (row_i + 1) * (MOBA_TOPK * MOBA_BLOCK))
    lg = jnp.where(mine, lg, NEG)
    lg_own = (lax.dot_general(qb, kn_ref[...].astype(BF16), _NT, preferred_element_type=F32)
              + far + bias_ref[0, 0, 0:s_new, 0:s_new])
    m = jnp.maximum(jnp.max(lg, axis=1, keepdims=True), jnp.max(lg_own, axis=1, keepdims=True))
    p = jnp.exp(lg - m)
    p_own = jnp.exp(lg_own - m)
    denom = jnp.sum(p, axis=1, keepdims=True) + jnp.sum(p_own, axis=1, keepdims=True)
    acc = (jnp.dot(p.astype(BF16), vb, preferred_element_type=F32)
           + jnp.dot(p_own.astype(BF16), vn_ref[...].astype(BF16), preferred_element_type=F32))
    o_ref[...] = acc / denom


def _sample_attention(q, k, v, cache_k, cache_v, page_table, sel_idx, bias, far_bias, s_new, own):
    db, n_pages = page_table.shape
    page_size = cache_k.shape[1]
    nkeys = s_new * MOBA_TOPK * MOBA_BLOCK
    new_spec = pl.BlockSpec((s_new, HEAD_DIM), lambda b, h, pt, ix: (b, h))
    grid_spec = pltpu.PrefetchScalarGridSpec(
        num_scalar_prefetch=2,
        grid=(db, N_HEADS),
        in_specs=[pl.BlockSpec(memory_space=pltpu.SMEM),
                  new_spec, new_spec, new_spec,
                  pl.BlockSpec((1, 2, MOBA_BLOCK, MOBA_BLOCK), lambda b, h, pt, ix: (h, 0, 0, 0)),
                  pl.BlockSpec(memory_space=pl.ANY),
                  pl.BlockSpec(memory_space=pl.ANY)],
        out_specs=new_spec,
        scratch_shapes=[pltpu.VMEM((2, nkeys, HEAD_DIM), F32),
                        pltpu.VMEM((2, nkeys, HEAD_DIM), F32),
                        pltpu.SemaphoreType.DMA((2, 2))])
    return pl.pallas_call(
        functools.partial(_sattn_kernel, n_pages=n_pages, page_size=page_size, own=own, s_new=s_new),
        grid_spec=grid_spec,
        out_shape=jax.ShapeDtypeStruct((db * s_new, ATTN_WIDTH), F32),
        compiler_params=_params(("arbitrary", "arbitrary")),
        name="sample_attn",
    )(page_table.reshape(-1), sel_idx.reshape(-1), far_bias, q, k, v, bias, cache_k, cache_v)


def _conv_kernel(pt_ref, halo_ref, cur_ref, cw_ref, cb_ref, lg_ref, lb_ref, *rest, tt, rc, zero_first, scan):
    i = pl.program_id(1)
    if scan:
        ck_hbm, o_ref, bsum_hbm, ext_scr, sh = rest[:5]
        rider = _ScanRider(pt_ref, ck_hbm, bsum_hbm, *rest[5:], **scan,
                           step=pl.program_id(0) * pl.num_programs(1) + i,
                           n_steps=pl.num_programs(0) * pl.num_programs(1))
        rider.prologue()
    else:
        o_ref, ext_scr, sh = rest
    if zero_first:
        @pl.when(i == 0)
        def _():
            ext_scr[0:CONV_HALO, :] = jnp.zeros((CONV_HALO, CONV_CH), F32)

        @pl.when(i > 0)
        def _():
            ext_scr[0:CONV_HALO, :] = halo_ref[...]
    else:
        ext_scr[0:CONV_HALO, :] = halo_ref[...]
    ext_scr[CONV_HALO:CONV_HALO + tt, :] = cur_ref[...]
    lead = CONV_HALO - (CONV_WIDTH - 1)

    for r in range(SUBLANES):
        span = tt + CONV_HALO - r
        for s0 in range(0, span, LANES):
            rows = min(LANES, span - s0)
            sh[r, s0:s0 + rows, :] = ext_scr[r + s0:r + s0 + rows, :]

    def chunk(c, carry):
        r0 = pl.multiple_of(c * rc, rc)
        acc = jnp.zeros((rc, CONV_CH), F32)
        for w in range(CONV_WIDTH):
            r = (lead + w) % SUBLANES
            a0 = lead + w - r
            acc = acc + sh[r, pl.ds(pl.multiple_of(r0 + a0, SUBLANES), rc), :] * cw_ref[w:w + 1, :]
        hc = acc + cb_ref[...]
        mu = jnp.mean(hc, axis=-1, keepdims=True)
        var = jnp.mean(jnp.square(hc - mu), axis=-1, keepdims=True)
        y = (hc - mu) * lax.rsqrt(var + EPS) * lg_ref[...] + lb_ref[...]
        o_ref[pl.ds(r0, rc), :] = (y * _sigmoid(y)).astype(o_ref.dtype)
        return carry

    n_chunks = tt // rc
    if scan:
        cps = scan["cps"]
        assert n_chunks % cps == 0
        for k in range(cps):
            rider.phase(k)
            lax.fori_loop(k * (n_chunks // cps), (k + 1) * (n_chunks // cps), chunk, 0)
        rider.finish()
    else:
        lax.fori_loop(0, n_chunks, chunk, 0)


def _conv_module(halo_src, halo_map, cur_src, conv_w, conv_b, ln_g, ln_b, n_seq, n_tiles, tt,
                 zero_first, out_dtype, page_table, cache_k=None, scan_part=None):
    rc = min(tt, 32)
    vec = pl.BlockSpec((1, CONV_CH), lambda b, i, pt: (0, 0))
    cur_map = lambda b, i, pt: (b * n_tiles + i, 0)
    any_spec = pl.BlockSpec(memory_space=pl.ANY)
    scan, scan_out, scan_scratch = (None, None, [])
    if scan_part is not None:
        scan, scan_out, scan_scratch = _scan_plan(page_table, cache_k, n_seq * n_tiles, scan_part)
    grid_spec = pltpu.PrefetchScalarGridSpec(
        num_scalar_prefetch=1,
        grid=(n_seq, n_tiles),
        in_specs=[pl.BlockSpec((CONV_HALO, CONV_CH), halo_map),
                  pl.BlockSpec((tt, CONV_CH), cur_map),
                  pl.BlockSpec((CONV_WIDTH, CONV_CH), lambda b, i, pt: (0, 0)),
                  vec, vec, vec] + ([any_spec] if scan else []),
        out_specs=[pl.BlockSpec((tt, CONV_CH), cur_map)] + ([any_spec] if scan else []),
        scratch_shapes=[pltpu.VMEM((CONV_HALO + tt, CONV_CH), F32),
                        pltpu.VMEM((SUBLANES, tt + CONV_HALO, CONV_CH), F32)] + scan_scratch)
    return pl.pallas_call(
        functools.partial(_conv_kernel, tt=tt, rc=rc, zero_first=zero_first, scan=scan),
        grid_spec=grid_spec,
        out_shape=[jax.ShapeDtypeStruct((n_seq * n_tiles * tt, CONV_CH), out_dtype)] + ([scan_out] if scan else []),
        compiler_params=_params(("arbitrary", "arbitrary")),
        name="conv_module",
    )(page_table.reshape(-1), halo_src, cur_src, conv_w, conv_b, ln_g, ln_b, *([cache_k] if scan else []))


def _mix_kernel(pt_ref, xn_ref, at_ref, cv_ref, wg_ref, bg_ref, wa_ref, wc_ref, *rest, tn, scan):
    if scan:
        ck_hbm, o_ref, bsum_hbm = rest[:3]
        rider = _ScanRider(pt_ref, ck_hbm, bsum_hbm, *rest[3:], **scan)
        rider.prologue()
        per_slot = _spread(scan["cps"], D_MODEL // tn)
        sub = iter(range(scan["cps"]))
    else:
        o_ref = rest[0]
    xn = xn_ref[...]
    at = at_ref[...].astype(BF16)
    cv = cv_ref[...].astype(BF16)
    for c0 in range(0, D_MODEL, tn):
        if scan:
            for _ in range(per_slot[c0 // tn]):
                rider.phase(next(sub))
        a_cols = slice(c0, c0 + tn)
        c_cols = slice(D_MODEL + c0, D_MODEL + c0 + tn)
        ga = _sigmoid(jnp.dot(xn, wg_ref[:, a_cols], preferred_element_type=F32) + bg_ref[:, a_cols])
        gc = _sigmoid(jnp.dot(xn, wg_ref[:, c_cols], preferred_element_type=F32) + bg_ref[:, c_cols])
        ya = jnp.dot(at, wa_ref[:, a_cols], preferred_element_type=F32)
        yc = jnp.dot(cv, wc_ref[:, a_cols], preferred_element_type=F32)
        o_ref[:, a_cols] = (ga * ya + gc * yc).astype(o_ref.dtype)
    if scan:
        rider.finish()


def _gate_mix(xn, attn, cva, w_gate, b_gate, w_attn_out, w_conv_out, tm, page_table, cache_k=None,
              scan_part=None, tn=512):
    n = xn.shape[0]
    row = lambda i, pt: (i, 0)
    any_spec = pl.BlockSpec(memory_space=pl.ANY)
    scan, scan_out, scan_scratch = (None, None, [])
    if scan_part is not None:
        scan, scan_out, scan_scratch = _scan_plan(page_table, cache_k, n // tm, scan_part)
    grid_spec = pltpu.PrefetchScalarGridSpec(
        num_scalar_prefetch=1,
        grid=(n // tm,),
        in_specs=[pl.BlockSpec((tm, D_MODEL), row),
                  pl.BlockSpec((tm, ATTN_WIDTH), row),
                  pl.BlockSpec((tm, CONV_CH), row),
                  _resident(w_gate.shape), _resident(b_gate.shape),
                  _resident(w_attn_out.shape), _resident(w_conv_out.shape)] + ([any_spec] if scan else []),
        out_specs=[pl.BlockSpec((tm, D_MODEL), row)] + ([any_spec] if scan else []),
        scratch_shapes=scan_scratch)
    return pl.pallas_call(
        functools.partial(_mix_kernel, tn=tn, scan=scan),
        grid_spec=grid_spec,
        out_shape=[jax.ShapeDtypeStruct((n, D_MODEL), BF16)] + ([scan_out] if scan else []),
        compiler_params=_params(("arbitrary",)),
        name="gate_mix",
    )(page_table.reshape(-1), xn, attn, cva, w_gate, b_gate, w_attn_out, w_conv_out, *([cache_k] if scan else []))


def _outproj_kernel(pt_ref, x_ref, mix_ref, wo_ref, g2_ref, wrh_ref, wrl_ref, br_ref, cin_ref, *rest, scan):
    if scan:
        ck_hbm, x1_ref, xn2_ref, rt_ref, cout_ref, bsum_hbm, run = rest[:7]
        rider = _ScanRider(pt_ref, ck_hbm, bsum_hbm, *rest[7:], **scan)
        rider.prologue()
        first = -(-scan["cps"] // 2)
        for k in range(first):
            rider.phase(k)
    else:
        x1_ref, xn2_ref, rt_ref, cout_ref, run = rest

    @pl.when(pl.program_id(0) == 0)
    def _():
        run[...] = cin_ref[...]

    x1 = x_ref[...] + jnp.dot(mix_ref[...], wo_ref[...], preferred_element_type=F32)
    x1_ref[...] = x1
    xn2 = x1 * lax.rsqrt(jnp.mean(x1 * x1, axis=-1, keepdims=True) + EPS) * g2_ref[...]
    xn2_ref[...] = _pack_bf16_pairs(xn2)
    if scan:
        for k in range(first, scan["cps"]):
            rider.phase(k)
    hi, lo = _split_bf16(xn2)
    wrh = wrh_ref[...]
    logits = (jnp.dot(hi, wrh, preferred_element_type=F32) + jnp.dot(lo, wrh, preferred_element_type=F32)
              + jnp.dot(hi, wrl_ref[...], preferred_element_type=F32) + br_ref[...])
    tm = logits.shape[0]
    col = lax.broadcasted_iota(jnp.int32, (tm, LANES), 1).astype(F32)
    big = float(LANES)

    def first_argmax(vals):
        mx = jnp.max(vals, axis=1, keepdims=True)
        return mx, jnp.min(jnp.where(vals == mx, col, big), axis=1, keepdims=True)

    gmask = col < N_GROUPS
    gmax, gidx = first_argmax(jnp.where(gmask, logits, -jnp.inf))
    p_g = 1.0 / jnp.sum(jnp.where(gmask, jnp.exp(logits - gmax), 0.0), axis=1, keepdims=True)
    e_lo = N_GROUPS + gidx * EXPERTS_PER_GROUP
    el = jnp.where((col >= e_lo) & (col < e_lo + EXPERTS_PER_GROUP), logits, -jnp.inf)
    v1, i1 = first_argmax(el)
    v2, i2 = first_argmax(jnp.where(col == i1, -jnp.inf, el))
    e21 = jnp.exp(v2 - v1)
    p1 = 1.0 / (1.0 + e21)
    p2 = e21 / (1.0 + e21)
    e1 = i1 - N_GROUPS
    e2 = i2 - N_GROUPS

    oh1 = (col == e1).astype(F32)
    oh2 = (col == e2).astype(F32)
    both = oh1 + oh2
    ri = lax.broadcasted_iota(jnp.int32, (tm, tm), 0)
    ci = lax.broadcasted_iota(jnp.int32, (tm, tm), 1)
    lower = (ci < ri).astype(BF16)
    before = jnp.dot(lower, both.astype(BF16), preferred_element_type=F32) + run[...]
    pos1 = jnp.sum(before * oh1, axis=1, keepdims=True)
    pos2 = jnp.sum(before * oh2, axis=1, keepdims=True)
    new_run = run[...] + jnp.sum(both, axis=0, keepdims=True)
    run[...] = new_run
    cout_ref[...] = new_run

    rt = jnp.zeros((tm, LANES), F32)
    for lane, val in enumerate((e1, e2, p_g * p1, p_g * p2, pos1, pos2)):
        rt = jnp.where(col == lane, val, rt)
    rt_ref[...] = rt
    if scan:
        rider.finish()


def _out_proj(x, mix, w_out, g2, wr_hi, wr_lo, br, cnt_in, tm, page_table, cache_k=None, scan_part=None):
    n = x.shape[0]
    row = lambda i, pt: (i, 0)
    const = lambda i, pt: (0, 0)
    any_spec = pl.BlockSpec(memory_space=pl.ANY)
    scan, scan_out, scan_scratch = (None, None, [])
    if scan_part is not None:
        scan, scan_out, scan_scratch = _scan_plan(page_table, cache_k, n // tm, scan_part)
    grid_spec = pltpu.PrefetchScalarGridSpec(
        num_scalar_prefetch=1,
        grid=(n // tm,),
        in_specs=[pl.BlockSpec((tm, D_MODEL), row),
                  pl.BlockSpec((tm, D_MODEL), row),
                  _resident((D_MODEL, D_MODEL)),
                  _resident((1, D_MODEL)),
                  _resident((D_MODEL, LANES)),
                  _resident((D_MODEL, LANES)),
                  _resident((1, LANES)),
                  _resident((1, LANES))] + ([any_spec] if scan else []),
        out_specs=[pl.BlockSpec((tm, D_MODEL), row),
                   pl.BlockSpec((tm, D_MODEL // 2), row),
                   pl.BlockSpec((tm, LANES), row),
                   pl.BlockSpec((1, LANES), const)] + ([any_spec] if scan else []),
        scratch_shapes=[pltpu.VMEM((1, LANES), F32)] + scan_scratch)
    return pl.pallas_call(
        functools.partial(_outproj_kernel, scan=scan),
        grid_spec=grid_spec,
        out_shape=[jax.ShapeDtypeStruct((n, D_MODEL), F32),
                   jax.ShapeDtypeStruct((n, D_MODEL // 2), jnp.uint32),
                   jax.ShapeDtypeStruct((n, LANES), F32),
                   jax.ShapeDtypeStruct((1, LANES), F32)] + ([scan_out] if scan else []),
        compiler_params=_params(("arbitrary",)),
        name="out_proj",
    )(page_table.reshape(-1), x, mix, w_out, g2, wr_hi, wr_lo, br, cnt_in, *([cache_k] if scan else []))


def _dispatch_kernel(dest_ref, pad0_ref, padn_ref, tail_ref, xp_ref, xs_ref, xb_out, zbuf, sem, zsem, *,
                     tm, np_tiles, n_rows):
    i = pl.program_id(0)

    def issue_rows(x_ref):
        def row_copy(t, d):
            return pltpu.make_async_copy(x_ref.at[pl.ds(t, 1), :], xb_out.at[pl.ds(d, 1), :], sem)

        def issue(t, carry):
            base = (i * tm + t) * 2
            row_copy(t, dest_ref[base]).start()
            row_copy(t, dest_ref[base + 1]).start()
            return carry

        lax.fori_loop(0, tm, issue, 0, unroll=DMA_UNROLL)

    @pl.when(i < np_tiles)
    def _():
        issue_rows(xp_ref)

    @pl.when(i >= np_tiles)
    def _():
        issue_rows(xs_ref)

    @pl.when(i == 0)
    def _():
        zbuf[...] = jnp.zeros(zbuf.shape, zbuf.dtype)

        def zero_row(d):
            return pltpu.make_async_copy(zbuf.at[pl.ds(0, 1), :], xb_out.at[pl.ds(d, 1), :], zsem)

        def zero_blk(d):
            return pltpu.make_async_copy(zbuf, xb_out.at[pl.ds(d, MOE_ROWS), :], zsem)

        def each(n, fn):
            def body(r, carry):
                fn(r)
                return carry
            lax.fori_loop(0, n, body, 0)

        n_tail = (n_rows - tail_ref[0]) // MOE_ROWS
        for e in range(N_EXPERTS):
            each(padn_ref[e], lambda r, e=e: zero_row(pad0_ref[e] + r).start())
        each(n_tail, lambda j: zero_blk(pl.multiple_of(tail_ref[0] + j * MOE_ROWS, MOE_ROWS)).start())
        for e in range(N_EXPERTS):
            each(padn_ref[e], lambda r: zero_row(0).wait())
        each(n_tail, lambda j: zero_blk(0).wait())

    for _ in range(2):
        pltpu.make_async_copy(xp_ref, xb_out.at[pl.ds(0, tm), :], sem).wait()


def _dispatch(xn2_p, xn2_s, dest, pad0, padn, tail, n_rows, tm):
    np_tiles = xn2_p.shape[0] // tm
    ns_tiles = xn2_s.shape[0] // tm
    width = xn2_p.shape[1]
    grid_spec = pltpu.PrefetchScalarGridSpec(
        num_scalar_prefetch=4,
        grid=(np_tiles + ns_tiles,),
        in_specs=[pl.BlockSpec((tm, width), lambda i, *_: (jnp.minimum(i, np_tiles - 1), 0)),
                  pl.BlockSpec((tm, width), lambda i, *_: (jnp.maximum(i - np_tiles, 0), 0))],
        out_specs=pl.BlockSpec(memory_space=pl.ANY),
        scratch_shapes=[pltpu.VMEM((MOE_ROWS, width), xn2_p.dtype),
                        pltpu.SemaphoreType.DMA(()),
                        pltpu.SemaphoreType.DMA(())])
    return pl.pallas_call(
        functools.partial(_dispatch_kernel, tm=tm, np_tiles=np_tiles, n_rows=n_rows),
        grid_spec=grid_spec,
        out_shape=jax.ShapeDtypeStruct((n_rows, width), xn2_p.dtype),
        compiler_params=_params(("arbitrary",)),
        name="moe_dispatch",
    )(dest, pad0, padn, tail, xn2_p, xn2_s)


def _moe_kernel(be_ref, nu_ref, seg_ref, nxt_ref, x_ref, w1_hbm, w3_hbm, w2_hbm, o_ref,
                w1f, w3f, w2f, w1b, w3b, w2b, sem):
    g = pl.program_id(0)
    used = g < nu_ref[0]
    e = be_ref[g]
    new_expert = (g == 0) | (e != be_ref[jnp.maximum(g - 1, 0)])
    slot = seg_ref[g] % 2

    def fetch(expert, s):
        return [pltpu.make_async_copy(w1_hbm.at[expert], w1f.at[s], sem.at[s]),
                pltpu.make_async_copy(w3_hbm.at[expert], w3f.at[s], sem.at[s]),
                pltpu.make_async_copy(w2_hbm.at[expert], w2f.at[s], sem.at[s])]

    @pl.when(g == 0)
    def _():
        for cp in fetch(e, slot):
            cp.start()

    @pl.when(used & new_expert)
    def _():
        for cp in fetch(e, slot):
            cp.wait()
        w1b[...] = w1f[slot].astype(BF16)
        w3b[...] = w3f[slot].astype(BF16)
        w2b[...] = w2f[slot].astype(BF16)

        @pl.when(nxt_ref[e] < N_EXPERTS)
        def _():
            for cp in fetch(nxt_ref[e], 1 - slot):
                cp.start()

    @pl.when(used)
    def _():
        xb = _unpack_bf16_pairs(x_ref[...])
        a = jnp.dot(xb, w1b[...], preferred_element_type=F32)
        gt = jnp.dot(xb, w3b[...], preferred_element_type=F32)
        hmid = (a * _sigmoid(a) * gt).astype(BF16)
        o_ref[...] = jnp.dot(hmid, w2b[...], preferred_element_type=F32)

    @pl.when(jnp.logical_not(used))
    def _():
        o_ref[...] = jnp.zeros(o_ref.shape, F32)


def _moe_experts(xb, block_e, n_used, segment, next_expert, w1, w3, w2):
    n_blocks = xb.shape[0] // MOE_ROWS
    any_spec = pl.BlockSpec(memory_space=pl.ANY)
    grid_spec = pltpu.PrefetchScalarGridSpec(
        num_scalar_prefetch=4,
        grid=(n_blocks,),
        in_specs=[pl.BlockSpec((MOE_ROWS, xb.shape[1]), lambda g, be, nu, *_: (jnp.minimum(g, nu[0] - 1), 0)),
                  any_spec, any_spec, any_spec],
        out_specs=pl.BlockSpec((MOE_ROWS, D_MODEL), lambda g, *_: (g, 0)),
        scratch_shapes=[pltpu.VMEM((2, D_MODEL, D_EXPERT), F32),
                        pltpu.VMEM((2, D_MODEL, D_EXPERT), F32),
                        pltpu.VMEM((2, D_EXPERT, D_MODEL), F32),
                        pltpu.VMEM((D_MODEL, D_EXPERT), BF16),
                        pltpu.VMEM((D_MODEL, D_EXPERT), BF16),
                        pltpu.VMEM((D_EXPERT, D_MODEL), BF16),
                        pltpu.SemaphoreType.DMA((2,))])
    return pl.pallas_call(
        _moe_kernel,
        grid_spec=grid_spec,
        out_shape=jax.ShapeDtypeStruct((xb.shape[0], D_MODEL), F32),
        compiler_params=_params(("arbitrary",)),
        name="moe_experts",
    )(block_e, n_used, segment, next_expert, xb, w1, w3, w2)


def _combine_kernel(dest_ref, x1_ref, rt_ref, gf_ref, yb_hbm, o_ref, ybuf, sem, *, tm):
    i = pl.program_id(0)
    slot = i % 2

    def issue_tile(tile, s):
        def row_copy(k, t, d):
            return pltpu.make_async_copy(yb_hbm.at[pl.ds(d, 1), :], ybuf.at[s, k, pl.ds(t, 1), :], sem.at[s])

        def issue(t, carry):
            base = (tile * tm + t) * 2
            row_copy(0, t, dest_ref[base]).start()
            row_copy(1, t, dest_ref[base + 1]).start()
            return carry

        lax.fori_loop(0, tm, issue, 0, unroll=DMA_UNROLL)

    @pl.when(i == 0)
    def _():
        issue_tile(i, slot)

    @pl.when(i + 1 < pl.num_programs(0))
    def _():
        issue_tile(i + 1, 1 - slot)

    for k in range(2):
        pltpu.make_async_copy(yb_hbm.at[pl.ds(0, tm), :], ybuf.at[slot, k], sem.at[slot]).wait()
    rt = rt_ref[...]
    moe = ybuf[slot, 0] * rt[:, 2:3] + ybuf[slot, 1] * rt[:, 3:4]
    xo = x1_ref[...] + moe
    o_ref[...] = xo * lax.rsqrt(jnp.mean(xo * xo, axis=-1, keepdims=True) + EPS) * gf_ref[...]


def _combine(x1, route, gf, yb, dest, tm):
    n = x1.shape[0]
    grid_spec = pltpu.PrefetchScalarGridSpec(
        num_scalar_prefetch=1,
        grid=(n // tm,),
        in_specs=[pl.BlockSpec((tm, D_MODEL), lambda i, d: (i, 0)),
                  pl.BlockSpec((tm, LANES), lambda i, d: (i, 0)),
                  pl.BlockSpec((1, D_MODEL), lambda i, d: (0, 0)),
                  pl.BlockSpec(memory_space=pl.ANY)],
        out_specs=pl.BlockSpec((tm, D_MODEL), lambda i, d: (i, 0)),
        scratch_shapes=[pltpu.VMEM((2, 2, tm, D_MODEL), F32),
                        pltpu.SemaphoreType.DMA((2,))])
    return pl.pallas_call(
        functools.partial(_combine_kernel, tm=tm),
        grid_spec=grid_spec,
        out_shape=jax.ShapeDtypeStruct((n, D_MODEL), F32),
        compiler_params=_params(("arbitrary",)),
        name="moe_combine",
    )(dest, x1, route, gf, yb)


def _dispatch_plan(routes, counts):
    counts = counts[0, :N_EXPERTS].astype(jnp.int32)
    pcounts = (counts + MOE_ROWS - 1) // MOE_ROWS * MOE_ROWS
    pend = jnp.cumsum(pcounts)
    pstart = pend - pcounts
    dests = []
    n_assign = 0
    for rt in routes:
        e = rt[:, 0:2].astype(jnp.int32)
        pos = rt[:, 4:6].astype(jnp.int32)
        onehot = e[:, :, None] == jnp.arange(N_EXPERTS, dtype=jnp.int32)[None, None, :]
        dests.append((jnp.sum(jnp.where(onehot, pstart[None, None, :], 0), axis=-1) + pos).reshape(-1))
        n_assign += e.size
    n_blocks = (n_assign + N_EXPERTS * (MOE_ROWS - 1) + MOE_ROWS - 1) // MOE_ROWS
    first_row = jnp.arange(n_blocks, dtype=jnp.int32) * MOE_ROWS
    block_e = jnp.minimum(jnp.sum((pend[None, :] <= first_row[:, None]).astype(jnp.int32), axis=1), N_EXPERTS - 1)
    n_used = (pend[-1] // MOE_ROWS).astype(jnp.int32).reshape(1)
    pads = (pstart + counts, pcounts - counts, pend[-1:])
    changed = jnp.concatenate([jnp.zeros((1,), jnp.int32), (block_e[1:] != block_e[:-1]).astype(jnp.int32)])
    segment = jnp.cumsum(changed).astype(jnp.int32)
    ids = jnp.arange(N_EXPERTS, dtype=jnp.int32)
    later_nonempty = (ids[None, :] > ids[:, None]) & (counts[None, :] > 0)
    next_expert = jnp.min(jnp.where(later_nonempty, ids[None, :], N_EXPERTS), axis=1).astype(jnp.int32)
    return dests, block_e.astype(jnp.int32), n_used, n_blocks * MOE_ROWS, pads, segment, next_expert


def kernel(x_prompt, x_sample, cache_k, cache_v, state_conv, page_table, rel_bias, norm1_g, w_in, w_attn_out,
           conv_w, conv_b, conv_ln_g, conv_ln_b, w_conv_out, w_gate, b_gate, w_out, norm2_g, router_g_w,
           router_g_b, router_e_w, router_e_b, moe_w1, moe_w3, moe_w2, norm_f_g):
    depth = norm1_g.shape[0]
    assert depth == 1
    batch, seq, _ = x_prompt.shape
    db, s_new, _ = x_sample.shape
    n_pages = page_table.shape[1]
    page_size = cache_k.shape[2]
    past_len = n_pages * page_size
    assert seq % MOBA_BLOCK == 0 and past_len % MOBA_BLOCK == 0 and s_new <= MOBA_BLOCK and s_new % SUBLANES == 0
    own_s = past_len // MOBA_BLOCK
    n_p = batch * seq
    n_s = db * s_new
    tm = ROW_TILE
    assert n_p % tm == 0 and n_s % tm == 0

    wi_b = w_in[0].astype(BF16)
    wr = jnp.concatenate([router_g_w[0], router_e_w[0]], axis=1)
    wr = jnp.pad(wr, ((0, 0), (0, LANES - wr.shape[1])))
    wr_hi, wr_lo = _split_bf16(wr)
    br = jnp.pad(jnp.concatenate([router_g_b[0], router_e_b[0]]), (0, LANES - N_GROUPS - N_EXPERTS))[None, :]
    far_bias = rel_bias[NUM_BUCKETS - 1, :]
    wg_b, wa_b, wc_b, wo_b = (w[0].astype(BF16) for w in (w_gate, w_attn_out, w_conv_out, w_out))
    cw = conv_w[0]
    bias = _bias_tiles(rel_bias)

    ck, cv = cache_k[0], cache_v[0]
    xn_p, q_p, kc_p, vc_p, u_p, ksum, k_p, v_p = _proj(
        x_prompt.reshape(n_p, D_MODEL), norm1_g, wi_b, tm, BF16, page_table)
    xn_s, q_s, kc_s, vc_s, u_s, _, k_s, v_s = _proj(x_sample.reshape(n_s, D_MODEL), norm1_g, wi_b, tm, F32, page_table)

    nb = seq // MOBA_BLOCK
    km = ksum.reshape(batch, nb, N_HEADS, HEAD_DIM) / MOBA_BLOCK
    km = jnp.pad(km.transpose(0, 2, 1, 3), ((0, 0), (0, 0), (0, LANES - nb), (0, 0)))
    attn_p = _prompt_attention(q_p, kc_p, vc_p, km, bias, far_bias, batch, seq)

    tt = 512
    tiles = seq // tt
    per_halo = tt // CONV_HALO
    cva_p, bsum_b = _conv_module(u_p, lambda b, i, pt: (jnp.maximum((b * tiles + i) * per_halo - 1, 0), 0), u_p,
                                 cw, conv_b, conv_ln_g, conv_ln_b, batch, tiles, tt, True, BF16, page_table, ck, (0, 2))
    lead = CONV_HALO - (CONV_WIDTH - 1)
    halo_s = jnp.pad(state_conv[0], ((0, 0), (lead, 0), (0, 0))).reshape(db * CONV_HALO, CONV_CH)
    cva_s, = _conv_module(halo_s, lambda b, i, pt: (b, 0), u_s,
                          cw, conv_b, conv_ln_g, conv_ln_b, db, 1, s_new, False, F32, page_table)

    mix_p, bsum_c = _gate_mix(xn_p, attn_p, cva_p, wg_b, b_gate, wa_b, wc_b, tm, page_table, ck, (2, 2))
    zero_cnt = jnp.zeros((1, LANES), F32)
    x1_p, xn2_p, rt_p, cnt_p = _out_proj(x_prompt.reshape(n_p, D_MODEL), mix_p, wo_b, norm2_g, wr_hi, wr_lo,
                                         br, zero_cnt, tm, page_table)

    bsum = jnp.concatenate([bsum_b, bsum_c], axis=0)
    k_new = k_s.reshape(db, s_new, N_HEADS, HEAD_DIM)
    sums = jnp.concatenate([bsum, jnp.sum(k_new, axis=1)[:, None]], axis=1)
    km_s = jnp.pad((sums / MOBA_BLOCK).transpose(0, 2, 1, 3), ((0, 0), (0, 0), (0, LANES - own_s - 1), (0, 0)))
    sel_idx = _sample_select(q_s, km_s, db, s_new, own_s)[..., :MOBA_TOPK]
    attn_s = _sample_attention(q_s, kc_s, vc_s, ck, cv, page_table, sel_idx, bias, far_bias, s_new, own_s)

    mix_s, = _gate_mix(xn_s, attn_s, cva_s, wg_b, b_gate, wa_b, wc_b, tm, page_table)
    x1_s, xn2_s, rt_s, cnt = _out_proj(x_sample.reshape(n_s, D_MODEL), mix_s, wo_b, norm2_g, wr_hi, wr_lo, br,
                                       cnt_p, tm, page_table)

    (dest_p, dest_s), block_e, n_used, n_rows, pads, segment, next_expert = _dispatch_plan((rt_p, rt_s), cnt)
    xb = _dispatch(xn2_p, xn2_s, jnp.concatenate([dest_p, dest_s]), *pads, n_rows, tm)
    yb = _moe_experts(xb, block_e, n_used, segment, next_expert, moe_w1[0], moe_w3[0], moe_w2[0])
    gf = norm_f_g[None, :]
    y_p = _combine(x1_p, rt_p, gf, yb, dest_p, tm)
    y_s = _combine(x1_s, rt_s, gf, yb, dest_s, tm)

    conv_prompt = u_p.reshape(batch, seq, CONV_CH)[:, seq - (CONV_WIDTH - 1):]
    conv_sample = jnp.concatenate([state_conv[0], u_s.reshape(db, s_new, CONV_CH)], axis=1)[:, -(CONV_WIDTH - 1):]
    shp_p = (1, batch, seq, N_HEADS, HEAD_DIM)
    shp_s = (1, db, s_new, N_HEADS, HEAD_DIM)
    return (y_p.reshape(batch, seq, D_MODEL), y_s.reshape(db, s_new, D_MODEL),
            k_p.reshape(shp_p), v_p.reshape(shp_p), conv_prompt[None],
            k_s.reshape(shp_s), v_s.reshape(shp_s), conv_sample[None])
```
